```python
import jax, jax.numpy as jnp
from jax import lax
import numpy as np

D_MODEL = 1024
BATCH = 32
SEQ = 2048
DEPTH = 1

D_MIX = D_MODEL
CONV_CH = 512
CONV_WIDTH = 3
N_HEADS = 8
HEAD_DIM = 64
N_KV_HEADS = 2
Q_PER_KV = N_HEADS // N_KV_HEADS
KV_DIM = N_KV_HEADS * HEAD_DIM
N_BRANCH = 3
CMP_LEN = 32
CMP_STRIDE = 16
SEL_BLOCK = 64
SEL_TOPK = 8
WINDOW = 512
NSA_Q_BLOCK = 64
FORCED_SCORE = 1e4
N_EXPERTS = 256
TOP_K = 8
N_GROUPS = 8
TOPK_GROUPS = 4
EXPERT_HIDDEN = 256
SHARED_HIDDEN = 256
ROUTED_SCALE = 2.5
DISPATCH_BLOCK = 128
LN_EPS = 1e-5
ALPHA = (2.0 * DEPTH) ** 0.25
BETA = (8.0 * DEPTH) ** -0.25
NEG_INF = -1e30
SPLIT_WIDTHS = (CONV_CH, CONV_CH, CONV_CH, N_HEADS * HEAD_DIM,
                KV_DIM, KV_DIM, KV_DIM, KV_DIM, KV_DIM, KV_DIM, N_HEADS * N_BRANCH)
D_IN_PROJ = 3 * CONV_CH + N_HEADS * HEAD_DIM + 6 * KV_DIM + N_HEADS * N_BRANCH

kernel_name = "hybrid_conv_nsa_moe_deepnorm"


def layer_norm(x, g, b):
    xf = x.astype(jnp.float32)
    mu = xf.mean(-1, keepdims=True)
    var = jnp.square(xf - mu).mean(-1, keepdims=True)
    return ((xf - mu) * lax.rsqrt(var + LN_EPS) * g.astype(jnp.float32)
            + b.astype(jnp.float32)).astype(x.dtype)


def masked_softmax(s, mask):
    s = jnp.where(mask, s.astype(jnp.float32), NEG_INF)
    return jax.nn.softmax(s, axis=-1) * mask


def alibi_slopes():
    return 2.0 ** (-8.0 * jnp.arange(1, N_HEADS + 1, dtype=jnp.float32) / N_HEADS)


def swiglu(x, wg, wu, wd):
    return (jax.nn.silu(x @ wg) * (x @ wu)) @ wd


def short_conv_mixer(b_gate, c_gate, u_in, conv_w):
    u = c_gate * u_in
    S = u.shape[1]
    up = jnp.pad(u, ((0, 0), (CONV_WIDTH - 1, 0), (0, 0)))
    y = sum(conv_w[j] * up[:, j:j + S] for j in range(CONV_WIDTH))
    return b_gate * y


def compress_blocks(blk, pos, w1, b1, w2):
    B, n = blk.shape[:2]
    blk = blk + pos[None, None, :, None, :]
    blk = blk.transpose(0, 1, 3, 2, 4).reshape(B, n, N_KV_HEADS, CMP_LEN * HEAD_DIM)
    return jax.nn.gelu(blk @ w1 + b1) @ w2


def native_sparse_attention(q, kc_raw, vc_raw, ks, vs, kw, vw, gate_logits,
                            ck_pos, ck_w1, ck_b1, ck_w2, cv_pos, cv_w1, cv_b1, cv_w2):
    B, S = q.shape[:2]
    G, R, Dh, Q = N_KV_HEADS, Q_PER_KV, HEAD_DIM, NSA_Q_BLOCK
    n_cmp = (S - CMP_LEN) // CMP_STRIDE + 1
    n_sel = S // SEL_BLOCK
    top_n = min(SEL_TOPK, n_sel)
    slopes = alibi_slopes().reshape(G, R)
    qg = q.reshape(B, S, G, R, Dh) * (HEAD_DIM ** -0.5)
    gates = jax.nn.sigmoid(gate_logits.astype(jnp.float32)).astype(q.dtype).reshape(B, S, G, R, N_BRANCH)
    kc_raw, vc_raw, ks, vs, kw, vw = [a.reshape(B, S, G, Dh) for a in (kc_raw, vc_raw, ks, vs, kw, vw)]

    cmp_start = jnp.arange(n_cmp) * CMP_STRIDE
    cmp_end = cmp_start + CMP_LEN - 1
    blk_idx = cmp_start[:, None] + jnp.arange(CMP_LEN)[None, :]
    kc = compress_blocks(kc_raw[:, blk_idx], ck_pos, ck_w1, ck_b1, ck_w2)
    vc = compress_blocks(vc_raw[:, blk_idx], cv_pos, cv_w1, cv_b1, cv_w2)

    sel_start = jnp.arange(n_sel) * SEL_BLOCK
    overlap = ((cmp_start[:, None] < sel_start[None, :] + SEL_BLOCK)
               & (cmp_end[:, None] >= sel_start[None, :])).astype(jnp.float32)

    ks_blocks = ks.reshape(B, n_sel, SEL_BLOCK, G, Dh).transpose(0, 3, 1, 2, 4)
    vs_blocks = vs.reshape(B, n_sel, SEL_BLOCK, G, Dh).transpose(0, 3, 1, 2, 4)
    kw_pad = jnp.pad(kw, ((0, 0), (WINDOW, 0), (0, 0), (0, 0)))
    vw_pad = jnp.pad(vw, ((0, 0), (WINDOW, 0), (0, 0), (0, 0)))
    b_ix = jnp.arange(B)[:, None, None, None]
    g_ix = jnp.arange(G)[None, :, None, None]
    blk_ids = jnp.arange(n_sel)

    def query_block(qi):
        q0 = qi * Q
        t = q0 + jnp.arange(Q)
        qb = lax.dynamic_slice_in_dim(qg, q0, Q, axis=1)
        gb = lax.dynamic_slice_in_dim(gates, q0, Q, axis=1)

        d_c = t[:, None] - cmp_end[None, :]
        s_c = jnp.einsum('bqgrd,bngd->bgrqn', qb, kc).astype(jnp.float32)
        s_c = s_c - slopes[:, :, None, None] * d_c.astype(jnp.float32)
        p_cmp = masked_softmax(s_c, d_c >= 0)
        o_cmp = jnp.einsum('bgrqn,bngd->bqgrd', p_cmp.astype(vc.dtype), vc)

        imp = jnp.einsum('bgrqn,nj->bgqj', p_cmp, overlap)
        forced = (blk_ids[None, :] == 0) | (blk_ids[None, :] == (t // SEL_BLOCK)[:, None])
        causal_blk = sel_start[None, :] <= t[:, None]
        score = jnp.where(forced, FORCED_SCORE, jnp.where(causal_blk, imp, -1.0))
        _, sel = lax.top_k(score, top_n)
        k_g = ks_blocks[b_ix, g_ix, sel]
        v_g = vs_blocks[b_ix, g_ix, sel].reshape(B, G, Q, top_n * SEL_BLOCK, Dh)
        pos = sel[..., None] * SEL_BLOCK + jnp.arange(SEL_BLOCK)
        d_s = (t[None, None, :, None, None] - pos)[:, :, None]
        s_s = jnp.einsum('bqgrd,bgqnld->bgrqnl', qb, k_g).astype(jnp.float32)
        s_s = s_s - slopes[None, :, :, None, None, None] * d_s.astype(jnp.float32)
        p_s = masked_softmax(s_s.reshape(B, G, R, Q, top_n * SEL_BLOCK),
                             (d_s >= 0).reshape(B, G, 1, Q, top_n * SEL_BLOCK))
        o_slc = jnp.einsum('bgrqm,bgqmd->bqgrd', p_s.astype(v_g.dtype), v_g)

        kwb = lax.dynamic_slice_in_dim(kw_pad, q0, WINDOW + Q, axis=1)
        vwb = lax.dynamic_slice_in_dim(vw_pad, q0, WINDOW + Q, axis=1)
        pos_w = q0 - WINDOW + jnp.arange(WINDOW + Q)
        d_w = t[:, None] - pos_w[None, :]
        m_w = (d_w >= 0) & (d_w < WINDOW) & (pos_w[None, :] >= 0)
        s_w = jnp.einsum('bqgrd,bkgd->bgrqk', qb, kwb).astype(jnp.float32)
        s_w = s_w - slopes[:, :, None, None] * d_w.astype(jnp.float32)
        p_w = masked_softmax(s_w, m_w)
        o_win = jnp.einsum('bgrqk,bkgd->bqgrd', p_w.astype(vwb.dtype), vwb)

        o = gb[..., 0:1] * o_cmp + gb[..., 1:2] * o_slc + gb[..., 2:3] * o_win
        return o.reshape(B, Q, N_HEADS * Dh).astype(q.dtype)

    out = lax.map(query_block, jnp.arange(S // Q))
    return out.transpose(1, 0, 2, 3).reshape(B, S, N_HEADS * Dh)


def moe_ffn(x, router_w, router_bias, w_gate, w_up, w_down, ws_gate, ws_up, ws_down):
    B, S, D = x.shape
    T = B * S
    xf = x.reshape(T, D)
    shared = swiglu(xf, ws_gate, ws_up, ws_down)

    scores = jax.nn.sigmoid((xf @ router_w).astype(jnp.float32))
    biased = scores + router_bias.astype(jnp.float32)
    grp = biased.reshape(T, N_GROUPS, N_EXPERTS // N_GROUPS)
    grp_score = lax.top_k(grp, 2)[0].sum(-1)
    _, top_grp = lax.top_k(grp_score, TOPK_GROUPS)
    grp_mask = jax.nn.one_hot(top_grp, N_GROUPS).sum(-2) > 0
    expert_mask = jnp.repeat(grp_mask, N_EXPERTS // N_GROUPS, axis=-1)
    _, top_e = lax.top_k(jnp.where(expert_mask, biased, NEG_INF), TOP_K)
    w = jnp.take_along_axis(scores, top_e, axis=-1)
    w = w / w.sum(-1, keepdims=True) * ROUTED_SCALE

    A = T * TOP_K
    flat_e = top_e.reshape(A)
    flat_tok = jnp.repeat(jnp.arange(T, dtype=jnp.int32), TOP_K)
    flat_w = w.reshape(A).astype(x.dtype)
    order = jnp.argsort(flat_e)
    e_s, tok_s, w_s = flat_e[order], flat_tok[order], flat_w[order]
    counts = jnp.bincount(flat_e, length=N_EXPERTS)
    padded = (counts + DISPATCH_BLOCK - 1) // DISPATCH_BLOCK * DISPATCH_BLOCK
    start = jnp.cumsum(counts) - counts
    pad_end = jnp.cumsum(padded)
    pad_start = pad_end - padded
    dest = pad_start[e_s] + jnp.arange(A) - start[e_s]
    n_blocks = -(-(A + N_EXPERTS * (DISPATCH_BLOCK - 1)) // DISPATCH_BLOCK)
    P = n_blocks * DISPATCH_BLOCK
    tok_pad = jnp.zeros((P,), jnp.int32).at[dest].set(tok_s)
    w_pad = jnp.zeros((P,), x.dtype).at[dest].set(w_s)
    blk_e = jnp.minimum(jnp.searchsorted(pad_end, jnp.arange(n_blocks) * DISPATCH_BLOCK,
                                         side='right'), N_EXPERTS - 1)

    def expert_block(y, inp):
        tok, wt, e = inp
        xb = xf[tok]
        out = swiglu(xb, w_gate[e], w_up[e], w_down[e]) * wt[:, None]
        return y.at[tok].add(out.astype(y.dtype)), None

    routed, _ = lax.scan(expert_block, jnp.zeros_like(xf),
                         (tok_pad.reshape(n_blocks, DISPATCH_BLOCK),
                          w_pad.reshape(n_blocks, DISPATCH_BLOCK), blk_e))
    return (shared + routed).reshape(B, S, D)


def setup_inputs(seed: int = 0) -> dict:
    key = jax.random.key(seed)
    ks = jax.random.split(key, 24)
    f32 = jnp.float32
    L = DEPTH

    def nrm(k, shape, scale):
        return jax.random.normal(k, shape, f32) * scale

    return {
        "x": nrm(ks[0], (BATCH, SEQ, D_MODEL), 1.0),
        "w_in": nrm(ks[1], (L, D_MODEL, D_IN_PROJ), D_MODEL ** -0.5),
        "conv_w": nrm(ks[2], (L, CONV_WIDTH, CONV_CH), CONV_WIDTH ** -0.5),
        "ck_pos": nrm(ks[3], (L, CMP_LEN, HEAD_DIM), 0.1),
        "ck_w1": nrm(ks[4], (L, CMP_LEN * HEAD_DIM, HEAD_DIM), (CMP_LEN * HEAD_DIM) ** -0.5),
        "ck_b1": nrm(ks[5], (L, HEAD_DIM), 0.02),
        "ck_w2": nrm(ks[6], (L, HEAD_DIM, HEAD_DIM), HEAD_DIM ** -0.5),
        "cv_pos": nrm(ks[7], (L, CMP_LEN, HEAD_DIM), 0.1),
        "cv_w1": nrm(ks[8], (L, CMP_LEN * HEAD_DIM, HEAD_DIM), (CMP_LEN * HEAD_DIM) ** -0.5),
        "cv_b1": nrm(ks[9], (L, HEAD_DIM), 0.02),
        "cv_w2": nrm(ks[10], (L, HEAD_DIM, HEAD_DIM), HEAD_DIM ** -0.5),
        "w_out": nrm(ks[11], (L, D_MIX, D_MODEL), D_MIX ** -0.5 * BETA),
        "ln1_g": 1.0 + nrm(ks[12], (L, D_MODEL), 0.02),
        "ln1_b": nrm(ks[13], (L, D_MODEL), 0.02),
        "router_w": nrm(ks[14], (L, D_MODEL, N_EXPERTS), D_MODEL ** -0.5),
        "router_bias": nrm(ks[15], (L, N_EXPERTS), 0.01),
        "w_gate": nrm(ks[16], (L, N_EXPERTS, D_MODEL, EXPERT_HIDDEN), D_MODEL ** -0.5),
        "w_up": nrm(ks[17], (L, N_EXPERTS, D_MODEL, EXPERT_HIDDEN), D_MODEL ** -0.5),
        "w_down": nrm(ks[18], (L, N_EXPERTS, EXPERT_HIDDEN, D_MODEL), EXPERT_HIDDEN ** -0.5 * BETA),
        "ws_gate": nrm(ks[19], (L, D_MODEL, SHARED_HIDDEN), D_MODEL ** -0.5),
        "ws_up": nrm(ks[20], (L, D_MODEL, SHARED_HIDDEN), D_MODEL ** -0.5),
        "ws_down": nrm(ks[21], (L, SHARED_HIDDEN, D_MODEL), SHARED_HIDDEN ** -0.5 * BETA),
        "ln2_g": 1.0 + nrm(ks[22], (L, D_MODEL), 0.02),
        "ln2_b": nrm(ks[23], (L, D_MODEL), 0.02),
    }


def reference(x, w_in, conv_w, ck_pos, ck_w1, ck_b1, ck_w2, cv_pos, cv_w1, cv_b1, cv_w2,
              w_out, ln1_g, ln1_b, router_w, router_bias, w_gate, w_up, w_down,
              ws_gate, ws_up, ws_down, ln2_g, ln2_b):
    offsets = []
    acc = 0
    for w in SPLIT_WIDTHS[:-1]:
        acc += w
        offsets.append(acc)
    for l in range(DEPTH):
        proj = x @ w_in[l]
        b_g, c_g, u_in, q, kc, vc, ksl, vsl, kwn, vwn, g_lg = jnp.split(proj, offsets, axis=-1)
        conv_out = short_conv_mixer(b_g, c_g, u_in, conv_w[l])
        nsa_out = native_sparse_attention(q, kc, vc, ksl, vsl, kwn, vwn, g_lg,
                                          ck_pos[l], ck_w1[l], ck_b1[l], ck_w2[l],
                                          cv_pos[l], cv_w1[l], cv_b1[l], cv_w2[l])
        mix = jnp.concatenate([conv_out, nsa_out], axis=-1) @ w_out[l]
        x = layer_norm(ALPHA * x + mix, ln1_g[l], ln1_b[l])
        ffn = moe_ffn(x, router_w[l], router_bias[l], w_gate[l], w_up[l], w_down[l],
                      ws_gate[l], ws_up[l], ws_down[l])
        x = layer_norm(ALPHA * x + ffn, ln2_g[l], ln2_b[l])
    return x
```

```python
import functools
import math

import jax
import jax.numpy as jnp
import numpy as np
from jax import lax
from jax.experimental import pallas as pl
from jax.experimental.pallas import tpu as pltpu

F32 = jnp.float32
BF16 = jnp.bfloat16
I32 = jnp.int32

CONV_CH = 512
CONV_WIDTH = 3
N_HEADS = 8
HEAD_DIM = 64
N_KV_HEADS = 2
Q_PER_KV = N_HEADS // N_KV_HEADS
KV_DIM = N_KV_HEADS * HEAD_DIM
N_BRANCH = 3
CMP_LEN = 32
CMP_STRIDE = 16
SEL_BLOCK = 64
SEL_TOPK = 8
WINDOW = 512
FORCED_SCORE = 1e4
N_EXPERTS = 256
TOP_K = 8
N_GROUPS = 8
TOPK_GROUPS = 4
GROUP_SIZE = N_EXPERTS // N_GROUPS
ROUTED_SCALE = 2.5
LN_EPS = 1e-5
NEG_INF = -1e30
SEL_SHIFT = SEL_BLOCK.bit_length() - 1
GROUP_SHIFT = GROUP_SIZE.bit_length() - 1
TOPK_SHIFT = TOP_K.bit_length() - 1

LANES = 128
SUBLANES = 8
VMEM_LIMIT = 56 * 1024 * 1024

PROJ_ROWS = 512
NSA_Q = 128
NSA_KC = 512
POST_ROWS = 256
SLOT_BLOCK = 256
PUSH_ROWS = 128
COMB_ROWS = 128


def _dot(a, b):
    return jnp.dot(a, b, preferred_element_type=F32)


def _dot_t(a, b):
    return lax.dot_general(a, b, (((1,), (1,)), ((), ())), preferred_element_type=F32)


def _split_bf16(x):
    hi = x.astype(BF16)
    lo = (x - hi.astype(F32)).astype(BF16)
    return hi, lo


def _cparams(sem):
    return pltpu.CompilerParams(dimension_semantics=sem, vmem_limit_bytes=VMEM_LIMIT)


def _proj_conv_kernel(x_ref, wc_ref, wq_ref, wkv_ref, wg_ref, cw_ref,
                      conv_ref, q_ref, kv_ref, gate_ref, carry_ref):
    rows = x_ref.shape[0]

    @pl.when(pl.program_id(1) == 0)
    def _():
        carry_ref[...] = jnp.zeros_like(carry_ref)

    xb = x_ref[...].astype(BF16)
    acc = _dot(xb, wc_ref[...])
    b_g = acc[:, :CONV_CH]
    u = acc[:, CONV_CH:2 * CONV_CH] * acc[:, 2 * CONV_CH:]
    prev2 = carry_ref[SUBLANES - 2:SUBLANES - 1, :]
    prev1 = carry_ref[SUBLANES - 1:SUBLANES, :]
    ri = lax.broadcasted_iota(I32, (rows, 1), 0)
    u1 = jnp.where(ri == 0, prev1, pltpu.roll(u, 1, 0))
    u2 = jnp.where(ri == 0, prev2, jnp.where(ri == 1, prev1, pltpu.roll(u, 2, 0)))
    y = cw_ref[0:1, :] * u2 + cw_ref[1:2, :] * u1 + cw_ref[2:3, :] * u
    conv_ref[...] = (b_g * y).astype(BF16)
    carry_ref[...] = u[rows - SUBLANES:, :]

    q_ref[...] = (_dot(xb, wq_ref[...]) * (HEAD_DIM ** -0.5)).astype(BF16)
    kv_ref[...] = _dot(xb, wkv_ref[...]).astype(BF16)
    gate_ref[...] = jax.nn.sigmoid(_dot(xb, wg_ref[...]))


def _proj_conv(x2, w_conv, w_q, w_kv, w_g, conv_w, batch, seq):
    T, D = x2.shape
    rows = min(PROJ_ROWS, seq)
    nt = seq // rows
    row_map = lambda b, i: (b * nt + i, 0)
    fixed = lambda b, i: (0, 0)
    return pl.pallas_call(
        _proj_conv_kernel,
        grid=(batch, nt),
        in_specs=[
            pl.BlockSpec((rows, D), row_map),
            pl.BlockSpec(w_conv.shape, fixed),
            pl.BlockSpec(w_q.shape, fixed),
            pl.BlockSpec(w_kv.shape, fixed),
            pl.BlockSpec(w_g.shape, fixed),
            pl.BlockSpec(conv_w.shape, fixed),
        ],
        out_specs=[
            pl.BlockSpec((rows, CONV_CH), row_map),
            pl.BlockSpec((rows, N_HEADS * HEAD_DIM), row_map),
            pl.BlockSpec((rows, 6 * KV_DIM), row_map),
            pl.BlockSpec((rows, LANES), row_map),
        ],
        out_shape=[
            jax.ShapeDtypeStruct((T, CONV_CH), BF16),
            jax.ShapeDtypeStruct((T, N_HEADS * HEAD_DIM), BF16),
            jax.ShapeDtypeStruct((T, 6 * KV_DIM), BF16),
            jax.ShapeDtypeStruct((T, LANES), F32),
        ],
        scratch_shapes=[pltpu.VMEM((SUBLANES, CONV_CH), F32)],
        compiler_params=_cparams(("arbitrary", "arbitrary")),
        name="proj_conv",
    )(x2, w_conv, w_q, w_kv, w_g, conv_w)


def _compress_kernel(ck_ref, cv_ref, wtk_ref, wbk_ref, w2k_ref, ptk_ref, pbk_ref, b1k_ref,
                     wtv_ref, wbv_ref, w2v_ref, ptv_ref, pbv_ref, b1v_ref, kc_ref, vc_ref):
    def one(c_ref, wt_ref, wb_ref, w2_ref, pt_ref, pb_ref, b1_ref, o_ref):
        c = c_ref[0]
        top = _dot(c, wt_ref[...])
        bot = _dot(c, wb_ref[...])
        c0 = _dot(pt_ref[...], wt_ref[...]) + _dot(pb_ref[...], wb_ref[...]) + b1_ref[...]
        n = top.shape[0]
        h = top + pltpu.roll(bot, n - 1, 0) + c0[0:1, :]
        g = jax.nn.gelu(h, approximate=True)
        o_ref[0] = _dot(g.astype(BF16), w2_ref[...]).astype(BF16)

    one(ck_ref, wtk_ref, wbk_ref, w2k_ref, ptk_ref, pbk_ref, b1k_ref, kc_ref)
    one(cv_ref, wtv_ref, wbv_ref, w2v_ref, ptv_ref, pbv_ref, b1v_ref, vc_ref)


def _blockdiag2(w):
    z = jnp.zeros_like(w)
    return jnp.concatenate([jnp.concatenate([w, z], 1), jnp.concatenate([z, w], 1)], 0)


def _compress_weights(pos, w1, b1, w2):
    w1r = w1.reshape(CMP_LEN, HEAD_DIM, HEAD_DIM)
    eye = jnp.eye(N_KV_HEADS, dtype=w1.dtype)
    wfull = (w1r[:, None, :, None, :] * eye[None, :, None, :, None]).reshape(CMP_LEN, KV_DIM, KV_DIM)
    w_top = wfull[:CMP_STRIDE].reshape(CMP_STRIDE * KV_DIM, KV_DIM).astype(BF16)
    w_bot = wfull[CMP_STRIDE:].reshape(CMP_STRIDE * KV_DIM, KV_DIM).astype(BF16)
    posr = jnp.tile(pos, (1, N_KV_HEADS))
    pos_top = jnp.tile(posr[:CMP_STRIDE].reshape(1, -1), (SUBLANES, 1)).astype(BF16)
    pos_bot = jnp.tile(posr[CMP_STRIDE:].reshape(1, -1), (SUBLANES, 1)).astype(BF16)
    b1r = jnp.tile(b1[None, :], (SUBLANES, N_KV_HEADS)).astype(F32)
    return w_top, w_bot, _blockdiag2(w2).astype(BF16), pos_top, pos_bot, b1r


def _compress(kc_raw, vc_raw, wk, wv, batch, seq):
    chunks = seq // CMP_STRIDE
    width = CMP_STRIDE * KV_DIM
    ck = kc_raw.reshape(batch, chunks, width)
    cv = vc_raw.reshape(batch, chunks, width)
    bmap = lambda b: (b, 0, 0)
    fixed = lambda b: (0, 0)
    wspecs = [pl.BlockSpec(w.shape, fixed) for w in wk]
    return pl.pallas_call(
        _compress_kernel,
        grid=(batch,),
        in_specs=[pl.BlockSpec((1, chunks, width), bmap), pl.BlockSpec((1, chunks, width), bmap)]
        + wspecs + wspecs,
        out_specs=[pl.BlockSpec((1, chunks, KV_DIM), bmap)] * 2,
        out_shape=[jax.ShapeDtypeStruct((batch, chunks, KV_DIM), BF16)] * 2,
        compiler_params=_cparams(("arbitrary",)),
        name="compress",
    )(ck, cv, *wk, *wv)


def _softmax_rows(s, valid):
    s = jnp.where(valid, s, NEG_INF)
    m = jnp.max(s, axis=-1, keepdims=True)
    p = jnp.where(valid, jnp.exp(s - m), 0.0)
    l = jnp.sum(p, axis=-1, keepdims=True)
    inv = jnp.where(l > 0.0, 1.0 / l, 0.0)
    return p, inv


def _nsa_kernel(q_ref, kc_ref, vc_ref, ks_ref, vs_ref, kw_ref, vw_ref, gate_ref, ovl_ref,
                o_ref, *, seq, n_sel):
    tq = q_ref.shape[0]
    ncp = kc_ref.shape[1]
    rows = Q_PER_KV * tq
    q0 = pl.program_id(1) * tq
    t_col = q0 + lax.broadcasted_iota(I32, (tq, 1), 0)
    t4 = jnp.concatenate([t_col] * Q_PER_KV, axis=0)
    row_i = lax.broadcasted_iota(I32, (rows, 1), 0)
    lane = lax.broadcasted_iota(I32, (1, LANES), 1)
    lane_f = lane.astype(F32)
    gates = gate_ref[...]
    win_len = WINDOW + tq
    w_start = pl.multiple_of(jnp.maximum(q0 - WINDOW, 0), tq)
    n_chunks = (q0 + tq + NSA_KC - 1) // NSA_KC
    outs = []
    for g in range(N_KV_HEADS):
        lo, hi = g * HEAD_DIM, (g + 1) * HEAD_DIM
        qg = jnp.concatenate(
            [q_ref[:, (g * Q_PER_KV + r) * HEAD_DIM:(g * Q_PER_KV + r + 1) * HEAD_DIM]
             for r in range(Q_PER_KV)], axis=0)
        slope = jnp.zeros((rows, 1), F32)
        for r in range(Q_PER_KV):
            h = g * Q_PER_KV + r
            in_head = (row_i >= r * tq) & (row_i < (r + 1) * tq)
            slope = jnp.where(in_head, 2.0 ** (-8.0 * (h + 1) / N_HEADS), slope)

        cmp_end = lax.broadcasted_iota(I32, (1, ncp), 1) * CMP_STRIDE + (CMP_LEN - 1)
        d_c = t4 - cmp_end
        s_c = _dot_t(qg, kc_ref[0, :, lo:hi]) - slope * d_c.astype(F32)
        p_c, inv_c = _softmax_rows(s_c, d_c >= 0)
        p_c = p_c * inv_c
        o_cmp = _dot(p_c.astype(BF16), vc_ref[0, :, lo:hi])

        ps = p_c[0:tq]
        for r in range(1, Q_PER_KV):
            ps = ps + p_c[r * tq:(r + 1) * tq]
        ps_hi, ps_lo = _split_bf16(ps)
        imp = _dot(ps_hi, ovl_ref[...]) + _dot(ps_lo, ovl_ref[...])
        forced = (lane == 0) | (lane == jnp.right_shift(t_col, SEL_SHIFT))
        causal = lane * SEL_BLOCK <= t_col
        score = jnp.where(forced, FORCED_SCORE, jnp.where(causal, imp, -1.0))
        score = jnp.where(lane < n_sel, score, -jnp.inf)
        sel = jnp.zeros((tq, LANES), F32)
        for _ in range(min(SEL_TOPK, n_sel)):
            mx = jnp.max(score, axis=-1, keepdims=True)
            first = jnp.min(jnp.where(score == mx, lane_f, float(LANES)), axis=-1, keepdims=True)
            hit = lane_f == first
            sel = jnp.where(hit, 1.0, sel)
            score = jnp.where(hit, -jnp.inf, score)
        sel_b = sel.astype(BF16)

        def sel_chunk(c, carry):
            m, l, acc = carry
            k0 = pl.multiple_of(c * NSA_KC, NSA_KC)
            kch = ks_ref[0, pl.ds(k0, NSA_KC), lo:hi]
            vch = vs_ref[0, pl.ds(k0, NSA_KC), lo:hi]
            pos = k0 + lax.broadcasted_iota(I32, (1, NSA_KC), 1)
            s = _dot_t(qg, kch) + slope * (pos - q0).astype(F32)
            blk = lax.broadcasted_iota(I32, (LANES, NSA_KC), 0)
            kblk = jnp.right_shift(k0 + lax.broadcasted_iota(I32, (LANES, NSA_KC), 1), SEL_SHIFT)
            expand = jnp.where(blk == kblk, 1.0, 0.0).astype(BF16)
            picked = _dot(sel_b, expand)
            ok = jnp.where((picked > 0.5) & (pos <= t_col), 1.0, 0.0)
            ok4 = jnp.concatenate([ok] * Q_PER_KV, axis=0) > 0.5
            s = jnp.where(ok4, s, NEG_INF)
            m_new = jnp.maximum(m, jnp.max(s, axis=-1, keepdims=True))
            a = jnp.exp(m - m_new)
            p = jnp.where(ok4, jnp.exp(s - m_new), 0.0)
            l = a * l + jnp.sum(p, axis=-1, keepdims=True)
            acc = a * acc + _dot(p.astype(BF16), vch)
            return m_new, l, acc

        m0 = jnp.full((rows, 1), NEG_INF, F32)
        l0 = jnp.zeros((rows, 1), F32)
        a0 = jnp.zeros((rows, HEAD_DIM), F32)
        _, l_s, acc_s = lax.fori_loop(0, n_chunks, sel_chunk, (m0, l0, a0))
        o_slc = acc_s * jnp.where(l_s > 0.0, 1.0 / l_s, 0.0)

        kwb = kw_ref[0, pl.ds(w_start, win_len), lo:hi]
        vwb = vw_ref[0, pl.ds(w_start, win_len), lo:hi]
        pos_w = w_start + lax.broadcasted_iota(I32, (1, win_len), 1)
        d_w = t4 - pos_w
        s_w = _dot_t(qg, kwb) + slope * (pos_w - q0).astype(F32)
        p_w, inv_w = _softmax_rows(s_w, (d_w >= 0) & (d_w < WINDOW))
        o_win = _dot(p_w.astype(BF16), vwb) * inv_w

        for r in range(Q_PER_KV):
            h = g * Q_PER_KV + r
            sl = slice(r * tq, (r + 1) * tq)
            gc = gates[:, h * N_BRANCH:h * N_BRANCH + 1]
            gs = gates[:, h * N_BRANCH + 1:h * N_BRANCH + 2]
            gw = gates[:, h * N_BRANCH + 2:h * N_BRANCH + 3]
            outs.append(gc * o_cmp[sl] + gs * o_slc[sl] + gw * o_win[sl])
    o_ref[...] = jnp.concatenate(outs, axis=-1).astype(BF16)


def _nsa(q, kc, vc, kv3, gates, overlap, batch, seq):
    T = q.shape[0]
    tq = min(NSA_Q, seq)
    nq = seq // tq
    ncp = kc.shape[1]
    n_sel = seq // SEL_BLOCK
    row_map = lambda b, i: (b * nq + i, 0)
    bmap = lambda b, i: (b, 0, 0)
    kvspec = lambda j: pl.BlockSpec((1, seq, KV_DIM), lambda b, i, j=j: (b, 0, j))
    return pl.pallas_call(
        functools.partial(_nsa_kernel, seq=seq, n_sel=n_sel),
        grid=(batch, nq),
        in_specs=[
            pl.BlockSpec((tq, N_HEADS * HEAD_DIM), row_map),
            pl.BlockSpec((1, ncp, KV_DIM), bmap),
            pl.BlockSpec((1, ncp, KV_DIM), bmap),
            kvspec(2), kvspec(3), kvspec(4), kvspec(5),
            pl.BlockSpec((tq, LANES), row_map),
            pl.BlockSpec(overlap.shape, lambda b, i: (0, 0)),
        ],
        out_specs=pl.BlockSpec((tq, N_HEADS * HEAD_DIM), row_map),
        out_shape=jax.ShapeDtypeStruct((T, N_HEADS * HEAD_DIM), BF16),
        compiler_params=_cparams(("arbitrary", "arbitrary")),
        name="nsa",
    )(q, kc, vc, kv3, kv3, kv3, kv3, gates, overlap)


def _layer_norm(y, g, b):
    mu = jnp.mean(y, axis=-1, keepdims=True)
    yc = y - mu
    var = jnp.mean(yc * yc, axis=-1, keepdims=True)
    return yc * lax.rsqrt(var + LN_EPS) * g + b


def _post_kernel(x_ref, conv_ref, nsa_ref, wo_ref, g1_ref, b1_ref, rwh_ref, rwl_ref, rb_ref,
                 wsg_ref, wsu_ref, wsd_ref,
                 x3_ref, base_ref, ri_ref, rw_ref, cnt_ref, carry_ref, *, alpha):
    rows, D = x_ref.shape

    @pl.when(pl.program_id(0) == 0)
    def _():
        carry_ref[...] = jnp.zeros_like(carry_ref)

    half = wo_ref.shape[0] // 2
    mix = _dot(conv_ref[...], wo_ref[:half, :]) + _dot(nsa_ref[...], wo_ref[half:, :])
    x1 = _layer_norm(alpha * x_ref[...] + mix, g1_ref[...], b1_ref[...])
    for s in range(D // LANES):
        x3_ref[pl.ds(s, rows, stride=SUBLANES), :] = x1[:, s * LANES:(s + 1) * LANES]

    xh, xl = _split_bf16(x1)
    hid = jax.nn.silu(_dot(xh, wsg_ref[...])) * _dot(xh, wsu_ref[...])
    base_ref[...] = alpha * x1 + _dot(hid.astype(BF16), wsd_ref[...])

    logits = _dot(xh, rwh_ref[...]) + _dot(xl, rwh_ref[...]) + _dot(xh, rwl_ref[...])
    scores = jax.nn.sigmoid(logits)
    biased = scores + rb_ref[...]
    lane = lax.broadcasted_iota(I32, (1, N_EXPERTS), 1)
    lane_f = lane.astype(F32)
    grp = jnp.right_shift(lane, GROUP_SHIFT)
    gscore = []
    for gi in range(N_GROUPS):
        ing = grp == gi
        v = jnp.where(ing, biased, -jnp.inf)
        m1 = jnp.max(v, axis=-1, keepdims=True)
        i1 = jnp.min(jnp.where(v == m1, lane_f, float(N_EXPERTS)), axis=-1, keepdims=True)
        m2 = jnp.max(jnp.where(lane_f == i1, -jnp.inf, v), axis=-1, keepdims=True)
        gscore.append(m1 + m2)
    allowed = jnp.zeros((rows, N_EXPERTS), F32)
    for gi in range(N_GROUPS):
        ahead = jnp.zeros((rows, 1), F32)
        for gj in range(N_GROUPS):
            if gj == gi:
                continue
            beats = (gscore[gj] >= gscore[gi]) if gj < gi else (gscore[gj] > gscore[gi])
            ahead = ahead + jnp.where(beats, 1.0, 0.0)
        keep = ahead < float(TOPK_GROUPS)
        allowed = jnp.where((grp == gi) & keep, 1.0, allowed)
    cand = jnp.where(allowed > 0.5, biased, NEG_INF)
    onehot = jnp.zeros((rows, N_EXPERTS), F32)
    idx_cols, w_cols = [], []
    for _ in range(TOP_K):
        mx = jnp.max(cand, axis=-1, keepdims=True)
        first = jnp.min(jnp.where(cand == mx, lane_f, float(N_EXPERTS)), axis=-1, keepdims=True)
        hit = lane_f == first
        idx_cols.append(first)
        w_cols.append(jnp.sum(jnp.where(hit, scores, 0.0), axis=-1, keepdims=True))
        onehot = jnp.where(hit, 1.0, onehot)
        cand = jnp.where(hit, -jnp.inf, cand)
    wsum = w_cols[0]
    for k in range(1, TOP_K):
        wsum = wsum + w_cols[k]

    ri = lax.broadcasted_iota(I32, (rows, rows), 0)
    ci = lax.broadcasted_iota(I32, (rows, rows), 1)
    lower = jnp.where(ci < ri, 1.0, 0.0).astype(BF16)
    before = _dot(lower, onehot.astype(BF16)) + carry_ref[0:1, :]
    out_lane = lax.broadcasted_iota(I32, (1, LANES), 1)
    ri_out = jnp.zeros((rows, LANES), F32)
    rw_out = jnp.zeros((rows, LANES), F32)
    for k in range(TOP_K):
        rank = jnp.sum(jnp.where(lane_f == idx_cols[k], before, 0.0), axis=-1, keepdims=True)
        ri_out = jnp.where(out_lane == k, idx_cols[k], ri_out)
        ri_out = jnp.where(out_lane == TOP_K + k, rank, ri_out)
        rw_out = jnp.where(out_lane == k, w_cols[k] / wsum * ROUTED_SCALE, rw_out)
    ri_ref[...] = ri_out.astype(I32)
    rw_ref[...] = rw_out
    total = carry_ref[0:1, :] + jnp.sum(onehot, axis=0, keepdims=True)
    carry_ref[...] = jnp.broadcast_to(total, carry_ref.shape)
    cnt_ref[...] = jnp.broadcast_to(total, cnt_ref.shape).astype(I32)


def _post(x2, conv_out, nsa_out, w_out, g1, b1, rw_hi, rw_lo, rbias, wsg, wsu, wsd, alpha):
    T, D = x2.shape
    rows = min(POST_ROWS, T)
    row_map = lambda i: (i, 0)
    fixed = lambda i: (0, 0)
    full = lambda a: pl.BlockSpec(a.shape, fixed)
    return pl.pallas_call(
        functools.partial(_post_kernel, alpha=alpha),
        grid=(T // rows,),
        in_specs=[
            pl.BlockSpec((rows, D), row_map),
            pl.BlockSpec((rows, CONV_CH), row_map),
            pl.BlockSpec((rows, N_HEADS * HEAD_DIM), row_map),
            full(w_out), full(g1), full(b1), full(rw_hi), full(rw_lo), full(rbias),
            full(wsg), full(wsu), full(wsd),
        ],
        out_specs=[
            pl.BlockSpec((rows * SUBLANES, LANES), row_map),
            pl.BlockSpec((rows, D), row_map),
            pl.BlockSpec((rows, LANES), row_map),
            pl.BlockSpec((rows, LANES), row_map),
            pl.BlockSpec((SUBLANES, N_EXPERTS), fixed),
        ],
        out_shape=[
            jax.ShapeDtypeStruct((T * SUBLANES, LANES), F32),
            jax.ShapeDtypeStruct((T, D), F32),
            jax.ShapeDtypeStruct((T, LANES), I32),
            jax.ShapeDtypeStruct((T, LANES), F32),
            jax.ShapeDtypeStruct((SUBLANES, N_EXPERTS), I32),
        ],
        scratch_shapes=[pltpu.VMEM((SUBLANES, N_EXPERTS), F32)],
        compiler_params=_cparams(("arbitrary",)),
        name="post_attn_router",
    )(x2, conv_out, nsa_out, w_out, g1, b1, rw_hi, rw_lo, rbias, wsg, wsu, wsd)


def _push_kernel(pstart_ref, zoff_ref, e_ref, r_ref, x3_ref, xs_ref, zero_ref, sem, zsem):
    toks = x3_ref.shape[0] // SUBLANES
    zrows = zero_ref.shape[0]

    def zero_copy(e):
        off = pl.multiple_of(zoff_ref[e] * SUBLANES, SUBLANES)
        return pltpu.make_async_copy(zero_ref, xs_ref.at[pl.ds(off, zrows), :], zsem)

    @pl.when(pl.program_id(0) == 0)
    def _():
        zero_ref[...] = jnp.zeros_like(zero_ref)

        def start(e, c):
            @pl.when(zoff_ref[e] >= 0)
            def _():
                zero_copy(e).start()
            return c

        def wait(e, c):
            @pl.when(zoff_ref[e] >= 0)
            def _():
                zero_copy(e).wait()
            return c

        lax.fori_loop(0, N_EXPERTS, start, 0)
        lax.fori_loop(0, N_EXPERTS, wait, 0)

    def row_copy(a):
        t = jnp.right_shift(a, TOPK_SHIFT)
        slot = pstart_ref[e_ref[a]] + r_ref[a]
        src = x3_ref.at[pl.ds(pl.multiple_of(t * SUBLANES, SUBLANES), SUBLANES), :]
        dst = xs_ref.at[pl.ds(pl.multiple_of(slot * SUBLANES, SUBLANES), SUBLANES), :]
        return pltpu.make_async_copy(src, dst, sem)

    def start(a, c):
        row_copy(a).start()
        return c

    def wait(a, c):
        row_copy(a).wait()
        return c

    lax.fori_loop(0, toks * TOP_K, start, 0)
    lax.fori_loop(0, toks * TOP_K, wait, 0)


def _push(pad_start, zero_off, e_flat, r_flat, x3, n_slots):
    T = x3.shape[0] // SUBLANES
    toks = min(PUSH_ROWS, T)
    return pl.pallas_call(
        _push_kernel,
        grid_spec=pltpu.PrefetchScalarGridSpec(
            num_scalar_prefetch=2,
            grid=(T // toks,),
            in_specs=[
                pl.BlockSpec((toks * TOP_K,), lambda i, *_: (i,), memory_space=pltpu.SMEM),
                pl.BlockSpec((toks * TOP_K,), lambda i, *_: (i,), memory_space=pltpu.SMEM),
                pl.BlockSpec((toks * SUBLANES, LANES), lambda i, *_: (i, 0)),
            ],
            out_specs=pl.BlockSpec(memory_space=pl.ANY),
            scratch_shapes=[
                pltpu.VMEM((SLOT_BLOCK * SUBLANES, LANES), F32),
                pltpu.SemaphoreType.DMA(()),
                pltpu.SemaphoreType.DMA(()),
            ],
        ),
        out_shape=jax.ShapeDtypeStruct((n_slots * SUBLANES, LANES), F32),
        compiler_params=_cparams(("arbitrary",)),
        name="moe_push",
    )(pad_start, zero_off, e_flat, r_flat, x3)


def _expert_kernel(blk_e_ref, nused_ref, xs_ref, wg_ref, wu_ref, wd_ref, ys_ref, wgu_s, wd_s):
    b = pl.program_id(0)
    rows = xs_ref.shape[0] // SUBLANES
    D = wg_ref.shape[1]
    H = wg_ref.shape[2]
    prev = blk_e_ref[jnp.maximum(b - 1, 0)]

    @pl.when((b == 0) | (blk_e_ref[b] != prev))
    def _():
        wgu_s[:, :H] = wg_ref[0].astype(BF16)
        wgu_s[:, H:] = wu_ref[0].astype(BF16)
        wd_s[...] = wd_ref[0].astype(BF16)

    @pl.when(b < nused_ref[0])
    def _():
        xb = jnp.concatenate(
            [xs_ref[pl.ds(s, rows, stride=SUBLANES), :].astype(BF16) for s in range(D // LANES)],
            axis=-1)
        h = _dot(xb, wgu_s[...])
        act = (jax.nn.silu(h[:, :H]) * h[:, H:]).astype(BF16)
        out = _dot(act, wd_s[...])
        for s in range(D // LANES):
            ys_ref[pl.ds(s, rows, stride=SUBLANES), :] = out[:, s * LANES:(s + 1) * LANES]

    @pl.when(b >= nused_ref[0])
    def _():
        ys_ref[...] = jnp.zeros_like(ys_ref)


def _experts(blk_e, n_used, xs, w_gate, w_up, w_down):
    n_blocks = blk_e.shape[0]
    E, D, H = w_gate.shape
    rows = SLOT_BLOCK
    return pl.pallas_call(
        _expert_kernel,
        grid_spec=pltpu.PrefetchScalarGridSpec(
            num_scalar_prefetch=2,
            grid=(n_blocks,),
            in_specs=[
                pl.BlockSpec((rows * SUBLANES, LANES), lambda b, be, nu: (b, 0)),
                pl.BlockSpec((1, D, H), lambda b, be, nu: (be[b], 0, 0)),
                pl.BlockSpec((1, D, H), lambda b, be, nu: (be[b], 0, 0)),
                pl.BlockSpec((1, H, D), lambda b, be, nu: (be[b], 0, 0)),
            ],
            out_specs=pl.BlockSpec((rows * SUBLANES, LANES), lambda b, be, nu: (b, 0)),
            scratch_shapes=[pltpu.VMEM((D, 2 * H), BF16), pltpu.VMEM((H, D), BF16)],
        ),
        out_shape=jax.ShapeDtypeStruct(xs.shape, F32),
        compiler_params=_cparams(("arbitrary",)),
        name="moe_experts",
    )(blk_e, n_used, xs, w_gate, w_up, w_down)


def _combine_kernel(pstart_ref, e_ref, r_ref, ys_ref, base_ref, rw_ref, g2_ref, b2_ref,
                    o_ref, buf_ref, sem):
    toks, D = base_ref.shape

    def row_copy(a):
        t = jnp.right_shift(a, TOPK_SHIFT)
        k = jnp.bitwise_and(a, TOP_K - 1)
        slot = pstart_ref[e_ref[a]] + r_ref[a]
        src = ys_ref.at[pl.ds(pl.multiple_of(slot * SUBLANES, SUBLANES), SUBLANES), :]
        dst = buf_ref.at[pl.ds(pl.multiple_of((k * toks + t) * SUBLANES, SUBLANES), SUBLANES), :]
        return pltpu.make_async_copy(src, dst, sem)

    def start(a, c):
        row_copy(a).start()
        return c

    def wait(a, c):
        row_copy(a).wait()
        return c

    lax.fori_loop(0, toks * TOP_K, start, 0)
    lax.fori_loop(0, toks * TOP_K, wait, 0)

    w = rw_ref[...]
    pieces = []
    for s in range(D // LANES):
        acc = jnp.zeros((toks, LANES), F32)
        for k in range(TOP_K):
            rows = buf_ref[pl.ds(k * toks * SUBLANES + s, toks, stride=SUBLANES), :]
            acc = acc + w[:, k:k + 1] * rows
        pieces.append(acc)
    y = base_ref[...] + jnp.concatenate(pieces, axis=-1)
    o_ref[...] = _layer_norm(y, g2_ref[...], b2_ref[...])


def _combine(pad_start, e_flat, r_flat, ys, base, rw, g2, b2):
    T, D = base.shape
    toks = min(COMB_ROWS, T)
    return pl.pallas_call(
        _combine_kernel,
        grid_spec=pltpu.PrefetchScalarGridSpec(
            num_scalar_prefetch=1,
            grid=(T // toks,),
            in_specs=[
                pl.BlockSpec((toks * TOP_K,), lambda i, *_: (i,), memory_space=pltpu.SMEM),
                pl.BlockSpec((toks * TOP_K,), lambda i, *_: (i,), memory_space=pltpu.SMEM),
                pl.BlockSpec(memory_space=pl.ANY),
                pl.BlockSpec((toks, D), lambda i, *_: (i, 0)),
                pl.BlockSpec((toks, LANES), lambda i, *_: (i, 0)),
                pl.BlockSpec(g2.shape, lambda i, *_: (0, 0)),
                pl.BlockSpec(b2.shape, lambda i, *_: (0, 0)),
            ],
            out_specs=pl.BlockSpec((toks, D), lambda i, *_: (i, 0)),
            scratch_shapes=[
                pltpu.VMEM((TOP_K * toks * SUBLANES, LANES), F32),
                pltpu.SemaphoreType.DMA(()),
            ],
        ),
        out_shape=jax.ShapeDtypeStruct((T, D), F32),
        compiler_params=_cparams(("arbitrary",)),
        name="moe_combine",
    )(pad_start, e_flat, r_flat, ys, base, rw, g2, b2)


def _overlap_matrix(ncp):
    n = np.arange(ncp)[:, None]
    j = np.arange(LANES)[None, :]
    start = n * CMP_STRIDE
    end = start + CMP_LEN - 1
    sel_start = j * SEL_BLOCK
    ovl = (start < sel_start + SEL_BLOCK) & (end >= sel_start)
    return jnp.asarray(ovl.astype(np.float32), dtype=BF16)


def _mixer(x2, batch, seq, w_in, conv_w, cmp_k, cmp_v):
    c3 = 3 * CONV_CH
    qd = N_HEADS * HEAD_DIM
    w_conv = w_in[:, :c3].astype(BF16)
    w_q = w_in[:, c3:c3 + qd].astype(BF16)
    w_kv = w_in[:, c3 + qd:c3 + qd + 6 * KV_DIM].astype(BF16)
    w_g = jnp.pad(w_in[:, c3 + qd + 6 * KV_DIM:], ((0, 0), (0, LANES - N_HEADS * N_BRANCH))).astype(BF16)
    conv_out, q, kv, gates = _proj_conv(x2, w_conv, w_q, w_kv, w_g, conv_w, batch, seq)
    kc, vc = _compress(kv[:, :KV_DIM], kv[:, KV_DIM:2 * KV_DIM], cmp_k, cmp_v, batch, seq)
    ncp = -(-kc.shape[1] // LANES) * LANES
    if ncp != kc.shape[1]:
        padn = ((0, 0), (0, ncp - kc.shape[1]), (0, 0))
        kc, vc = jnp.pad(kc, padn), jnp.pad(vc, padn)
    kv3 = kv.reshape(batch, seq, 6 * KV_DIM)
    nsa_out = _nsa(q, kc, vc, kv3, gates, _overlap_matrix(ncp), batch, seq)
    return conv_out, nsa_out


def _moe(x3, base, route_i, route_w, counts, w_gate, w_up, w_down, g2, b2):
    T = base.shape[0]
    A = T * TOP_K
    n_blocks = -(-(A + N_EXPERTS * (SLOT_BLOCK - 1)) // SLOT_BLOCK)
    cnt = counts[0]
    padded = (cnt + SLOT_BLOCK - 1) // SLOT_BLOCK * SLOT_BLOCK
    pad_end = jnp.cumsum(padded)
    pad_start = (pad_end - padded).astype(I32)
    zero_off = jnp.where(padded > 0, pad_end - SLOT_BLOCK, -1).astype(I32)
    n_used = (pad_end[-1:] // SLOT_BLOCK).astype(I32)
    blk_e = jnp.minimum(
        jnp.searchsorted(pad_end, jnp.arange(n_blocks, dtype=I32) * SLOT_BLOCK, side='right'),
        N_EXPERTS - 1).astype(I32)
    e_flat = route_i[:, :TOP_K].reshape(A)
    r_flat = route_i[:, TOP_K:2 * TOP_K].reshape(A)
    xs = _push(pad_start, zero_off, e_flat, r_flat, x3, n_blocks * SLOT_BLOCK)
    ys = _experts(blk_e, n_used, xs, w_gate, w_up, w_down)
    return _combine(pad_start, e_flat, r_flat, ys, base, route_w, g2, b2)


def kernel(x, w_in, conv_w, ck_pos, ck_w1, ck_b1, ck_w2, cv_pos, cv_w1, cv_b1, cv_w2, w_out, ln1_g, ln1_b, router_w, router_bias, w_gate, w_up, w_down, ws_gate, ws_up, ws_down, ln2_g, ln2_b):
    batch, seq, D = x.shape
    depth = w_in.shape[0]
    alpha = (2.0 * depth) ** 0.25
    x2 = x.reshape(batch * seq, D)
    for l in range(depth):
        cmp_k = _compress_weights(ck_pos[l], ck_w1[l], ck_b1[l], ck_w2[l])
        cmp_v = _compress_weights(cv_pos[l], cv_w1[l], cv_b1[l], cv_w2[l])
        conv_out, nsa_out = _mixer(x2, batch, seq, w_in[l], conv_w[l], cmp_k, cmp_v)
        rw_hi, rw_lo = _split_bf16(router_w[l])
        x3, base, route_i, route_w, counts = _post(
            x2, conv_out, nsa_out, w_out[l].astype(BF16), ln1_g[l][None, :], ln1_b[l][None, :],
            rw_hi, rw_lo, router_bias[l][None, :],
            ws_gate[l].astype(BF16), ws_up[l].astype(BF16), ws_down[l].astype(BF16), alpha)
        x2 = _moe(x3, base, route_i, route_w, counts, w_gate[l], w_up[l], w_down[l],
                  ln2_g[l][None, :], ln2_b[l][None, :])
    return x2.reshape(batch, seq, D)
```

```python
import functools
import math

import jax
import jax.numpy as jnp
import numpy as np
from jax import lax
from jax.experimental import pallas as pl
from jax.experimental.pallas import tpu as pltpu

F32 = jnp.float32
BF16 = jnp.bfloat16
I32 = jnp.int32

CONV_CH = 512
CONV_WIDTH = 3
N_HEADS = 8
HEAD_DIM = 64
N_KV_HEADS = 2
Q_PER_KV = N_HEADS // N_KV_HEADS
KV_DIM = N_KV_HEADS * HEAD_DIM
N_BRANCH = 3
CMP_LEN = 32
CMP_STRIDE = 16
SEL_BLOCK = 64
SEL_TOPK = 8
WINDOW = 512
FORCED_SCORE = 1e4
N_EXPERTS = 256
TOP_K = 8
N_GROUPS = 8
TOPK_GROUPS = 4
GROUP_SIZE = N_EXPERTS // N_GROUPS
ROUTED_SCALE = 2.5
LN_EPS = 1e-5
NEG_INF = -1e30
SEL_SHIFT = SEL_BLOCK.bit_length() - 1
GROUP_SHIFT = GROUP_SIZE.bit_length() - 1
TOPK_SHIFT = TOP_K.bit_length() - 1

LANES = 128
SUBLANES = 8
VMEM_LIMIT = 56 * 1024 * 1024

PROJ_ROWS = 512
NSA_Q = 128
NSA_KC = 512
POST_ROWS = 256
SLOT_BLOCK = 256
PUSH_ROWS = 512
COMB_ROWS = 256


def _dot(a, b):
    return jnp.dot(a, b, preferred_element_type=F32)


def _dot_t(a, b):
    return lax.dot_general(a, b, (((1,), (1,)), ((), ())), preferred_element_type=F32)


def _split_bf16(x):
    hi = x.astype(BF16)
    lo = (x - hi.astype(F32)).astype(BF16)
    return hi, lo


def _cparams(sem):
    return pltpu.CompilerParams(dimension_semantics=sem, vmem_limit_bytes=VMEM_LIMIT)


def _proj_conv_kernel(x_ref, wc_ref, wq_ref, wkv_ref, wg_ref, cw_ref,
                      conv_ref, q_ref, kv_ref, gate_ref, carry_ref):
    rows = x_ref.shape[0]

    @pl.when(pl.program_id(1) == 0)
    def _():
        carry_ref[...] = jnp.zeros_like(carry_ref)

    xb = x_ref[...].astype(BF16)
    acc = _dot(xb, wc_ref[...])
    b_g = acc[:, :CONV_CH]
    u = acc[:, CONV_CH:2 * CONV_CH] * acc[:, 2 * CONV_CH:]
    prev2 = carry_ref[SUBLANES - 2:SUBLANES - 1, :]
    prev1 = carry_ref[SUBLANES - 1:SUBLANES, :]
    ri = lax.broadcasted_iota(I32, (rows, 1), 0)
    u1 = jnp.where(ri == 0, prev1, pltpu.roll(u, 1, 0))
    u2 = jnp.where(ri == 0, prev2, jnp.where(ri == 1, prev1, pltpu.roll(u, 2, 0)))
    y = cw_ref[0:1, :] * u2 + cw_ref[1:2, :] * u1 + cw_ref[2:3, :] * u
    conv_ref[...] = (b_g * y).astype(BF16)
    carry_ref[...] = u[rows - SUBLANES:, :]

    q_ref[...] = (_dot(xb, wq_ref[...]) * (HEAD_DIM ** -0.5)).astype(BF16)
    kv_ref[...] = _dot(xb, wkv_ref[...]).astype(BF16)
    gate_ref[...] = jax.nn.sigmoid(_dot(xb, wg_ref[...]))


def _proj_conv(x2, w_conv, w_q, w_kv, w_g, conv_w, batch, seq):
    T, D = x2.shape
    rows = min(PROJ_ROWS, seq)
    nt = seq // rows
    row_map = lambda b, i: (b * nt + i, 0)
    fixed = lambda b, i: (0, 0)
    return pl.pallas_call(
        _proj_conv_kernel,
        grid=(batch, nt),
        in_specs=[
            pl.BlockSpec((rows, D), row_map),
            pl.BlockSpec(w_conv.shape, fixed),
            pl.BlockSpec(w_q.shape, fixed),
            pl.BlockSpec(w_kv.shape, fixed),
            pl.BlockSpec(w_g.shape, fixed),
            pl.BlockSpec(conv_w.shape, fixed),
        ],
        out_specs=[
            pl.BlockSpec((rows, CONV_CH), row_map),
            pl.BlockSpec((rows, N_HEADS * HEAD_DIM), row_map),
            pl.BlockSpec((rows, 6 * KV_DIM), row_map),
            pl.BlockSpec((rows, LANES), row_map),
        ],
        out_shape=[
            jax.ShapeDtypeStruct((T, CONV_CH), BF16),
            jax.ShapeDtypeStruct((T, N_HEADS * HEAD_DIM), BF16),
            jax.ShapeDtypeStruct((T, 6 * KV_DIM), BF16),
            jax.ShapeDtypeStruct((T, LANES), F32),
        ],
        scratch_shapes=[pltpu.VMEM((SUBLANES, CONV_CH), F32)],
        compiler_params=_cparams(("arbitrary", "arbitrary")),
        name="proj_conv",
    )(x2, w_conv, w_q, w_kv, w_g, conv_w)


def _compress_kernel(ck_ref, cv_ref, wtk_ref, wbk_ref, w2k_ref, ptk_ref, pbk_ref, b1k_ref,
                     wtv_ref, wbv_ref, w2v_ref, ptv_ref, pbv_ref, b1v_ref, kc_ref, vc_ref):
    def one(c_ref, wt_ref, wb_ref, w2_ref, pt_ref, pb_ref, b1_ref, o_ref):
        c = c_ref[0]
        top = _dot(c, wt_ref[...])
        bot = _dot(c, wb_ref[...])
        c0 = _dot(pt_ref[...], wt_ref[...]) + _dot(pb_ref[...], wb_ref[...]) + b1_ref[...]
        n = top.shape[0]
        h = top + pltpu.roll(bot, n - 1, 0) + c0[0:1, :]
        g = jax.nn.gelu(h, approximate=True)
        o_ref[0] = _dot(g.astype(BF16), w2_ref[...]).astype(BF16)

    one(ck_ref, wtk_ref, wbk_ref, w2k_ref, ptk_ref, pbk_ref, b1k_ref, kc_ref)
    one(cv_ref, wtv_ref, wbv_ref, w2v_ref, ptv_ref, pbv_ref, b1v_ref, vc_ref)


def _blockdiag2(w):
    z = jnp.zeros_like(w)
    return jnp.concatenate([jnp.concatenate([w, z], 1), jnp.concatenate([z, w], 1)], 0)


def _compress_weights(pos, w1, b1, w2):
    w1r = w1.reshape(CMP_LEN, HEAD_DIM, HEAD_DIM)
    eye = jnp.eye(N_KV_HEADS, dtype=w1.dtype)
    wfull = (w1r[:, None, :, None, :] * eye[None, :, None, :, None]).reshape(CMP_LEN, KV_DIM, KV_DIM)
    w_top = wfull[:CMP_STRIDE].reshape(CMP_STRIDE * KV_DIM, KV_DIM).astype(BF16)
    w_bot = wfull[CMP_STRIDE:].reshape(CMP_STRIDE * KV_DIM, KV_DIM).astype(BF16)
    posr = jnp.tile(pos, (1, N_KV_HEADS))
    pos_top = jnp.tile(posr[:CMP_STRIDE].reshape(1, -1), (SUBLANES, 1)).astype(BF16)
    pos_bot = jnp.tile(posr[CMP_STRIDE:].reshape(1, -1), (SUBLANES, 1)).astype(BF16)
    b1r = jnp.tile(b1[None, :], (SUBLANES, N_KV_HEADS)).astype(F32)
    return w_top, w_bot, _blockdiag2(w2).astype(BF16), pos_top, pos_bot, b1r


def _compress(kc_raw, vc_raw, wk, wv, batch, seq):
    chunks = seq // CMP_STRIDE
    width = CMP_STRIDE * KV_DIM
    ck = kc_raw.reshape(batch, chunks, width)
    cv = vc_raw.reshape(batch, chunks, width)
    bmap = lambda b: (b, 0, 0)
    fixed = lambda b: (0, 0)
    wspecs = [pl.BlockSpec(w.shape, fixed) for w in wk]
    return pl.pallas_call(
        _compress_kernel,
        grid=(batch,),
        in_specs=[pl.BlockSpec((1, chunks, width), bmap), pl.BlockSpec((1, chunks, width), bmap)]
        + wspecs + wspecs,
        out_specs=[pl.BlockSpec((1, chunks, KV_DIM), bmap)] * 2,
        out_shape=[jax.ShapeDtypeStruct((batch, chunks, KV_DIM), BF16)] * 2,
        compiler_params=_cparams(("arbitrary",)),
        name="compress",
    )(ck, cv, *wk, *wv)


def _softmax_rows(s, valid):
    s = jnp.where(valid, s, NEG_INF)
    m = jnp.max(s, axis=-1, keepdims=True)
    p = jnp.where(valid, jnp.exp(s - m), 0.0)
    l = jnp.sum(p, axis=-1, keepdims=True)
    inv = jnp.where(l > 0.0, 1.0 / l, 0.0)
    return p, inv


def _nsa_kernel(q_ref, kc_ref, vc_ref, ks_ref, vs_ref, kw_ref, vw_ref, gate_ref, ovl_ref,
                o_ref, *, seq, n_sel):
    tq = q_ref.shape[0]
    ncp = kc_ref.shape[1]
    rows = Q_PER_KV * tq
    q0 = pl.program_id(1) * tq
    t_col = q0 + lax.broadcasted_iota(I32, (tq, 1), 0)
    t4 = jnp.concatenate([t_col] * Q_PER_KV, axis=0)
    row_i = lax.broadcasted_iota(I32, (rows, 1), 0)
    lane = lax.broadcasted_iota(I32, (1, LANES), 1)
    lane_f = lane.astype(F32)
    gates = gate_ref[...]
    win_len = WINDOW + tq
    w_start = pl.multiple_of(jnp.maximum(q0 - WINDOW, 0), tq)
    n_chunks = (q0 + tq + NSA_KC - 1) // NSA_KC
    outs = []
    for g in range(N_KV_HEADS):
        lo, hi = g * HEAD_DIM, (g + 1) * HEAD_DIM
        qg = jnp.concatenate(
            [q_ref[:, (g * Q_PER_KV + r) * HEAD_DIM:(g * Q_PER_KV + r + 1) * HEAD_DIM]
             for r in range(Q_PER_KV)], axis=0)
        slope = jnp.zeros((rows, 1), F32)
        for r in range(Q_PER_KV):
            h = g * Q_PER_KV + r
            in_head = (row_i >= r * tq) & (row_i < (r + 1) * tq)
            slope = jnp.where(in_head, 2.0 ** (-8.0 * (h + 1) / N_HEADS), slope)

        cmp_end = lax.broadcasted_iota(I32, (1, ncp), 1) * CMP_STRIDE + (CMP_LEN - 1)
        d_c = t4 - cmp_end
        s_c = _dot_t(qg, kc_ref[0, :, lo:hi]) - slope * d_c.astype(F32)
        p_c, inv_c = _softmax_rows(s_c, d_c >= 0)
        p_c = p_c * inv_c
        o_cmp = _dot(p_c.astype(BF16), vc_ref[0, :, lo:hi])

        ps = p_c[0:tq]
        for r in range(1, Q_PER_KV):
            ps = ps + p_c[r * tq:(r + 1) * tq]
        ps_hi, ps_lo = _split_bf16(ps)
        imp = _dot(ps_hi, ovl_ref[...]) + _dot(ps_lo, ovl_ref[...])
        forced = (lane == 0) | (lane == jnp.right_shift(t_col, SEL_SHIFT))
        causal = lane * SEL_BLOCK <= t_col
        score = jnp.where(forced, FORCED_SCORE, jnp.where(causal, imp, -1.0))
        score = jnp.where(lane < n_sel, score, -jnp.inf)
        sel = jnp.zeros((tq, LANES), F32)
        for _ in range(min(SEL_TOPK, n_sel)):
            mx = jnp.max(score, axis=-1, keepdims=True)
            first = jnp.min(jnp.where(score == mx, lane_f, float(LANES)), axis=-1, keepdims=True)
            hit = lane_f == first
            sel = jnp.where(hit, 1.0, sel)
            score = jnp.where(hit, -jnp.inf, score)
        sel_b = sel.astype(BF16)

        def sel_chunk(c, carry):
            m, l, acc = carry
            k0 = pl.multiple_of(c * NSA_KC, NSA_KC)
            kch = ks_ref[0, pl.ds(k0, NSA_KC), lo:hi]
            vch = vs_ref[0, pl.ds(k0, NSA_KC), lo:hi]
            pos = k0 + lax.broadcasted_iota(I32, (1, NSA_KC), 1)
            s = _dot_t(qg, kch) + slope * (pos - q0).astype(F32)
            blk = lax.broadcasted_iota(I32, (LANES, NSA_KC), 0)
            kblk = jnp.right_shift(k0 + lax.broadcasted_iota(I32, (LANES, NSA_KC), 1), SEL_SHIFT)
            expand = jnp.where(blk == kblk, 1.0, 0.0).astype(BF16)
            picked = _dot(sel_b, expand)
            ok = jnp.where((picked > 0.5) & (pos <= t_col), 1.0, 0.0)
            ok4 = jnp.concatenate([ok] * Q_PER_KV, axis=0) > 0.5
            s = jnp.where(ok4, s, NEG_INF)
            m_new = jnp.maximum(m, jnp.max(s, axis=-1, keepdims=True))
            a = jnp.exp(m - m_new)
            p = jnp.where(ok4, jnp.exp(s - m_new), 0.0)
            l = a * l + jnp.sum(p, axis=-1, keepdims=True)
            acc = a * acc + _dot(p.astype(BF16), vch)
            return m_new, l, acc

        m0 = jnp.full((rows, 1), NEG_INF, F32)
        l0 = jnp.zeros((rows, 1), F32)
        a0 = jnp.zeros((rows, HEAD_DIM), F32)
        _, l_s, acc_s = lax.fori_loop(0, n_chunks, sel_chunk, (m0, l0, a0))
        o_slc = acc_s * jnp.where(l_s > 0.0, 1.0 / l_s, 0.0)

        kwb = kw_ref[0, pl.ds(w_start, win_len), lo:hi]
        vwb = vw_ref[0, pl.ds(w_start, win_len), lo:hi]
        pos_w = w_start + lax.broadcasted_iota(I32, (1, win_len), 1)
        d_w = t4 - pos_w
        s_w = _dot_t(qg, kwb) + slope * (pos_w - q0).astype(F32)
        p_w, inv_w = _softmax_rows(s_w, (d_w >= 0) & (d_w < WINDOW))
        o_win = _dot(p_w.astype(BF16), vwb) * inv_w

        for r in range(Q_PER_KV):
            h = g * Q_PER_KV + r
            sl = slice(r * tq, (r + 1) * tq)
            gc = gates[:, h * N_BRANCH:h * N_BRANCH + 1]
            gs = gates[:, h * N_BRANCH + 1:h * N_BRANCH + 2]
            gw = gates[:, h * N_BRANCH + 2:h * N_BRANCH + 3]
            outs.append(gc * o_cmp[sl] + gs * o_slc[sl] + gw * o_win[sl])
    o_ref[...] = jnp.concatenate(outs, axis=-1).astype(BF16)


def _nsa(q, kc, vc, kv3, gates, overlap, batch, seq):
    T = q.shape[0]
    tq = min(NSA_Q, seq)
    nq = seq // tq
    ncp = kc.shape[1]
    n_sel = seq // SEL_BLOCK
    row_map = lambda b, i: (b * nq + i, 0)
    bmap = lambda b, i: (b, 0, 0)
    kvspec = lambda j: pl.BlockSpec((1, seq, KV_DIM), lambda b, i, j=j: (b, 0, j))
    return pl.pallas_call(
        functools.partial(_nsa_kernel, seq=seq, n_sel=n_sel),
        grid=(batch, nq),
        in_specs=[
            pl.BlockSpec((tq, N_HEADS * HEAD_DIM), row_map),
            pl.BlockSpec((1, ncp, KV_DIM), bmap),
            pl.BlockSpec((1, ncp, KV_DIM), bmap),
            kvspec(2), kvspec(3), kvspec(4), kvspec(5),
            pl.BlockSpec((tq, LANES), row_map),
            pl.BlockSpec(overlap.shape, lambda b, i: (0, 0)),
        ],
        out_specs=pl.BlockSpec((tq, N_HEADS * HEAD_DIM), row_map),
        out_shape=jax.ShapeDtypeStruct((T, N_HEADS * HEAD_DIM), BF16),
        compiler_params=_cparams(("arbitrary", "arbitrary")),
        name="nsa",
    )(q, kc, vc, kv3, kv3, kv3, kv3, gates, overlap)


def _layer_norm(y, g, b):
    mu = jnp.mean(y, axis=-1, keepdims=True)
    yc = y - mu
    var = jnp.mean(yc * yc, axis=-1, keepdims=True)
    return yc * lax.rsqrt(var + LN_EPS) * g + b


def _post_kernel(x_ref, conv_ref, nsa_ref, wo_ref, g1_ref, b1_ref, rwh_ref, rwl_ref, rb_ref,
                 wsg_ref, wsu_ref, wsd_ref,
                 x3_ref, base_ref, ri_ref, rw_ref, cnt_ref, carry_ref, *, alpha):
    rows, D = x_ref.shape

    @pl.when(pl.program_id(0) == 0)
    def _():
        carry_ref[...] = jnp.zeros_like(carry_ref)

    half = wo_ref.shape[0] // 2
    mix = _dot(conv_ref[...], wo_ref[:half, :]) + _dot(nsa_ref[...], wo_ref[half:, :])
    x1 = _layer_norm(alpha * x_ref[...] + mix, g1_ref[...], b1_ref[...])
    for s in range(D // LANES):
        x3_ref[pl.ds(s, rows, stride=SUBLANES), :] = x1[:, s * LANES:(s + 1) * LANES]

    xh, xl = _split_bf16(x1)
    hid = jax.nn.silu(_dot(xh, wsg_ref[...])) * _dot(xh, wsu_ref[...])
    base_ref[...] = alpha * x1 + _dot(hid.astype(BF16), wsd_ref[...])

    logits = _dot(xh, rwh_ref[...]) + _dot(xl, rwh_ref[...]) + _dot(xh, rwl_ref[...])
    scores = jax.nn.sigmoid(logits)
    biased = scores + rb_ref[...]
    lane = lax.broadcasted_iota(I32, (1, N_EXPERTS), 1)
    lane_f = lane.astype(F32)
    grp = jnp.right_shift(lane, GROUP_SHIFT)
    gscore = []
    for gi in range(N_GROUPS):
        ing = grp == gi
        v = jnp.where(ing, biased, -jnp.inf)
        m1 = jnp.max(v, axis=-1, keepdims=True)
        i1 = jnp.min(jnp.where(v == m1, lane_f, float(N_EXPERTS)), axis=-1, keepdims=True)
        m2 = jnp.max(jnp.where(lane_f == i1, -jnp.inf, v), axis=-1, keepdims=True)
        gscore.append(m1 + m2)
    allowed = jnp.zeros((rows, N_EXPERTS), F32)
    for gi in range(N_GROUPS):
        ahead = jnp.zeros((rows, 1), F32)
        for gj in range(N_GROUPS):
            if gj == gi:
                continue
            beats = (gscore[gj] >= gscore[gi]) if gj < gi else (gscore[gj] > gscore[gi])
            ahead = ahead + jnp.where(beats, 1.0, 0.0)
        keep = ahead < float(TOPK_GROUPS)
        allowed = jnp.where((grp == gi) & keep, 1.0, allowed)
    cand = jnp.where(allowed > 0.5, biased, NEG_INF)
    onehot = jnp.zeros((rows, N_EXPERTS), F32)
    idx_cols, w_cols = [], []
    for _ in range(TOP_K):
        mx = jnp.max(cand, axis=-1, keepdims=True)
        first = jnp.min(jnp.where(cand == mx, lane_f, float(N_EXPERTS)), axis=-1, keepdims=True)
        hit = lane_f == first
        idx_cols.append(first)
        w_cols.append(jnp.sum(jnp.where(hit, scores, 0.0), axis=-1, keepdims=True))
        onehot = jnp.where(hit, 1.0, onehot)
        cand = jnp.where(hit, -jnp.inf, cand)
    wsum = w_cols[0]
    for k in range(1, TOP_K):
        wsum = wsum + w_cols[k]

    ri = lax.broadcasted_iota(I32, (rows, rows), 0)
    ci = lax.broadcasted_iota(I32, (rows, rows), 1)
    lower = jnp.where(ci < ri, 1.0, 0.0).astype(BF16)
    before = _dot(lower, onehot.astype(BF16)) + carry_ref[0:1, :]
    out_lane = lax.broadcasted_iota(I32, (1, LANES), 1)
    ri_out = jnp.zeros((rows, LANES), F32)
    rw_out = jnp.zeros((rows, LANES), F32)
    for k in range(TOP_K):
        rank = jnp.sum(jnp.where(lane_f == idx_cols[k], before, 0.0), axis=-1, keepdims=True)
        ri_out = jnp.where(out_lane == k, idx_cols[k], ri_out)
        ri_out = jnp.where(out_lane == TOP_K + k, rank, ri_out)
        rw_out = jnp.where(out_lane == k, w_cols[k] / wsum * ROUTED_SCALE, rw_out)
    ri_ref[...] = ri_out.astype(I32)
    rw_ref[...] = rw_out
    total = carry_ref[0:1, :] + jnp.sum(onehot, axis=0, keepdims=True)
    carry_ref[...] = jnp.broadcast_to(total, carry_ref.shape)
    cnt_ref[...] = jnp.broadcast_to(total, cnt_ref.shape).astype(I32)


def _post(x2, conv_out, nsa_out, w_out, g1, b1, rw_hi, rw_lo, rbias, wsg, wsu, wsd, alpha):
    T, D = x2.shape
    rows = min(POST_ROWS, T)
    row_map = lambda i: (i, 0)
    fixed = lambda i: (0, 0)
    full = lambda a: pl.BlockSpec(a.shape, fixed)
    return pl.pallas_call(
        functools.partial(_post_kernel, alpha=alpha),
        grid=(T // rows,),
        in_specs=[
            pl.BlockSpec((rows, D), row_map),
            pl.BlockSpec((rows, CONV_CH), row_map),
            pl.BlockSpec((rows, N_HEADS * HEAD_DIM), row_map),
            full(w_out), full(g1), full(b1), full(rw_hi), full(rw_lo), full(rbias),
            full(wsg), full(wsu), full(wsd),
        ],
        out_specs=[
            pl.BlockSpec((rows * SUBLANES, LANES), row_map),
            pl.BlockSpec((rows, D), row_map),
            pl.BlockSpec((rows, LANES), row_map),
            pl.BlockSpec((rows, LANES), row_map),
            pl.BlockSpec((SUBLANES, N_EXPERTS), fixed),
        ],
        out_shape=[
            jax.ShapeDtypeStruct((T * SUBLANES, LANES), F32),
            jax.ShapeDtypeStruct((T, D), F32),
            jax.ShapeDtypeStruct((T, LANES), I32),
            jax.ShapeDtypeStruct((T, LANES), F32),
            jax.ShapeDtypeStruct((SUBLANES, N_EXPERTS), I32),
        ],
        scratch_shapes=[pltpu.VMEM((SUBLANES, N_EXPERTS), F32)],
        compiler_params=_cparams(("arbitrary",)),
        name="post_attn_router",
    )(x2, conv_out, nsa_out, w_out, g1, b1, rw_hi, rw_lo, rbias, wsg, wsu, wsd)


def _push_kernel(pstart_ref, zoff_ref, e_ref, r_ref, x3_ref, xs_ref, zero_ref, sem, zsem):
    toks = x3_ref.shape[0] // SUBLANES
    zrows = zero_ref.shape[0]

    def zero_copy(e):
        off = pl.multiple_of(zoff_ref[e] * SUBLANES, SUBLANES)
        return pltpu.make_async_copy(zero_ref, xs_ref.at[pl.ds(off, zrows), :], zsem)

    @pl.when(pl.program_id(0) == 0)
    def _():
        zero_ref[...] = jnp.zeros_like(zero_ref)

        def start(e, c):
            @pl.when(zoff_ref[e] >= 0)
            def _():
                zero_copy(e).start()
            return c

        def wait(e, c):
            @pl.when(zoff_ref[e] >= 0)
            def _():
                zero_copy(e).wait()
            return c

        lax.fori_loop(0, N_EXPERTS, start, 0)
        lax.fori_loop(0, N_EXPERTS, wait, 0)

    def push_token(t, c):
        src = x3_ref.at[pl.ds(pl.multiple_of(t * SUBLANES, SUBLANES), SUBLANES), :]
        for k in range(TOP_K):
            a = t * TOP_K + k
            slot = pstart_ref[e_ref[a]] + r_ref[a]
            dst = xs_ref.at[pl.ds(pl.multiple_of(slot * SUBLANES, SUBLANES), SUBLANES), :]
            pltpu.make_async_copy(src, dst, sem).start()
        return c

    lax.fori_loop(0, toks, push_token, 0)
    for _ in range(TOP_K):
        pltpu.make_async_copy(x3_ref, xs_ref.at[pl.ds(0, toks * SUBLANES), :], sem).wait()


def _push(pad_start, zero_off, e_flat, r_flat, x3, n_slots):
    T = x3.shape[0] // SUBLANES
    toks = min(PUSH_ROWS, T)
    return pl.pallas_call(
        _push_kernel,
        grid_spec=pltpu.PrefetchScalarGridSpec(
            num_scalar_prefetch=2,
            grid=(T // toks,),
            in_specs=[
                pl.BlockSpec((toks * TOP_K,), lambda i, *_: (i,), memory_space=pltpu.SMEM),
                pl.BlockSpec((toks * TOP_K,), lambda i, *_: (i,), memory_space=pltpu.SMEM),
                pl.BlockSpec((toks * SUBLANES, LANES), lambda i, *_: (i, 0)),
            ],
            out_specs=pl.BlockSpec(memory_space=pl.ANY),
            scratch_shapes=[
                pltpu.VMEM((SLOT_BLOCK * SUBLANES, LANES), F32),
                pltpu.SemaphoreType.DMA(()),
                pltpu.SemaphoreType.DMA(()),
            ],
        ),
        out_shape=jax.ShapeDtypeStruct((n_slots * SUBLANES, LANES), F32),
        compiler_params=_cparams(("arbitrary",)),
        name="moe_push",
    )(pad_start, zero_off, e_flat, r_flat, x3)


def _expert_kernel(blk_e_ref, nused_ref, xs_ref, wg_ref, wu_ref, wd_ref, ys_ref, wgu_s, wd_s):
    b = pl.program_id(0)
    rows = xs_ref.shape[0] // SUBLANES
    D = wg_ref.shape[1]
    H = wg_ref.shape[2]
    prev = blk_e_ref[jnp.maximum(b - 1, 0)]

    @pl.when((b == 0) | (blk_e_ref[b] != prev))
    def _():
        wgu_s[:, :H] = wg_ref[0].astype(BF16)
        wgu_s[:, H:] = wu_ref[0].astype(BF16)
        wd_s[...] = wd_ref[0].astype(BF16)

    @pl.when(b < nused_ref[0])
    def _():
        xb = jnp.concatenate(
            [xs_ref[pl.ds(s, rows, stride=SUBLANES), :].astype(BF16) for s in range(D // LANES)],
            axis=-1)
        h = _dot(xb, wgu_s[...])
        act = (jax.nn.silu(h[:, :H]) * h[:, H:]).astype(BF16)
        out = _dot(act, wd_s[...])
        for s in range(D // LANES):
            ys_ref[pl.ds(s, rows, stride=SUBLANES), :] = out[:, s * LANES:(s + 1) * LANES]

    @pl.when(b >= nused_ref[0])
    def _():
        ys_ref[...] = jnp.zeros_like(ys_ref)


def _experts(blk_e, n_used, xs, w_gate, w_up, w_down):
    n_blocks = blk_e.shape[0]
    E, D, H = w_gate.shape
    rows = SLOT_BLOCK
    return pl.pallas_call(
        _expert_kernel,
        grid_spec=pltpu.PrefetchScalarGridSpec(
            num_scalar_prefetch=2,
            grid=(n_blocks,),
            in_specs=[
                pl.BlockSpec((rows * SUBLANES, LANES), lambda b, be, nu: (b, 0)),
                pl.BlockSpec((1, D, H), lambda b, be, nu: (be[b], 0, 0)),
                pl.BlockSpec((1, D, H), lambda b, be, nu: (be[b], 0, 0)),
                pl.BlockSpec((1, H, D), lambda b, be, nu: (be[b], 0, 0)),
            ],
            out_specs=pl.BlockSpec((rows * SUBLANES, LANES), lambda b, be, nu: (b, 0)),
            scratch_shapes=[pltpu.VMEM((D, 2 * H), BF16), pltpu.VMEM((H, D), BF16)],
        ),
        out_shape=jax.ShapeDtypeStruct(xs.shape, F32),
        compiler_params=_cparams(("arbitrary",)),
        name="moe_experts",
    )(blk_e, n_used, xs, w_gate, w_up, w_down)


def _combine_kernel(pstart_ref, e_ref, r_ref, en_ref, rn_ref, ys_ref, base_ref, rw_ref, g2_ref,
                    b2_ref, o_ref, buf0, buf1, sem0, sem1):
    toks, D = base_ref.shape
    i = pl.program_id(0)
    last = pl.num_programs(0) - 1

    def issue(eref, rref, buf, sem):
        def gather_token(t, c):
            for k in range(TOP_K):
                a = t * TOP_K + k
                slot = pstart_ref[eref[a]] + rref[a]
                src = ys_ref.at[pl.ds(pl.multiple_of(slot * SUBLANES, SUBLANES), SUBLANES), :]
                dst = buf.at[pl.ds(pl.multiple_of((k * toks + t) * SUBLANES, SUBLANES), SUBLANES), :]
                pltpu.make_async_copy(src, dst, sem).start()
            return c

        lax.fori_loop(0, toks, gather_token, 0)

    def finish(buf, sem):
        pltpu.make_async_copy(ys_ref.at[pl.ds(0, buf.shape[0]), :], buf, sem).wait()
        w = rw_ref[...]
        pieces = []
        for s in range(D // LANES):
            acc = jnp.zeros((toks, LANES), F32)
            for k in range(TOP_K):
                rows = buf[pl.ds(k * toks * SUBLANES + s, toks, stride=SUBLANES), :]
                acc = acc + w[:, k:k + 1] * rows
            pieces.append(acc)
        y = base_ref[...] + jnp.concatenate(pieces, axis=-1)
        o_ref[...] = _layer_norm(y, g2_ref[...], b2_ref[...])

    @pl.when(i == 0)
    def _():
        issue(e_ref, r_ref, buf0, sem0)

    for parity, (cur, csem, nxt, nsem) in enumerate(((buf0, sem0, buf1, sem1), (buf1, sem1, buf0, sem0))):
        @pl.when(jnp.bitwise_and(i, 1) == parity)
        def _(cur=cur, csem=csem, nxt=nxt, nsem=nsem):
            @pl.when(i < last)
            def _():
                issue(en_ref, rn_ref, nxt, nsem)
            finish(cur, csem)


def _combine(pad_start, e_flat, r_flat, ys, base, rw, g2, b2):
    T, D = base.shape
    toks = min(COMB_ROWS, T)
    steps = T // toks
    idx_now = pl.BlockSpec((toks * TOP_K,), lambda i, *_: (i,), memory_space=pltpu.SMEM)
    idx_next = pl.BlockSpec((toks * TOP_K,), lambda i, *_: (jnp.minimum(i + 1, steps - 1),),
                            memory_space=pltpu.SMEM)
    return pl.pallas_call(
        _combine_kernel,
        grid_spec=pltpu.PrefetchScalarGridSpec(
            num_scalar_prefetch=1,
            grid=(steps,),
            in_specs=[
                idx_now, idx_now, idx_next, idx_next,
                pl.BlockSpec(memory_space=pl.ANY),
                pl.BlockSpec((toks, D), lambda i, *_: (i, 0)),
                pl.BlockSpec((toks, LANES), lambda i, *_: (i, 0)),
                pl.BlockSpec(g2.shape, lambda i, *_: (0, 0)),
                pl.BlockSpec(b2.shape, lambda i, *_: (0, 0)),
            ],
            out_specs=pl.BlockSpec((toks, D), lambda i, *_: (i, 0)),
            scratch_shapes=[
                pltpu.VMEM((TOP_K * toks * SUBLANES, LANES), F32),
                pltpu.VMEM((TOP_K * toks * SUBLANES, LANES), F32),
                pltpu.SemaphoreType.DMA(()),
                pltpu.SemaphoreType.DMA(()),
            ],
        ),
        out_shape=jax.ShapeDtypeStruct((T, D), F32),
        compiler_params=_cparams(("arbitrary",)),
        name="moe_combine",
    )(pad_start, e_flat, r_flat, e_flat, r_flat, ys, base, rw, g2, b2)


def _overlap_matrix(ncp):
    n = np.arange(ncp)[:, None]
    j = np.arange(LANES)[None, :]
    start = n * CMP_STRIDE
    end = start + CMP_LEN - 1
    sel_start = j * SEL_BLOCK
    ovl = (start < sel_start + SEL_BLOCK) & (end >= sel_start)
    return jnp.asarray(ovl.astype(np.float32), dtype=BF16)


def _mixer(x2, batch, seq, w_in, conv_w, cmp_k, cmp_v):
    c3 = 3 * CONV_CH
    qd = N_HEADS * HEAD_DIM
    w_conv = w_in[:, :c3].astype(BF16)
    w_q = w_in[:, c3:c3 + qd].astype(BF16)
    w_kv = w_in[:, c3 + qd:c3 + qd + 6 * KV_DIM].astype(BF16)
    w_g = jnp.pad(w_in[:, c3 + qd + 6 * KV_DIM:], ((0, 0), (0, LANES - N_HEADS * N_BRANCH))).astype(BF16)
    conv_out, q, kv, gates = _proj_conv(x2, w_conv, w_q, w_kv, w_g, conv_w, batch, seq)
    kc, vc = _compress(kv[:, :KV_DIM], kv[:, KV_DIM:2 * KV_DIM], cmp_k, cmp_v, batch, seq)
    ncp = -(-kc.shape[1] // LANES) * LANES
    if ncp != kc.shape[1]:
        padn = ((0, 0), (0, ncp - kc.shape[1]), (0, 0))
        kc, vc = jnp.pad(kc, padn), jnp.pad(vc, padn)
    kv3 = kv.reshape(batch, seq, 6 * KV_DIM)
    nsa_out = _nsa(q, kc, vc, kv3, gates, _overlap_matrix(ncp), batch, seq)
    return conv_out, nsa_out


def _moe(x3, base, route_i, route_w, counts, w_gate, w_up, w_down, g2, b2):
    T = base.shape[0]
    A = T * TOP_K
    n_blocks = -(-(A + N_EXPERTS * (SLOT_BLOCK - 1)) // SLOT_BLOCK)
    cnt = counts[0]
    padded = (cnt + SLOT_BLOCK - 1) // SLOT_BLOCK * SLOT_BLOCK
    pad_end = jnp.cumsum(padded)
    pad_start = (pad_end - padded).astype(I32)
    zero_off = jnp.where(padded > 0, pad_end - SLOT_BLOCK, -1).astype(I32)
    n_used = (pad_end[-1:] // SLOT_BLOCK).astype(I32)
    blk_start = jnp.arange(n_blocks, dtype=I32) * SLOT_BLOCK
    blk_e = jnp.minimum(jnp.sum((pad_end[None, :] <= blk_start[:, None]).astype(I32), axis=1),
                        N_EXPERTS - 1).astype(I32)
    e_flat = route_i[:, :TOP_K].reshape(A)
    r_flat = route_i[:, TOP_K:2 * TOP_K].reshape(A)
    xs = _push(pad_start, zero_off, e_flat, r_flat, x3, n_blocks * SLOT_BLOCK)
    ys = _experts(blk_e, n_used, xs, w_gate, w_up, w_down)
    return _combine(pad_start, e_flat, r_flat, ys, base, route_w, g2, b2)


def kernel(x, w_in, conv_w, ck_pos, ck_w1, ck_b1, ck_w2, cv_pos, cv_w1, cv_b1, cv_w2, w_out, ln1_g, ln1_b, router_w, router_bias, w_gate, w_up, w_down, ws_gate, ws_up, ws_down, ln2_g, ln2_b):
    batch, seq, D = x.shape
    depth = w_in.shape[0]
    alpha = (2.0 * depth) ** 0.25
    x2 = x.reshape(batch * seq, D)
    for l in range(depth):
        cmp_k = _compress_weights(ck_pos[l], ck_w1[l], ck_b1[l], ck_w2[l])
        cmp_v = _compress_weights(cv_pos[l], cv_w1[l], cv_b1[l], cv_w2[l])
        conv_out, nsa_out = _mixer(x2, batch, seq, w_in[l], conv_w[l], cmp_k, cmp_v)
        rw_hi, rw_lo = _split_bf16(router_w[l])
        x3, base, route_i, route_w, counts = _post(
            x2, conv_out, nsa_out, w_out[l].astype(BF16), ln1_g[l][None, :], ln1_b[l][None, :],
            rw_hi, rw_lo, router_bias[l][None, :],
            ws_gate[l].astype(BF16), ws_up[l].astype(BF16), ws_down[l].astype(BF16), alpha)
        x2 = _moe(x3, base, route_i, route_w, counts, w_gate[l], w_up[l], w_down[l],
                  ln2_g[l][None, :], ln2_b[l][None, :])
    return x2.reshape(batch, seq, D)
```

```python
import functools
import math

import jax
import jax.numpy as jnp
import numpy as np
from jax import lax
from jax.experimental import pallas as pl
from jax.experimental.pallas import tpu as pltpu

F32 = jnp.float32
BF16 = jnp.bfloat16
I32 = jnp.int32

CONV_CH = 512
CONV_WIDTH = 3
N_HEADS = 8
HEAD_DIM = 64
N_KV_HEADS = 2
Q_PER_KV = N_HEADS // N_KV_HEADS
KV_DIM = N_KV_HEADS * HEAD_DIM
N_BRANCH = 3
CMP_LEN = 32
CMP_STRIDE = 16
SEL_BLOCK = 64
SEL_TOPK = 8
WINDOW = 512
FORCED_SCORE = 1e4
N_EXPERTS = 256
TOP_K = 8
N_GROUPS = 8
TOPK_GROUPS = 4
GROUP_SIZE = N_EXPERTS // N_GROUPS
ROUTED_SCALE = 2.5
LN_EPS = 1e-5
NEG_INF = -1e30
SEL_SHIFT = SEL_BLOCK.bit_length() - 1
GROUP_SHIFT = GROUP_SIZE.bit_length() - 1
TOPK_SHIFT = TOP_K.bit_length() - 1

LANES = 128
SUBLANES = 8
VMEM_LIMIT = 56 * 1024 * 1024
DMA_PRIORITIES = 2

PROJ_ROWS = 512
NSA_Q = 128
NSA_KC = 512
POST_ROWS = 256
SLOT_BLOCK = 256
PUSH_ROWS = 512
COMB_ROWS = 256


def _dot(a, b):
    return jnp.dot(a, b, preferred_element_type=F32)


def _dot_t(a, b):
    return lax.dot_general(a, b, (((1,), (1,)), ((), ())), preferred_element_type=F32)


def _split_bf16(x):
    hi = x.astype(BF16)
    lo = (x - hi.astype(F32)).astype(BF16)
    return hi, lo


def _cparams(sem):
    return pltpu.CompilerParams(dimension_semantics=sem, vmem_limit_bytes=VMEM_LIMIT)


def _proj_conv_kernel(x_ref, wc_ref, wq_ref, wkv_ref, wg_ref, cw_ref,
                      conv_ref, q_ref, kv_ref, gate_ref, carry_ref):
    rows = x_ref.shape[0]

    @pl.when(pl.program_id(1) == 0)
    def _():
        carry_ref[...] = jnp.zeros_like(carry_ref)

    xb = x_ref[...].astype(BF16)
    acc = _dot(xb, wc_ref[...])
    b_g = acc[:, :CONV_CH]
    u = acc[:, CONV_CH:2 * CONV_CH] * acc[:, 2 * CONV_CH:]
    prev2 = carry_ref[SUBLANES - 2:SUBLANES - 1, :]
    prev1 = carry_ref[SUBLANES - 1:SUBLANES, :]
    ri = lax.broadcasted_iota(I32, (rows, 1), 0)
    u1 = jnp.where(ri == 0, prev1, pltpu.roll(u, 1, 0))
    u2 = jnp.where(ri == 0, prev2, jnp.where(ri == 1, prev1, pltpu.roll(u, 2, 0)))
    y = cw_ref[0:1, :] * u2 + cw_ref[1:2, :] * u1 + cw_ref[2:3, :] * u
    conv_ref[...] = (b_g * y).astype(BF16)
    carry_ref[...] = u[rows - SUBLANES:, :]

    q_ref[...] = (_dot(xb, wq_ref[...]) * (HEAD_DIM ** -0.5)).astype(BF16)
    kv_ref[...] = _dot(xb, wkv_ref[...]).astype(BF16)
    gate_ref[...] = jax.nn.sigmoid(_dot(xb, wg_ref[...]))


def _proj_conv(x2, w_conv, w_q, w_kv, w_g, conv_w, batch, seq):
    T, D = x2.shape
    rows = min(PROJ_ROWS, seq)
    nt = seq // rows
    row_map = lambda b, i: (b * nt + i, 0)
    fixed = lambda b, i: (0, 0)
    return pl.pallas_call(
        _proj_conv_kernel,
        grid=(batch, nt),
        in_specs=[
            pl.BlockSpec((rows, D), row_map),
            pl.BlockSpec(w_conv.shape, fixed),
            pl.BlockSpec(w_q.shape, fixed),
            pl.BlockSpec(w_kv.shape, fixed),
            pl.BlockSpec(w_g.shape, fixed),
            pl.BlockSpec(conv_w.shape, fixed),
        ],
        out_specs=[
            pl.BlockSpec((rows, CONV_CH), row_map),
            pl.BlockSpec((rows, N_HEADS * HEAD_DIM), row_map),
            pl.BlockSpec((rows, 6 * KV_DIM), row_map),
            pl.BlockSpec((rows, LANES), row_map),
        ],
        out_shape=[
            jax.ShapeDtypeStruct((T, CONV_CH), BF16),
            jax.ShapeDtypeStruct((T, N_HEADS * HEAD_DIM), BF16),
            jax.ShapeDtypeStruct((T, 6 * KV_DIM), BF16),
            jax.ShapeDtypeStruct((T, LANES), F32),
        ],
        scratch_shapes=[pltpu.VMEM((SUBLANES, CONV_CH), F32)],
        compiler_params=_cparams(("arbitrary", "arbitrary")),
        name="proj_conv",
    )(x2, w_conv, w_q, w_kv, w_g, conv_w)


def _compress_kernel(ck_ref, cv_ref, wtk_ref, wbk_ref, w2k_ref, ptk_ref, pbk_ref, b1k_ref,
                     wtv_ref, wbv_ref, w2v_ref, ptv_ref, pbv_ref, b1v_ref, kc_ref, vc_ref):
    def one(c_ref, wt_ref, wb_ref, w2_ref, pt_ref, pb_ref, b1_ref, o_ref):
        c = c_ref[0]
        top = _dot(c, wt_ref[...])
        bot = _dot(c, wb_ref[...])
        c0 = _dot(pt_ref[...], wt_ref[...]) + _dot(pb_ref[...], wb_ref[...]) + b1_ref[...]
        n = top.shape[0]
        h = top + pltpu.roll(bot, n - 1, 0) + c0[0:1, :]
        g = jax.nn.gelu(h, approximate=True)
        o_ref[0] = _dot(g.astype(BF16), w2_ref[...]).astype(BF16)

    one(ck_ref, wtk_ref, wbk_ref, w2k_ref, ptk_ref, pbk_ref, b1k_ref, kc_ref)
    one(cv_ref, wtv_ref, wbv_ref, w2v_ref, ptv_ref, pbv_ref, b1v_ref, vc_ref)


def _blockdiag2(w):
    z = jnp.zeros_like(w)
    return jnp.concatenate([jnp.concatenate([w, z], 1), jnp.concatenate([z, w], 1)], 0)


def _compress_weights(pos, w1, b1, w2):
    w1r = w1.reshape(CMP_LEN, HEAD_DIM, HEAD_DIM)
    eye = jnp.eye(N_KV_HEADS, dtype=w1.dtype)
    wfull = (w1r[:, None, :, None, :] * eye[None, :, None, :, None]).reshape(CMP_LEN, KV_DIM, KV_DIM)
    w_top = wfull[:CMP_STRIDE].reshape(CMP_STRIDE * KV_DIM, KV_DIM).astype(BF16)
    w_bot = wfull[CMP_STRIDE:].reshape(CMP_STRIDE * KV_DIM, KV_DIM).astype(BF16)
    posr = jnp.tile(pos, (1, N_KV_HEADS))
    pos_top = jnp.tile(posr[:CMP_STRIDE].reshape(1, -1), (SUBLANES, 1)).astype(BF16)
    pos_bot = jnp.tile(posr[CMP_STRIDE:].reshape(1, -1), (SUBLANES, 1)).astype(BF16)
    b1r = jnp.tile(b1[None, :], (SUBLANES, N_KV_HEADS)).astype(F32)
    return w_top, w_bot, _blockdiag2(w2).astype(BF16), pos_top, pos_bot, b1r


def _compress(kc_raw, vc_raw, wk, wv, batch, seq):
    chunks = seq // CMP_STRIDE
    width = CMP_STRIDE * KV_DIM
    ck = kc_raw.reshape(batch, chunks, width)
    cv = vc_raw.reshape(batch, chunks, width)
    bmap = lambda b: (b, 0, 0)
    fixed = lambda b: (0, 0)
    wspecs = [pl.BlockSpec(w.shape, fixed) for w in wk]
    return pl.pallas_call(
        _compress_kernel,
        grid=(batch,),
        in_specs=[pl.BlockSpec((1, chunks, width), bmap), pl.BlockSpec((1, chunks, width), bmap)]
        + wspecs + wspecs,
        out_specs=[pl.BlockSpec((1, chunks, KV_DIM), bmap)] * 2,
        out_shape=[jax.ShapeDtypeStruct((batch, chunks, KV_DIM), BF16)] * 2,
        compiler_params=_cparams(("arbitrary",)),
        name="compress",
    )(ck, cv, *wk, *wv)


def _softmax_rows(s, valid, may_be_empty):
    s = jnp.where(valid, s, NEG_INF)
    m = jnp.max(s, axis=-1, keepdims=True)
    p = jnp.exp(s - m)
    if may_be_empty:
        p = jnp.where(valid, p, 0.0)
    l = jnp.sum(p, axis=-1, keepdims=True)
    inv = jnp.where(l > 0.0, 1.0 / l, 0.0)
    return p, inv


def _nsa_kernel(q_ref, kc_ref, vc_ref, ks_ref, vs_ref, kw_ref, vw_ref, gate_ref, ovl_ref,
                o_ref, *, seq, n_sel):
    tq = q_ref.shape[0]
    ncp = kc_ref.shape[1]
    rows = Q_PER_KV * tq
    q0 = pl.program_id(1) * tq
    t_col = q0 + lax.broadcasted_iota(I32, (tq, 1), 0)
    t4 = jnp.concatenate([t_col] * Q_PER_KV, axis=0)
    row_i = lax.broadcasted_iota(I32, (rows, 1), 0)
    lane = lax.broadcasted_iota(I32, (1, LANES), 1)
    lane_f = lane.astype(F32)
    gates = gate_ref[...]
    win_len = WINDOW + tq
    w_start = pl.multiple_of(jnp.maximum(q0 - WINDOW, 0), tq)
    n_chunks = (q0 + tq + NSA_KC - 1) // NSA_KC
    outs = []
    for g in range(N_KV_HEADS):
        lo, hi = g * HEAD_DIM, (g + 1) * HEAD_DIM
        qg = jnp.concatenate(
            [q_ref[:, (g * Q_PER_KV + r) * HEAD_DIM:(g * Q_PER_KV + r + 1) * HEAD_DIM]
             for r in range(Q_PER_KV)], axis=0)
        slope = jnp.zeros((rows, 1), F32)
        for r in range(Q_PER_KV):
            h = g * Q_PER_KV + r
            in_head = (row_i >= r * tq) & (row_i < (r + 1) * tq)
            slope = jnp.where(in_head, 2.0 ** (-8.0 * (h + 1) / N_HEADS), slope)

        cmp_end = lax.broadcasted_iota(I32, (1, ncp), 1) * CMP_STRIDE + (CMP_LEN - 1)
        d_c = t4 - cmp_end
        s_c = _dot_t(qg, kc_ref[0, :, lo:hi]) - slope * d_c.astype(F32)
        p_c, inv_c = _softmax_rows(s_c, d_c >= 0, may_be_empty=True)
        p_c = p_c * inv_c
        o_cmp = _dot(p_c.astype(BF16), vc_ref[0, :, lo:hi])

        ps = p_c[0:tq]
        for r in range(1, Q_PER_KV):
            ps = ps + p_c[r * tq:(r + 1) * tq]
        ps_hi, ps_lo = _split_bf16(ps)
        imp = _dot(ps_hi, ovl_ref[...]) + _dot(ps_lo, ovl_ref[...])
        forced = (lane == 0) | (lane == jnp.right_shift(t_col, SEL_SHIFT))
        causal = lane * SEL_BLOCK <= t_col
        score = jnp.where(forced, FORCED_SCORE, jnp.where(causal, imp, -1.0))
        score = jnp.where(lane < n_sel, score, -jnp.inf)
        sel = jnp.zeros((tq, LANES), F32)
        for _ in range(min(SEL_TOPK, n_sel)):
            mx = jnp.max(score, axis=-1, keepdims=True)
            first = jnp.min(jnp.where(score == mx, lane_f, float(LANES)), axis=-1, keepdims=True)
            hit = lane_f == first
            sel = jnp.where(hit, 1.0, sel)
            score = jnp.where(hit, -jnp.inf, score)
        unsel_b = jnp.where(sel > 0.5, 0.0, NEG_INF).astype(BF16)

        def sel_chunk(c, carry):
            m, l, acc = carry
            k0 = pl.multiple_of(c * NSA_KC, NSA_KC)
            kch = ks_ref[0, pl.ds(k0, NSA_KC), lo:hi]
            vch = vs_ref[0, pl.ds(k0, NSA_KC), lo:hi]
            pos = k0 + lax.broadcasted_iota(I32, (1, NSA_KC), 1)
            blk = lax.broadcasted_iota(I32, (LANES, NSA_KC), 0)
            kblk = jnp.right_shift(k0 + lax.broadcasted_iota(I32, (LANES, NSA_KC), 1), SEL_SHIFT)
            expand = jnp.where(blk == kblk, 1.0, 0.0).astype(BF16)
            mask = jnp.where(pos <= t_col, _dot(unsel_b, expand), NEG_INF)
            mask4 = jnp.concatenate([mask] * Q_PER_KV, axis=0)
            s = _dot_t(qg, kch) + (slope * (pos - q0).astype(F32) + mask4)
            m_new = jnp.maximum(m, jnp.max(s, axis=-1, keepdims=True))
            a = jnp.exp(m - m_new)
            p = jnp.exp(s - m_new)
            l = a * l + jnp.sum(p, axis=-1, keepdims=True)
            acc = a * acc + _dot(p.astype(BF16), vch)
            return m_new, l, acc

        m0 = jnp.full((rows, 1), NEG_INF, F32)
        l0 = jnp.zeros((rows, 1), F32)
        a0 = jnp.zeros((rows, HEAD_DIM), F32)
        _, l_s, acc_s = lax.fori_loop(0, n_chunks, sel_chunk, (m0, l0, a0))
        o_slc = acc_s * jnp.where(l_s > 0.0, 1.0 / l_s, 0.0)

        kwb = kw_ref[0, pl.ds(w_start, win_len), lo:hi]
        vwb = vw_ref[0, pl.ds(w_start, win_len), lo:hi]
        pos_w = w_start + lax.broadcasted_iota(I32, (1, win_len), 1)
        d_w = t4 - pos_w
        s_w = _dot_t(qg, kwb) + slope * (pos_w - q0).astype(F32)
        p_w, inv_w = _softmax_rows(s_w, (d_w >= 0) & (d_w < WINDOW), may_be_empty=False)
        o_win = _dot(p_w.astype(BF16), vwb) * inv_w

        for r in range(Q_PER_KV):
            h = g * Q_PER_KV + r
            sl = slice(r * tq, (r + 1) * tq)
            gc = gates[:, h * N_BRANCH:h * N_BRANCH + 1]
            gs = gates[:, h * N_BRANCH + 1:h * N_BRANCH + 2]
            gw = gates[:, h * N_BRANCH + 2:h * N_BRANCH + 3]
            outs.append(gc * o_cmp[sl] + gs * o_slc[sl] + gw * o_win[sl])
    o_ref[...] = jnp.concatenate(outs, axis=-1).astype(BF16)


def _nsa(q, kc, vc, kv3, gates, overlap, batch, seq):
    T = q.shape[0]
    tq = min(NSA_Q, seq)
    nq = seq // tq
    ncp = kc.shape[1]
    n_sel = seq // SEL_BLOCK
    row_map = lambda b, i: (b * nq + i, 0)
    bmap = lambda b, i: (b, 0, 0)
    kvspec = lambda j: pl.BlockSpec((1, seq, KV_DIM), lambda b, i, j=j: (b, 0, j))
    return pl.pallas_call(
        functools.partial(_nsa_kernel, seq=seq, n_sel=n_sel),
        grid=(batch, nq),
        in_specs=[
            pl.BlockSpec((tq, N_HEADS * HEAD_DIM), row_map),
            pl.BlockSpec((1, ncp, KV_DIM), bmap),
            pl.BlockSpec((1, ncp, KV_DIM), bmap),
            kvspec(2), kvspec(3), kvspec(4), kvspec(5),
            pl.BlockSpec((tq, LANES), row_map),
            pl.BlockSpec(overlap.shape, lambda b, i: (0, 0)),
        ],
        out_specs=pl.BlockSpec((tq, N_HEADS * HEAD_DIM), row_map),
        out_shape=jax.ShapeDtypeStruct((T, N_HEADS * HEAD_DIM), BF16),
        compiler_params=_cparams(("arbitrary", "arbitrary")),
        name="nsa",
    )(q, kc, vc, kv3, kv3, kv3, kv3, gates, overlap)


def _layer_norm(y, g, b):
    mu = jnp.mean(y, axis=-1, keepdims=True)
    yc = y - mu
    var = jnp.mean(yc * yc, axis=-1, keepdims=True)
    return yc * lax.rsqrt(var + LN_EPS) * g + b


def _post_kernel(x_ref, conv_ref, nsa_ref, wo_ref, g1_ref, b1_ref, rwh_ref, rwl_ref, rb_ref,
                 wsg_ref, wsu_ref, wsd_ref,
                 x3_ref, base_ref, e_ref, r_ref, w_ref, cnt_ref, carry_ref, *, alpha):
    rows, D = x_ref.shape

    @pl.when(pl.program_id(0) == 0)
    def _():
        carry_ref[...] = jnp.zeros_like(carry_ref)

    half = wo_ref.shape[0] // 2
    mix = _dot(conv_ref[...], wo_ref[:half, :]) + _dot(nsa_ref[...], wo_ref[half:, :])
    x1 = _layer_norm(alpha * x_ref[...] + mix, g1_ref[...], b1_ref[...])
    for s in range(D // LANES):
        x3_ref[pl.ds(s, rows, stride=SUBLANES), :] = x1[:, s * LANES:(s + 1) * LANES]

    xh, xl = _split_bf16(x1)
    hid = jax.nn.silu(_dot(xh, wsg_ref[...])) * _dot(xh, wsu_ref[...])
    base_ref[...] = alpha * x1 + _dot(hid.astype(BF16), wsd_ref[...])

    logits = (_dot_t(rwh_ref[...], xh) + _dot_t(rwh_ref[...], xl) + _dot_t(rwl_ref[...], xh))
    scores = jax.nn.sigmoid(logits)
    reps = rows // LANES
    biased = scores + jnp.concatenate([rb_ref[...]] * reps, axis=1)
    eidx = lax.broadcasted_iota(I32, (N_EXPERTS, rows), 0).astype(F32)
    gidx = lax.broadcasted_iota(I32, (GROUP_SIZE, rows), 0).astype(F32)
    gvals, gscore = [], []
    for gi in range(N_GROUPS):
        v = biased[gi * GROUP_SIZE:(gi + 1) * GROUP_SIZE, :]
        m1 = jnp.max(v, axis=0, keepdims=True)
        i1 = jnp.min(jnp.where(v == m1, gidx, float(GROUP_SIZE)), axis=0, keepdims=True)
        m2 = jnp.max(jnp.where(gidx == i1, -jnp.inf, v), axis=0, keepdims=True)
        gvals.append(v)
        gscore.append(m1 + m2)
    cands = []
    for gi in range(N_GROUPS):
        ahead = jnp.zeros((1, rows), F32)
        for gj in range(N_GROUPS):
            if gj == gi:
                continue
            beats = (gscore[gj] >= gscore[gi]) if gj < gi else (gscore[gj] > gscore[gi])
            ahead = ahead + jnp.where(beats, 1.0, 0.0)
        ahead_full = jnp.broadcast_to(ahead, gvals[gi].shape)
        cands.append(jnp.where(ahead_full < float(TOPK_GROUPS), gvals[gi], NEG_INF))
    cand = jnp.concatenate(cands, axis=0)
    onehot = jnp.zeros((N_EXPERTS, rows), F32)
    idx_rows, w_rows = [], []
    for _ in range(TOP_K):
        mx = jnp.max(cand, axis=0, keepdims=True)
        first = jnp.min(jnp.where(cand == mx, eidx, float(N_EXPERTS)), axis=0, keepdims=True)
        hit = eidx == first
        idx_rows.append(first)
        w_rows.append(jnp.sum(jnp.where(hit, scores, 0.0), axis=0, keepdims=True))
        onehot = jnp.where(hit, 1.0, onehot)
        cand = jnp.where(hit, -jnp.inf, cand)
    wsum = w_rows[0]
    for k in range(1, TOP_K):
        wsum = wsum + w_rows[k]

    ti = lax.broadcasted_iota(I32, (rows, rows), 0)
    tj = lax.broadcasted_iota(I32, (rows, rows), 1)
    earlier = jnp.where(ti < tj, 1.0, 0.0).astype(BF16)
    carry = carry_ref[...]
    before = _dot(onehot.astype(BF16), earlier) + jnp.concatenate([carry] * reps, axis=1)
    krow = lax.broadcasted_iota(I32, (TOP_K, rows), 0)
    e_out = jnp.zeros((TOP_K, rows), F32)
    r_out = jnp.zeros((TOP_K, rows), F32)
    w_out = jnp.zeros((TOP_K, rows), F32)
    for k in range(TOP_K):
        rank = jnp.sum(jnp.where(eidx == idx_rows[k], before, 0.0), axis=0, keepdims=True)
        e_out = jnp.where(krow == k, idx_rows[k], e_out)
        r_out = jnp.where(krow == k, rank, r_out)
        w_out = jnp.where(krow == k, w_rows[k] / wsum * ROUTED_SCALE, w_out)
    e_ref[...] = e_out.astype(I32)
    r_ref[...] = r_out.astype(I32)
    w_ref[...] = w_out
    total = carry + jnp.sum(onehot, axis=1, keepdims=True)
    carry_ref[...] = total
    cnt_ref[...] = total.astype(I32)


def _post(x2, conv_out, nsa_out, w_out, g1, b1, rw_hi, rw_lo, rbias, wsg, wsu, wsd, alpha):
    T, D = x2.shape
    rows = min(POST_ROWS, T)
    row_map = lambda i: (i, 0)
    col_map = lambda i: (0, i)
    fixed = lambda i: (0, 0)
    full = lambda a: pl.BlockSpec(a.shape, fixed)
    return pl.pallas_call(
        functools.partial(_post_kernel, alpha=alpha),
        grid=(T // rows,),
        in_specs=[
            pl.BlockSpec((rows, D), row_map),
            pl.BlockSpec((rows, CONV_CH), row_map),
            pl.BlockSpec((rows, N_HEADS * HEAD_DIM), row_map),
            full(w_out), full(g1), full(b1), full(rw_hi), full(rw_lo), full(rbias),
            full(wsg), full(wsu), full(wsd),
        ],
        out_specs=[
            pl.BlockSpec((rows * SUBLANES, LANES), row_map),
            pl.BlockSpec((rows, D), row_map),
            pl.BlockSpec((TOP_K, rows), col_map),
            pl.BlockSpec((TOP_K, rows), col_map),
            pl.BlockSpec((TOP_K, rows), col_map),
            pl.BlockSpec((N_EXPERTS, LANES), fixed),
        ],
        out_shape=[
            jax.ShapeDtypeStruct((T * SUBLANES, LANES), F32),
            jax.ShapeDtypeStruct((T, D), F32),
            jax.ShapeDtypeStruct((TOP_K, T), I32),
            jax.ShapeDtypeStruct((TOP_K, T), I32),
            jax.ShapeDtypeStruct((TOP_K, T), F32),
            jax.ShapeDtypeStruct((N_EXPERTS, LANES), I32),
        ],
        scratch_shapes=[pltpu.VMEM((N_EXPERTS, LANES), F32)],
        compiler_params=_cparams(("arbitrary",)),
        name="post_attn_router",
    )(x2, conv_out, nsa_out, w_out, g1, b1, rw_hi, rw_lo, rbias, wsg, wsu, wsd)


def _push_kernel(pstart_ref, zoff_ref, e_ref, r_ref, x3_ref, xs_ref, zero_ref, sem, zsem):
    toks = x3_ref.shape[0] // SUBLANES
    zrows = zero_ref.shape[0]

    def zero_copy(e):
        off = pl.multiple_of(zoff_ref[e] * SUBLANES, SUBLANES)
        return pltpu.make_async_copy(zero_ref, xs_ref.at[pl.ds(off, zrows), :], zsem)

    @pl.when(pl.program_id(0) == 0)
    def _():
        zero_ref[...] = jnp.zeros_like(zero_ref)

        def start(e, c):
            @pl.when(zoff_ref[e] >= 0)
            def _():
                zero_copy(e).start()
            return c

        def wait(e, c):
            @pl.when(zoff_ref[e] >= 0)
            def _():
                zero_copy(e).wait()
            return c

        lax.fori_loop(0, N_EXPERTS, start, 0)
        lax.fori_loop(0, N_EXPERTS, wait, 0)

    def push_token(t, c):
        src = x3_ref.at[pl.ds(pl.multiple_of(t * SUBLANES, SUBLANES), SUBLANES), :]
        for k in range(TOP_K):
            a = t * TOP_K + k
            slot = pstart_ref[e_ref[a]] + r_ref[a]
            dst = xs_ref.at[pl.ds(pl.multiple_of(slot * SUBLANES, SUBLANES), SUBLANES), :]
            pltpu.make_async_copy(src, dst, sem).start(priority=k % DMA_PRIORITIES)
        return c

    lax.fori_loop(0, toks, push_token, 0)
    for _ in range(TOP_K):
        pltpu.make_async_copy(x3_ref, xs_ref.at[pl.ds(0, toks * SUBLANES), :], sem).wait()


def _push(pad_start, zero_off, e_flat, r_flat, x3, n_slots):
    T = x3.shape[0] // SUBLANES
    toks = min(PUSH_ROWS, T)
    return pl.pallas_call(
        _push_kernel,
        grid_spec=pltpu.PrefetchScalarGridSpec(
            num_scalar_prefetch=2,
            grid=(T // toks,),
            in_specs=[
                pl.BlockSpec((toks * TOP_K,), lambda i, *_: (i,), memory_space=pltpu.SMEM),
                pl.BlockSpec((toks * TOP_K,), lambda i, *_: (i,), memory_space=pltpu.SMEM),
                pl.BlockSpec((toks * SUBLANES, LANES), lambda i, *_: (i, 0)),
            ],
            out_specs=pl.BlockSpec(memory_space=pl.ANY),
            scratch_shapes=[
                pltpu.VMEM((SLOT_BLOCK * SUBLANES, LANES), F32),
                pltpu.SemaphoreType.DMA(()),
                pltpu.SemaphoreType.DMA(()),
            ],
        ),
        out_shape=jax.ShapeDtypeStruct((n_slots * SUBLANES, LANES), F32),
        compiler_params=_cparams(("arbitrary",)),
        name="moe_push",
    )(pad_start, zero_off, e_flat, r_flat, x3)


def _expert_kernel(blk_e_ref, nused_ref, xs_ref, wg_ref, wu_ref, wd_ref, ys_ref, wgu_s, wd_s):
    b = pl.program_id(0)
    rows = xs_ref.shape[0] // SUBLANES
    D = wg_ref.shape[1]
    H = wg_ref.shape[2]
    prev = blk_e_ref[jnp.maximum(b - 1, 0)]

    @pl.when((b == 0) | (blk_e_ref[b] != prev))
    def _():
        wgu_s[:, :H] = wg_ref[0].astype(BF16)
        wgu_s[:, H:] = wu_ref[0].astype(BF16)
        wd_s[...] = wd_ref[0].astype(BF16)

    @pl.when(b < nused_ref[0])
    def _():
        xb = jnp.concatenate(
            [xs_ref[pl.ds(s, rows, stride=SUBLANES), :].astype(BF16) for s in range(D // LANES)],
            axis=-1)
        h = _dot(xb, wgu_s[...])
        act = (jax.nn.silu(h[:, :H]) * h[:, H:]).astype(BF16)
        out = _dot(act, wd_s[...])
        for s in range(D // LANES):
            ys_ref[pl.ds(s, rows, stride=SUBLANES), :] = out[:, s * LANES:(s + 1) * LANES]

    @pl.when(b >= nused_ref[0])
    def _():
        ys_ref[...] = jnp.zeros_like(ys_ref)


def _experts(blk_e, n_used, xs, w_gate, w_up, w_down):
    n_blocks = blk_e.shape[0]
    E, D, H = w_gate.shape
    rows = SLOT_BLOCK
    return pl.pallas_call(
        _expert_kernel,
        grid_spec=pltpu.PrefetchScalarGridSpec(
            num_scalar_prefetch=2,
            grid=(n_blocks,),
            in_specs=[
                pl.BlockSpec((rows * SUBLANES, LANES), lambda b, be, nu: (b, 0)),
                pl.BlockSpec((1, D, H), lambda b, be, nu: (be[b], 0, 0)),
                pl.BlockSpec((1, D, H), lambda b, be, nu: (be[b], 0, 0)),
                pl.BlockSpec((1, H, D), lambda b, be, nu: (be[b], 0, 0)),
            ],
            out_specs=pl.BlockSpec((rows * SUBLANES, LANES), lambda b, be, nu: (b, 0)),
            scratch_shapes=[pltpu.VMEM((D, 2 * H), BF16), pltpu.VMEM((H, D), BF16)],
        ),
        out_shape=jax.ShapeDtypeStruct(xs.shape, F32),
        compiler_params=_cparams(("arbitrary",)),
        name="moe_experts",
    )(blk_e, n_used, xs, w_gate, w_up, w_down)


def _combine_kernel(pstart_ref, e_ref, r_ref, en_ref, rn_ref, ys_ref, base_ref, rw_ref, g2_ref,
                    b2_ref, o_ref, buf0, buf1, sem0, sem1):
    toks, D = base_ref.shape
    i = pl.program_id(0)
    last = pl.num_programs(0) - 1

    def issue(eref, rref, buf, sem):
        def gather_token(t, c):
            for k in range(TOP_K):
                a = t * TOP_K + k
                slot = pstart_ref[eref[a]] + rref[a]
                src = ys_ref.at[pl.ds(pl.multiple_of(slot * SUBLANES, SUBLANES), SUBLANES), :]
                dst = buf.at[pl.ds(pl.multiple_of((k * toks + t) * SUBLANES, SUBLANES), SUBLANES), :]
                pltpu.make_async_copy(src, dst, sem).start(priority=k % DMA_PRIORITIES)
            return c

        lax.fori_loop(0, toks, gather_token, 0)

    def finish(buf, sem):
        pltpu.make_async_copy(ys_ref.at[pl.ds(0, buf.shape[0]), :], buf, sem).wait()
        w = rw_ref[...]
        pieces = []
        for s in range(D // LANES):
            acc = jnp.zeros((toks, LANES), F32)
            for k in range(TOP_K):
                rows = buf[pl.ds(k * toks * SUBLANES + s, toks, stride=SUBLANES), :]
                acc = acc + w[:, k:k + 1] * rows
            pieces.append(acc)
        y = base_ref[...] + jnp.concatenate(pieces, axis=-1)
        o_ref[...] = _layer_norm(y, g2_ref[...], b2_ref[...])

    @pl.when(i == 0)
    def _():
        issue(e_ref, r_ref, buf0, sem0)

    for parity, (cur, csem, nxt, nsem) in enumerate(((buf0, sem0, buf1, sem1), (buf1, sem1, buf0, sem0))):
        @pl.when(jnp.bitwise_and(i, 1) == parity)
        def _(cur=cur, csem=csem, nxt=nxt, nsem=nsem):
            @pl.when(i < last)
            def _():
                issue(en_ref, rn_ref, nxt, nsem)
            finish(cur, csem)


def _combine(pad_start, e_flat, r_flat, ys, base, rw, g2, b2):
    T, D = base.shape
    toks = min(COMB_ROWS, T)
    steps = T // toks
    idx_now = pl.BlockSpec((toks * TOP_K,), lambda i, *_: (i,), memory_space=pltpu.SMEM)
    idx_next = pl.BlockSpec((toks * TOP_K,), lambda i, *_: (jnp.minimum(i + 1, steps - 1),),
                            memory_space=pltpu.SMEM)
    return pl.pallas_call(
        _combine_kernel,
        grid_spec=pltpu.PrefetchScalarGridSpec(
            num_scalar_prefetch=1,
            grid=(steps,),
            in_specs=[
                idx_now, idx_now, idx_next, idx_next,
                pl.BlockSpec(memory_space=pl.ANY),
                pl.BlockSpec((toks, D), lambda i, *_: (i, 0)),
                pl.BlockSpec((toks, TOP_K), lambda i, *_: (i, 0)),
                pl.BlockSpec(g2.shape, lambda i, *_: (0, 0)),
                pl.BlockSpec(b2.shape, lambda i, *_: (0, 0)),
            ],
            out_specs=pl.BlockSpec((toks, D), lambda i, *_: (i, 0)),
            scratch_shapes=[
                pltpu.VMEM((TOP_K * toks * SUBLANES, LANES), F32),
                pltpu.VMEM((TOP_K * toks * SUBLANES, LANES), F32),
                pltpu.SemaphoreType.DMA(()),
                pltpu.SemaphoreType.DMA(()),
            ],
        ),
        out_shape=jax.ShapeDtypeStruct((T, D), F32),
        compiler_params=_cparams(("arbitrary",)),
        name="moe_combine",
    )(pad_start, e_flat, r_flat, e_flat, r_flat, ys, base, rw, g2, b2)


def _overlap_matrix(ncp):
    n = np.arange(ncp)[:, None]
    j = np.arange(LANES)[None, :]
    start = n * CMP_STRIDE
    end = start + CMP_LEN - 1
    sel_start = j * SEL_BLOCK
    ovl = (start < sel_start + SEL_BLOCK) & (end >= sel_start)
    return jnp.asarray(ovl.astype(np.float32), dtype=BF16)


def _mixer(x2, batch, seq, w_in, conv_w, cmp_k, cmp_v):
    c3 = 3 * CONV_CH
    qd = N_HEADS * HEAD_DIM
    w_conv = w_in[:, :c3].astype(BF16)
    w_q = w_in[:, c3:c3 + qd].astype(BF16)
    w_kv = w_in[:, c3 + qd:c3 + qd + 6 * KV_DIM].astype(BF16)
    w_g = jnp.pad(w_in[:, c3 + qd + 6 * KV_DIM:], ((0, 0), (0, LANES - N_HEADS * N_BRANCH))).astype(BF16)
    conv_out, q, kv, gates = _proj_conv(x2, w_conv, w_q, w_kv, w_g, conv_w, batch, seq)
    kc, vc = _compress(kv[:, :KV_DIM], kv[:, KV_DIM:2 * KV_DIM], cmp_k, cmp_v, batch, seq)
    ncp = -(-kc.shape[1] // LANES) * LANES
    if ncp != kc.shape[1]:
        padn = ((0, 0), (0, ncp - kc.shape[1]), (0, 0))
        kc, vc = jnp.pad(kc, padn), jnp.pad(vc, padn)
    kv3 = kv.reshape(batch, seq, 6 * KV_DIM)
    nsa_out = _nsa(q, kc, vc, kv3, gates, _overlap_matrix(ncp), batch, seq)
    return conv_out, nsa_out


def _moe(x3, base, e_t, r_t, w_t, counts, w_gate, w_up, w_down, g2, b2):
    T = base.shape[0]
    A = T * TOP_K
    n_blocks = -(-(A + N_EXPERTS * (SLOT_BLOCK - 1)) // SLOT_BLOCK)
    cnt = counts[:, 0]
    padded = (cnt + SLOT_BLOCK - 1) // SLOT_BLOCK * SLOT_BLOCK
    pad_end = jnp.cumsum(padded)
    pad_start = (pad_end - padded).astype(I32)
    zero_off = jnp.where(padded > 0, pad_end - SLOT_BLOCK, -1).astype(I32)
    n_used = (pad_end[-1:] // SLOT_BLOCK).astype(I32)
    blk_start = jnp.arange(n_blocks, dtype=I32) * SLOT_BLOCK
    blk_e = jnp.minimum(jnp.sum((pad_end[None, :] <= blk_start[:, None]).astype(I32), axis=1),
                        N_EXPERTS - 1).astype(I32)
    e_flat = e_t.T.reshape(A)
    r_flat = r_t.T.reshape(A)
    xs = _push(pad_start, zero_off, e_flat, r_flat, x3, n_blocks * SLOT_BLOCK)
    ys = _experts(blk_e, n_used, xs, w_gate, w_up, w_down)
    return _combine(pad_start, e_flat, r_flat, ys, base, w_t.T, g2, b2)


def kernel(x, w_in, conv_w, ck_pos, ck_w1, ck_b1, ck_w2, cv_pos, cv_w1, cv_b1, cv_w2, w_out, ln1_g, ln1_b, router_w, router_bias, w_gate, w_up, w_down, ws_gate, ws_up, ws_down, ln2_g, ln2_b):
    batch, seq, D = x.shape
    depth = w_in.shape[0]
    alpha = (2.0 * depth) ** 0.25
    x2 = x.reshape(batch * seq, D)
    for l in range(depth):
        cmp_k = _compress_weights(ck_pos[l], ck_w1[l], ck_b1[l], ck_w2[l])
        cmp_v = _compress_weights(cv_pos[l], cv_w1[l], cv_b1[l], cv_w2[l])
        conv_out, nsa_out = _mixer(x2, batch, seq, w_in[l], conv_w[l], cmp_k, cmp_v)
        rw_hi, rw_lo = _split_bf16(router_w[l].T)
        rbias = jnp.broadcast_to(router_bias[l][:, None], (N_EXPERTS, LANES))
        x3, base, e_t, r_t, w_t, counts = _post(
            x2, conv_out, nsa_out, w_out[l].astype(BF16), ln1_g[l][None, :], ln1_b[l][None, :],
            rw_hi, rw_lo, rbias,
            ws_gate[l].astype(BF16), ws_up[l].astype(BF16), ws_down[l].astype(BF16), alpha)
        x2 = _moe(x3, base, e_t, r_t, w_t, counts, w_gate[l], w_up[l], w_down[l],
                  ln2_g[l][None, :], ln2_b[l][None, :])
    return x2.reshape(batch, seq, D)
```

```python
import functools
import math

import jax
import jax.numpy as jnp
import numpy as np
from jax import lax
from jax.experimental import pallas as pl
from jax.experimental.pallas import tpu as pltpu

F32 = jnp.float32
BF16 = jnp.bfloat16
I32 = jnp.int32

CONV_CH = 512
CONV_WIDTH = 3
N_HEADS = 8
HEAD_DIM = 64
N_KV_HEADS = 2
Q_PER_KV = N_HEADS // N_KV_HEADS
KV_DIM = N_KV_HEADS * HEAD_DIM
N_BRANCH = 3
CMP_LEN = 32
CMP_STRIDE = 16
SEL_BLOCK = 64
SEL_TOPK = 8
WINDOW = 512
FORCED_SCORE = 1e4
N_EXPERTS = 256
TOP_K = 8
N_GROUPS = 8
TOPK_GROUPS = 4
GROUP_SIZE = N_EXPERTS // N_GROUPS
ROUTED_SCALE = 2.5
LN_EPS = 1e-5
NEG_INF = -1e30
SEL_SHIFT = SEL_BLOCK.bit_length() - 1
GROUP_SHIFT = GROUP_SIZE.bit_length() - 1
TOPK_SHIFT = TOP_K.bit_length() - 1

LANES = 128
SUBLANES = 8
VMEM_LIMIT = 56 * 1024 * 1024
DMA_PRIORITIES = 2

PROJ_ROWS = 512
NSA_Q = 128
NSA_KC = 512
POST_ROWS = 256
SLOT_BLOCK = 256
PUSH_ROWS = 512
COMB_ROWS = 256


def _dot(a, b):
    return jnp.dot(a, b, preferred_element_type=F32)


def _dot_t(a, b):
    return lax.dot_general(a, b, (((1,), (1,)), ((), ())), preferred_element_type=F32)


def _split_bf16(x):
    hi = x.astype(BF16)
    lo = (x - hi.astype(F32)).astype(BF16)
    return hi, lo


def _cparams(sem):
    return pltpu.CompilerParams(dimension_semantics=sem, vmem_limit_bytes=VMEM_LIMIT)


def _proj_conv_kernel(x_ref, wc_ref, wq_ref, wkv_ref, wg_ref, cw_ref,
                      conv_ref, q_ref, kv_ref, gate_ref, carry_ref):
    rows = x_ref.shape[0]

    @pl.when(pl.program_id(1) == 0)
    def _():
        carry_ref[...] = jnp.zeros_like(carry_ref)

    xb = x_ref[...].astype(BF16)
    acc = _dot(xb, wc_ref[...])
    b_g = acc[:, :CONV_CH]
    u = acc[:, CONV_CH:2 * CONV_CH] * acc[:, 2 * CONV_CH:]
    prev2 = carry_ref[SUBLANES - 2:SUBLANES - 1, :]
    prev1 = carry_ref[SUBLANES - 1:SUBLANES, :]
    ri = lax.broadcasted_iota(I32, (rows, 1), 0)
    u1 = jnp.where(ri == 0, prev1, pltpu.roll(u, 1, 0))
    u2 = jnp.where(ri == 0, prev2, jnp.where(ri == 1, prev1, pltpu.roll(u, 2, 0)))
    y = cw_ref[0:1, :] * u2 + cw_ref[1:2, :] * u1 + cw_ref[2:3, :] * u
    conv_ref[...] = (b_g * y).astype(BF16)
    carry_ref[...] = u[rows - SUBLANES:, :]

    q_ref[...] = (_dot(xb, wq_ref[...]) * (HEAD_DIM ** -0.5)).astype(BF16)
    kv_ref[...] = _dot(xb, wkv_ref[...]).astype(BF16)
    gate_ref[...] = jax.nn.sigmoid(_dot(xb, wg_ref[...]))


def _proj_conv(x2, w_conv, w_q, w_kv, w_g, conv_w, batch, seq):
    T, D = x2.shape
    rows = min(PROJ_ROWS, seq)
    nt = seq // rows
    row_map = lambda b, i: (b * nt + i, 0)
    fixed = lambda b, i: (0, 0)
    return pl.pallas_call(
        _proj_conv_kernel,
        grid=(batch, nt),
        in_specs=[
            pl.BlockSpec((rows, D), row_map),
            pl.BlockSpec(w_conv.shape, fixed),
            pl.BlockSpec(w_q.shape, fixed),
            pl.BlockSpec(w_kv.shape, fixed),
            pl.BlockSpec(w_g.shape, fixed),
            pl.BlockSpec(conv_w.shape, fixed),
        ],
        out_specs=[
            pl.BlockSpec((rows, CONV_CH), row_map),
            pl.BlockSpec((rows, N_HEADS * HEAD_DIM), row_map),
            pl.BlockSpec((rows, 6 * KV_DIM), row_map),
            pl.BlockSpec((rows, LANES), row_map),
        ],
        out_shape=[
            jax.ShapeDtypeStruct((T, CONV_CH), BF16),
            jax.ShapeDtypeStruct((T, N_HEADS * HEAD_DIM), BF16),
            jax.ShapeDtypeStruct((T, 6 * KV_DIM), BF16),
            jax.ShapeDtypeStruct((T, LANES), F32),
        ],
        scratch_shapes=[pltpu.VMEM((SUBLANES, CONV_CH), F32)],
        compiler_params=_cparams(("arbitrary", "arbitrary")),
        name="proj_conv",
    )(x2, w_conv, w_q, w_kv, w_g, conv_w)


def _compress_kernel(ck_ref, cv_ref, wtk_ref, wbk_ref, w2k_ref, ptk_ref, pbk_ref, b1k_ref,
                     wtv_ref, wbv_ref, w2v_ref, ptv_ref, pbv_ref, b1v_ref, kc_ref, vc_ref):
    def one(c_ref, wt_ref, wb_ref, w2_ref, pt_ref, pb_ref, b1_ref, o_ref):
        c = c_ref[0]
        top = _dot(c, wt_ref[...])
        bot = _dot(c, wb_ref[...])
        c0 = _dot(pt_ref[...], wt_ref[...]) + _dot(pb_ref[...], wb_ref[...]) + b1_ref[...]
        n = top.shape[0]
        h = top + pltpu.roll(bot, n - 1, 0) + c0[0:1, :]
        g = jax.nn.gelu(h, approximate=True)
        o_ref[0] = _dot(g.astype(BF16), w2_ref[...]).astype(BF16)

    one(ck_ref, wtk_ref, wbk_ref, w2k_ref, ptk_ref, pbk_ref, b1k_ref, kc_ref)
    one(cv_ref, wtv_ref, wbv_ref, w2v_ref, ptv_ref, pbv_ref, b1v_ref, vc_ref)


def _blockdiag2(w):
    z = jnp.zeros_like(w)
    return jnp.concatenate([jnp.concatenate([w, z], 1), jnp.concatenate([z, w], 1)], 0)


def _compress_weights(pos, w1, b1, w2):
    w1r = w1.reshape(CMP_LEN, HEAD_DIM, HEAD_DIM)
    eye = jnp.eye(N_KV_HEADS, dtype=w1.dtype)
    wfull = (w1r[:, None, :, None, :] * eye[None, :, None, :, None]).reshape(CMP_LEN, KV_DIM, KV_DIM)
    w_top = wfull[:CMP_STRIDE].reshape(CMP_STRIDE * KV_DIM, KV_DIM).astype(BF16)
    w_bot = wfull[CMP_STRIDE:].reshape(CMP_STRIDE * KV_DIM, KV_DIM).astype(BF16)
    posr = jnp.tile(pos, (1, N_KV_HEADS))
    pos_top = jnp.tile(posr[:CMP_STRIDE].reshape(1, -1), (SUBLANES, 1)).astype(BF16)
    pos_bot = jnp.tile(posr[CMP_STRIDE:].reshape(1, -1), (SUBLANES, 1)).astype(BF16)
    b1r = jnp.tile(b1[None, :], (SUBLANES, N_KV_HEADS)).astype(F32)
    return w_top, w_bot, _blockdiag2(w2).astype(BF16), pos_top, pos_bot, b1r


def _compress(kc_raw, vc_raw, wk, wv, batch, seq):
    chunks = seq // CMP_STRIDE
    width = CMP_STRIDE * KV_DIM
    ck = kc_raw.reshape(batch, chunks, width)
    cv = vc_raw.reshape(batch, chunks, width)
    bmap = lambda b: (b, 0, 0)
    fixed = lambda b: (0, 0)
    wspecs = [pl.BlockSpec(w.shape, fixed) for w in wk]
    return pl.pallas_call(
        _compress_kernel,
        grid=(batch,),
        in_specs=[pl.BlockSpec((1, chunks, width), bmap), pl.BlockSpec((1, chunks, width), bmap)]
        + wspecs + wspecs,
        out_specs=[pl.BlockSpec((1, chunks, KV_DIM), bmap)] * 2,
        out_shape=[jax.ShapeDtypeStruct((batch, chunks, KV_DIM), BF16)] * 2,
        compiler_params=_cparams(("arbitrary",)),
        name="compress",
    )(ck, cv, *wk, *wv)


def _softmax_rows(s, valid):
    s = jnp.where(valid, s, NEG_INF)
    m = jnp.max(s, axis=-1, keepdims=True)
    p = jnp.where(valid, jnp.exp(s - m), 0.0)
    l = jnp.sum(p, axis=-1, keepdims=True)
    inv = jnp.where(l > 0.0, 1.0 / l, 0.0)
    return p, inv


def _nsa_kernel(q_ref, kc_ref, vc_ref, ks_ref, vs_ref, kw_ref, vw_ref, gate_ref, ovl_ref,
                o_ref, *, seq, n_sel):
    tq = q_ref.shape[0]
    ncp = kc_ref.shape[1]
    rows = Q_PER_KV * tq
    q0 = pl.program_id(1) * tq
    t_col = q0 + lax.broadcasted_iota(I32, (tq, 1), 0)
    t4 = jnp.concatenate([t_col] * Q_PER_KV, axis=0)
    row_i = lax.broadcasted_iota(I32, (rows, 1), 0)
    lane = lax.broadcasted_iota(I32, (1, LANES), 1)
    feat_lane = lax.broadcasted_iota(I32, (1, HEAD_DIM), 1)
    n_sel_pad = -(-n_sel // SUBLANES) * SUBLANES
    blk_row = lax.broadcasted_iota(I32, (n_sel_pad, tq), 0)
    blk_row_f = blk_row.astype(F32)
    gates = gate_ref[...]
    win_len = WINDOW + tq
    w_start = pl.multiple_of(jnp.maximum(q0 - WINDOW, 0), tq)
    n_chunks = (q0 + tq + NSA_KC - 1) // NSA_KC
    outs, stage = [], []
    for g in range(N_KV_HEADS):
        lo, hi = g * HEAD_DIM, (g + 1) * HEAD_DIM
        qg = jnp.concatenate(
            [q_ref[:, (g * Q_PER_KV + r) * HEAD_DIM:(g * Q_PER_KV + r + 1) * HEAD_DIM]
             for r in range(Q_PER_KV)], axis=0)
        slope = jnp.zeros((rows, 1), F32)
        for r in range(Q_PER_KV):
            h = g * Q_PER_KV + r
            in_head = (row_i >= r * tq) & (row_i < (r + 1) * tq)
            slope = jnp.where(in_head, 2.0 ** (-8.0 * (h + 1) / N_HEADS), slope)

        cmp_end = lax.broadcasted_iota(I32, (1, ncp), 1) * CMP_STRIDE + (CMP_LEN - 1)
        d_c = t4 - cmp_end
        s_c = _dot_t(qg, kc_ref[0, :, lo:hi]) - slope * d_c.astype(F32)
        p_c, inv_c = _softmax_rows(s_c, d_c >= 0)
        p_c = p_c * inv_c
        o_cmp = _dot(p_c.astype(BF16), vc_ref[0, :, lo:hi])

        ps = p_c[0:tq]
        for r in range(1, Q_PER_KV):
            ps = ps + p_c[r * tq:(r + 1) * tq]
        ps_hi, ps_lo = _split_bf16(ps)
        imp = _dot(ps_hi, ovl_ref[...]) + _dot(ps_lo, ovl_ref[...])
        forced = (lane == 0) | (lane == jnp.right_shift(t_col, SEL_SHIFT))
        causal = lane * SEL_BLOCK <= t_col
        score = jnp.where(forced, FORCED_SCORE, jnp.where(causal, imp, -1.0))
        score_t = score.T[:n_sel_pad, :]
        score_t = jnp.where(blk_row < n_sel, score_t, -jnp.inf)
        sel_t = jnp.zeros((n_sel_pad, tq), F32)
        for _ in range(min(SEL_TOPK, n_sel)):
            mx = jnp.max(score_t, axis=0, keepdims=True)
            first = jnp.min(jnp.where(score_t == mx, blk_row_f, float(LANES)), axis=0, keepdims=True)
            hit = blk_row_f == first
            sel_t = jnp.where(hit, 1.0, sel_t)
            score_t = jnp.where(hit, -jnp.inf, score_t)
        unsel_t = jnp.concatenate(
            [jnp.where(sel_t > 0.5, 0.0, NEG_INF), jnp.full((LANES - n_sel_pad, tq), NEG_INF, F32)], axis=0)
        unsel_b = unsel_t.T.astype(BF16)

        q_feat = jnp.where(feat_lane == 0, slope * float(SEL_BLOCK),
                           jnp.where(feat_lane == 1, slope, 0.0)).astype(BF16)
        q_aug = jnp.concatenate([qg, q_feat], axis=1)
        lhs = jnp.concatenate([q_aug, jnp.concatenate([unsel_b] * Q_PER_KV, axis=0)], axis=1)

        stage.append((q_aug, lhs, o_cmp))

    def key_feat(k0, n):
        pos = k0 + lax.broadcasted_iota(I32, (n, HEAD_DIM), 0)
        ln = lax.broadcasted_iota(I32, (n, HEAD_DIM), 1)
        return jnp.where(ln == 0, jnp.right_shift(pos, SEL_SHIFT),
                         jnp.where(ln == 1, jnp.bitwise_and(pos, SEL_BLOCK - 1), 0)
                         ).astype(F32).astype(BF16)

    def sel_chunk(c, carry, diagonal):
        k0 = pl.multiple_of(c * NSA_KC, NSA_KC)
        kblk = jnp.right_shift(k0 + lax.broadcasted_iota(I32, (NSA_KC, LANES), 0), SEL_SHIFT)
        onehot = jnp.where(kblk == lax.broadcasted_iota(I32, (NSA_KC, LANES), 1), 1.0, 0.0)
        shared = jnp.concatenate([key_feat(k0, NSA_KC), onehot.astype(BF16)], axis=1)
        new = []
        for g in range(N_KV_HEADS):
            m, l, acc = carry[g]
            kch = ks_ref[0, pl.ds(k0, NSA_KC), g * HEAD_DIM:(g + 1) * HEAD_DIM]
            vch = vs_ref[0, pl.ds(k0, NSA_KC), g * HEAD_DIM:(g + 1) * HEAD_DIM]
            s = _dot_t(stage[g][1], jnp.concatenate([kch, shared], axis=1))
            if diagonal:
                pos = k0 + lax.broadcasted_iota(I32, (1, NSA_KC), 1)
                s = jnp.where(pos <= t4, s, NEG_INF)
            m_new = jnp.maximum(m, jnp.max(s, axis=-1, keepdims=True))
            a = jnp.exp(m - m_new)
            p = jnp.exp(s - m_new)
            l = a * l + jnp.sum(p, axis=-1, keepdims=True)
            acc = a * acc + _dot(p.astype(BF16), vch)
            new.append((m_new, l, acc))
        return tuple(new)

    init = tuple((jnp.full((rows, 1), NEG_INF, F32), jnp.zeros((rows, 1), F32),
                  jnp.zeros((rows, HEAD_DIM), F32)) for _ in range(N_KV_HEADS))
    carry = lax.fori_loop(0, n_chunks - 1, functools.partial(sel_chunk, diagonal=False), init)
    final = sel_chunk(n_chunks - 1, carry, diagonal=True)

    d_w = t_col - (w_start + lax.broadcasted_iota(I32, (1, win_len), 1))
    band = jnp.where((d_w >= 0) & (d_w < WINDOW), 0.0, NEG_INF)
    band4 = jnp.concatenate([band] * Q_PER_KV, axis=0)
    win_feat = key_feat(w_start, win_len)
    for g in range(N_KV_HEADS):
        lo, hi = g * HEAD_DIM, (g + 1) * HEAD_DIM
        q_aug, _, o_cmp = stage[g]
        _, l_s, acc_s = final[g]
        o_slc = acc_s * jnp.where(l_s > 0.0, 1.0 / l_s, 0.0)

        kwb = kw_ref[0, pl.ds(w_start, win_len), lo:hi]
        vwb = vw_ref[0, pl.ds(w_start, win_len), lo:hi]
        s_w = _dot_t(q_aug, jnp.concatenate([kwb, win_feat], axis=1)) + band4
        m_w = jnp.max(s_w, axis=-1, keepdims=True)
        p_w = jnp.exp(s_w - m_w)
        o_win = _dot(p_w.astype(BF16), vwb) * (1.0 / jnp.sum(p_w, axis=-1, keepdims=True))

        for r in range(Q_PER_KV):
            h = g * Q_PER_KV + r
            sl = slice(r * tq, (r + 1) * tq)
            gc = gates[:, h * N_BRANCH:h * N_BRANCH + 1]
            gs = gates[:, h * N_BRANCH + 1:h * N_BRANCH + 2]
            gw = gates[:, h * N_BRANCH + 2:h * N_BRANCH + 3]
            outs.append(gc * o_cmp[sl] + gs * o_slc[sl] + gw * o_win[sl])
    o_ref[...] = jnp.concatenate(outs, axis=-1).astype(BF16)


def _nsa(q, kc, vc, kv3, gates, overlap, batch, seq):
    T = q.shape[0]
    tq = min(NSA_Q, seq)
    nq = seq // tq
    ncp = kc.shape[1]
    n_sel = seq // SEL_BLOCK
    row_map = lambda b, i: (b * nq + i, 0)
    bmap = lambda b, i: (b, 0, 0)
    kvspec = lambda j: pl.BlockSpec((1, seq, KV_DIM), lambda b, i, j=j: (b, 0, j))
    return pl.pallas_call(
        functools.partial(_nsa_kernel, seq=seq, n_sel=n_sel),
        grid=(batch, nq),
        in_specs=[
            pl.BlockSpec((tq, N_HEADS * HEAD_DIM), row_map),
            pl.BlockSpec((1, ncp, KV_DIM), bmap),
            pl.BlockSpec((1, ncp, KV_DIM), bmap),
            kvspec(2), kvspec(3), kvspec(4), kvspec(5),
            pl.BlockSpec((tq, LANES), row_map),
            pl.BlockSpec(overlap.shape, lambda b, i: (0, 0)),
        ],
        out_specs=pl.BlockSpec((tq, N_HEADS * HEAD_DIM), row_map),
        out_shape=jax.ShapeDtypeStruct((T, N_HEADS * HEAD_DIM), BF16),
        compiler_params=_cparams(("arbitrary", "arbitrary")),
        name="nsa",
    )(q, kc, vc, kv3, kv3, kv3, kv3, gates, overlap)


def _layer_norm(y, g, b):
    mu = jnp.mean(y, axis=-1, keepdims=True)
    yc = y - mu
    var = jnp.mean(yc * yc, axis=-1, keepdims=True)
    return yc * lax.rsqrt(var + LN_EPS) * g + b


def _post_kernel(x_ref, conv_ref, nsa_ref, wo_ref, g1_ref, b1_ref, rwh_ref, rwl_ref, rb_ref,
                 wsg_ref, wsu_ref, wsd_ref,
                 x3_ref, base_ref, e_ref, r_ref, w_ref, cnt_ref, carry_ref, *, alpha):
    rows, D = x_ref.shape

    @pl.when(pl.program_id(0) == 0)
    def _():
        carry_ref[...] = jnp.zeros_like(carry_ref)

    half = wo_ref.shape[0] // 2
    mix = _dot(conv_ref[...], wo_ref[:half, :]) + _dot(nsa_ref[...], wo_ref[half:, :])
    x1 = _layer_norm(alpha * x_ref[...] + mix, g1_ref[...], b1_ref[...])
    for s in range(D // LANES):
        x3_ref[pl.ds(s, rows, stride=SUBLANES), :] = x1[:, s * LANES:(s + 1) * LANES]

    xh, xl = _split_bf16(x1)
    hid = jax.nn.silu(_dot(xh, wsg_ref[...])) * _dot(xh, wsu_ref[...])
    base_ref[...] = alpha * x1 + _dot(hid.astype(BF16), wsd_ref[...])

    logits = (_dot_t(rwh_ref[...], xh) + _dot_t(rwh_ref[...], xl) + _dot_t(rwl_ref[...], xh))
    scores = jax.nn.sigmoid(logits)
    reps = rows // LANES
    biased = scores + jnp.concatenate([rb_ref[...]] * reps, axis=1)
    eidx = lax.broadcasted_iota(I32, (N_EXPERTS, rows), 0).astype(F32)
    gidx = lax.broadcasted_iota(I32, (GROUP_SIZE, rows), 0).astype(F32)
    gvals, gscore = [], []
    for gi in range(N_GROUPS):
        v = biased[gi * GROUP_SIZE:(gi + 1) * GROUP_SIZE, :]
        m1 = jnp.max(v, axis=0, keepdims=True)
        i1 = jnp.min(jnp.where(v == m1, gidx, float(GROUP_SIZE)), axis=0, keepdims=True)
        m2 = jnp.max(jnp.where(gidx == i1, -jnp.inf, v), axis=0, keepdims=True)
        gvals.append(v)
        gscore.append(m1 + m2)
    cands = []
    for gi in range(N_GROUPS):
        ahead = jnp.zeros((1, rows), F32)
        for gj in range(N_GROUPS):
            if gj == gi:
                continue
            beats = (gscore[gj] >= gscore[gi]) if gj < gi else (gscore[gj] > gscore[gi])
            ahead = ahead + jnp.where(beats, 1.0, 0.0)
        ahead_full = jnp.broadcast_to(ahead, gvals[gi].shape)
        cands.append(jnp.where(ahead_full < float(TOPK_GROUPS), gvals[gi], NEG_INF))
    cand = jnp.concatenate(cands, axis=0)
    onehot = jnp.zeros((N_EXPERTS, rows), F32)
    idx_rows, w_rows = [], []
    for _ in range(TOP_K):
        mx = jnp.max(cand, axis=0, keepdims=True)
        first = jnp.min(jnp.where(cand == mx, eidx, float(N_EXPERTS)), axis=0, keepdims=True)
        hit = eidx == first
        idx_rows.append(first)
        w_rows.append(jnp.sum(jnp.where(hit, scores, 0.0), axis=0, keepdims=True))
        onehot = jnp.where(hit, 1.0, onehot)
        cand = jnp.where(hit, -jnp.inf, cand)
    wsum = w_rows[0]
    for k in range(1, TOP_K):
        wsum = wsum + w_rows[k]

    ti = lax.broadcasted_iota(I32, (rows, rows), 0)
    tj = lax.broadcasted_iota(I32, (rows, rows), 1)
    earlier = jnp.where(ti < tj, 1.0, 0.0).astype(BF16)
    carry = carry_ref[...]
    before = _dot(onehot.astype(BF16), earlier) + jnp.concatenate([carry] * reps, axis=1)
    krow = lax.broadcasted_iota(I32, (TOP_K, rows), 0)
    e_out = jnp.zeros((TOP_K, rows), F32)
    r_out = jnp.zeros((TOP_K, rows), F32)
    w_out = jnp.zeros((TOP_K, rows), F32)
    for k in range(TOP_K):
        rank = jnp.sum(jnp.where(eidx == idx_rows[k], before, 0.0), axis=0, keepdims=True)
        e_out = jnp.where(krow == k, idx_rows[k], e_out)
        r_out = jnp.where(krow == k, rank, r_out)
        w_out = jnp.where(krow == k, w_rows[k] / wsum * ROUTED_SCALE, w_out)
    e_ref[...] = e_out.astype(I32)
    r_ref[...] = r_out.astype(I32)
    w_ref[...] = w_out
    total = carry + jnp.sum(onehot, axis=1, keepdims=True)
    carry_ref[...] = total
    cnt_ref[...] = total.astype(I32)


def _post(x2, conv_out, nsa_out, w_out, g1, b1, rw_hi, rw_lo, rbias, wsg, wsu, wsd, alpha):
    T, D = x2.shape
    rows = min(POST_ROWS, T)
    row_map = lambda i: (i, 0)
    col_map = lambda i: (0, i)
    fixed = lambda i: (0, 0)
    full = lambda a: pl.BlockSpec(a.shape, fixed)
    return pl.pallas_call(
        functools.partial(_post_kernel, alpha=alpha),
        grid=(T // rows,),
        in_specs=[
            pl.BlockSpec((rows, D), row_map),
            pl.BlockSpec((rows, CONV_CH), row_map),
            pl.BlockSpec((rows, N_HEADS * HEAD_DIM), row_map),
            full(w_out), full(g1), full(b1), full(rw_hi), full(rw_lo), full(rbias),
            full(wsg), full(wsu), full(wsd),
        ],
        out_specs=[
            pl.BlockSpec((rows * SUBLANES, LANES), row_map),
            pl.BlockSpec((rows, D), row_map),
            pl.BlockSpec((TOP_K, rows), col_map),
            pl.BlockSpec((TOP_K, rows), col_map),
            pl.BlockSpec((TOP_K, rows), col_map),
            pl.BlockSpec((N_EXPERTS, LANES), fixed),
        ],
        out_shape=[
            jax.ShapeDtypeStruct((T * SUBLANES, LANES), F32),
            jax.ShapeDtypeStruct((T, D), F32),
            jax.ShapeDtypeStruct((TOP_K, T), I32),
            jax.ShapeDtypeStruct((TOP_K, T), I32),
            jax.ShapeDtypeStruct((TOP_K, T), F32),
            jax.ShapeDtypeStruct((N_EXPERTS, LANES), I32),
        ],
        scratch_shapes=[pltpu.VMEM((N_EXPERTS, LANES), F32)],
        compiler_params=_cparams(("arbitrary",)),
        name="post_attn_router",
    )(x2, conv_out, nsa_out, w_out, g1, b1, rw_hi, rw_lo, rbias, wsg, wsu, wsd)


def _push_kernel(pstart_ref, zoff_ref, e_ref, r_ref, x3_ref, xs_ref, zero_ref, sem, zsem):
    toks = x3_ref.shape[0] // SUBLANES
    zrows = zero_ref.shape[0]

    def zero_copy(e):
        off = pl.multiple_of(zoff_ref[e] * SUBLANES, SUBLANES)
        return pltpu.make_async_copy(zero_ref, xs_ref.at[pl.ds(off, zrows), :], zsem)

    @pl.when(pl.program_id(0) == 0)
    def _():
        zero_ref[...] = jnp.zeros_like(zero_ref)

        def start(e, c):
            @pl.when(zoff_ref[e] >= 0)
            def _():
                zero_copy(e).start()
            return c

        def wait(e, c):
            @pl.when(zoff_ref[e] >= 0)
            def _():
                zero_copy(e).wait()
            return c

        lax.fori_loop(0, N_EXPERTS, start, 0)
        lax.fori_loop(0, N_EXPERTS, wait, 0)

    def push_token(t, c):
        src = x3_ref.at[pl.ds(pl.multiple_of(t * SUBLANES, SUBLANES), SUBLANES), :]
        for k in range(TOP_K):
            a = t * TOP_K + k
            slot = pstart_ref[e_ref[a]] + r_ref[a]
            dst = xs_ref.at[pl.ds(pl.multiple_of(slot * SUBLANES, SUBLANES), SUBLANES), :]
            pltpu.make_async_copy(src, dst, sem).start(priority=k % DMA_PRIORITIES)
        return c

    lax.fori_loop(0, toks, push_token, 0)
    for _ in range(TOP_K):
        pltpu.make_async_copy(x3_ref, xs_ref.at[pl.ds(0, toks * SUBLANES), :], sem).wait()


def _push(pad_start, zero_off, e_flat, r_flat, x3, n_slots):
    T = x3.shape[0] // SUBLANES
    toks = min(PUSH_ROWS, T)
    return pl.pallas_call(
        _push_kernel,
        grid_spec=pltpu.PrefetchScalarGridSpec(
            num_scalar_prefetch=2,
            grid=(T // toks,),
            in_specs=[
                pl.BlockSpec((toks * TOP_K,), lambda i, *_: (i,), memory_space=pltpu.SMEM),
                pl.BlockSpec((toks * TOP_K,), lambda i, *_: (i,), memory_space=pltpu.SMEM),
                pl.BlockSpec((toks * SUBLANES, LANES), lambda i, *_: (i, 0)),
            ],
            out_specs=pl.BlockSpec(memory_space=pl.ANY),
            scratch_shapes=[
                pltpu.VMEM((SLOT_BLOCK * SUBLANES, LANES), F32),
                pltpu.SemaphoreType.DMA(()),
                pltpu.SemaphoreType.DMA(()),
            ],
        ),
        out_shape=jax.ShapeDtypeStruct((n_slots * SUBLANES, LANES), F32),
        compiler_params=_cparams(("arbitrary",)),
        name="moe_push",
    )(pad_start, zero_off, e_flat, r_flat, x3)


def _expert_kernel(blk_e_ref, nused_ref, xs_ref, wg_ref, wu_ref, wd_ref, ys_ref, wgu_s, wd_s):
    b = pl.program_id(0)
    rows = xs_ref.shape[0] // SUBLANES
    D = wg_ref.shape[1]
    H = wg_ref.shape[2]
    prev = blk_e_ref[jnp.maximum(b - 1, 0)]

    @pl.when((b == 0) | (blk_e_ref[b] != prev))
    def _():
        wgu_s[:, :H] = wg_ref[0].astype(BF16)
        wgu_s[:, H:] = wu_ref[0].astype(BF16)
        wd_s[...] = wd_ref[0].astype(BF16)

    @pl.when(b < nused_ref[0])
    def _():
        xb = jnp.concatenate(
            [xs_ref[pl.ds(s, rows, stride=SUBLANES), :].astype(BF16) for s in range(D // LANES)],
            axis=-1)
        h = _dot(xb, wgu_s[...])
        act = (jax.nn.silu(h[:, :H]) * h[:, H:]).astype(BF16)
        out = _dot(act, wd_s[...])
        for s in range(D // LANES):
            ys_ref[pl.ds(s, rows, stride=SUBLANES), :] = out[:, s * LANES:(s + 1) * LANES]

    @pl.when(b >= nused_ref[0])
    def _():
        ys_ref[...] = jnp.zeros_like(ys_ref)


def _experts(blk_e, n_used, xs, w_gate, w_up, w_down):
    n_blocks = blk_e.shape[0]
    E, D, H = w_gate.shape
    rows = SLOT_BLOCK
    return pl.pallas_call(
        _expert_kernel,
        grid_spec=pltpu.PrefetchScalarGridSpec(
            num_scalar_prefetch=2,
            grid=(n_blocks,),
            in_specs=[
                pl.BlockSpec((rows * SUBLANES, LANES), lambda b, be, nu: (b, 0)),
                pl.BlockSpec((1, D, H), lambda b, be, nu: (be[b], 0, 0)),
                pl.BlockSpec((1, D, H), lambda b, be, nu: (be[b], 0, 0)),
                pl.BlockSpec((1, H, D), lambda b, be, nu: (be[b], 0, 0)),
            ],
            out_specs=pl.BlockSpec((rows * SUBLANES, LANES), lambda b, be, nu: (b, 0)),
            scratch_shapes=[pltpu.VMEM((D, 2 * H), BF16), pltpu.VMEM((H, D), BF16)],
        ),
        out_shape=jax.ShapeDtypeStruct(xs.shape, F32),
        compiler_params=_cparams(("arbitrary",)),
        name="moe_experts",
    )(blk_e, n_used, xs, w_gate, w_up, w_down)


def _combine_kernel(pstart_ref, e_ref, r_ref, en_ref, rn_ref, ys_ref, base_ref, rw_ref, g2_ref,
                    b2_ref, o_ref, buf0, buf1, sem0, sem1):
    toks, D = base_ref.shape
    i = pl.program_id(0)
    last = pl.num_programs(0) - 1

    def issue(eref, rref, buf, sem):
        def gather_token(t, c):
            for k in range(TOP_K):
                a = t * TOP_K + k
                slot = pstart_ref[eref[a]] + rref[a]
                src = ys_ref.at[pl.ds(pl.multiple_of(slot * SUBLANES, SUBLANES), SUBLANES), :]
                dst = buf.at[pl.ds(pl.multiple_of((k * toks + t) * SUBLANES, SUBLANES), SUBLANES), :]
                pltpu.make_async_copy(src, dst, sem).start(priority=k % DMA_PRIORITIES)
            return c

        lax.fori_loop(0, toks, gather_token, 0)

    def finish(buf, sem):
        pltpu.make_async_copy(ys_ref.at[pl.ds(0, buf.shape[0]), :], buf, sem).wait()
        w = rw_ref[...]
        pieces = []
        for s in range(D // LANES):
            acc = jnp.zeros((toks, LANES), F32)
            for k in range(TOP_K):
                rows = buf[pl.ds(k * toks * SUBLANES + s, toks, stride=SUBLANES), :]
                acc = acc + w[:, k:k + 1] * rows
            pieces.append(acc)
        y = base_ref[...] + jnp.concatenate(pieces, axis=-1)
        o_ref[...] = _layer_norm(y, g2_ref[...], b2_ref[...])

    @pl.when(i == 0)
    def _():
        issue(e_ref, r_ref, buf0, sem0)

    for parity, (cur, csem, nxt, nsem) in enumerate(((buf0, sem0, buf1, sem1), (buf1, sem1, buf0, sem0))):
        @pl.when(jnp.bitwise_and(i, 1) == parity)
        def _(cur=cur, csem=csem, nxt=nxt, nsem=nsem):
            @pl.when(i < last)
            def _():
                issue(en_ref, rn_ref, nxt, nsem)
            finish(cur, csem)


def _combine(pad_start, e_flat, r_flat, ys, base, rw, g2, b2):
    T, D = base.shape
    toks = min(COMB_ROWS, T)
    steps = T // toks
    idx_now = pl.BlockSpec((toks * TOP_K,), lambda i, *_: (i,), memory_space=pltpu.SMEM)
    idx_next = pl.BlockSpec((toks * TOP_K,), lambda i, *_: (jnp.minimum(i + 1, steps - 1),),
                            memory_space=pltpu.SMEM)
    return pl.pallas_call(
        _combine_kernel,
        grid_spec=pltpu.PrefetchScalarGridSpec(
            num_scalar_prefetch=1,
            grid=(steps,),
            in_specs=[
                idx_now, idx_now, idx_next, idx_next,
                pl.BlockSpec(memory_space=pl.ANY),
                pl.BlockSpec((toks, D), lambda i, *_: (i, 0)),
                pl.BlockSpec((toks, TOP_K), lambda i, *_: (i, 0)),
                pl.BlockSpec(g2.shape, lambda i, *_: (0, 0)),
                pl.BlockSpec(b2.shape, lambda i, *_: (0, 0)),
            ],
            out_specs=pl.BlockSpec((toks, D), lambda i, *_: (i, 0)),
            scratch_shapes=[
                pltpu.VMEM((TOP_K * toks * SUBLANES, LANES), F32),
                pltpu.VMEM((TOP_K * toks * SUBLANES, LANES), F32),
                pltpu.SemaphoreType.DMA(()),
                pltpu.SemaphoreType.DMA(()),
            ],
        ),
        out_shape=jax.ShapeDtypeStruct((T, D), F32),
        compiler_params=_cparams(("arbitrary",)),
        name="moe_combine",
    )(pad_start, e_flat, r_flat, e_flat, r_flat, ys, base, rw, g2, b2)


def _overlap_matrix(ncp):
    n = np.arange(ncp)[:, None]
    j = np.arange(LANES)[None, :]
    start = n * CMP_STRIDE
    end = start + CMP_LEN - 1
    sel_start = j * SEL_BLOCK
    ovl = (start < sel_start + SEL_BLOCK) & (end >= sel_start)
    return jnp.asarray(ovl.astype(np.float32), dtype=BF16)


def _mixer(x2, batch, seq, w_in, conv_w, cmp_k, cmp_v):
    c3 = 3 * CONV_CH
    qd = N_HEADS * HEAD_DIM
    w_conv = w_in[:, :c3].astype(BF16)
    w_q = w_in[:, c3:c3 + qd].astype(BF16)
    w_kv = w_in[:, c3 + qd:c3 + qd + 6 * KV_DIM].astype(BF16)
    w_g = jnp.pad(w_in[:, c3 + qd + 6 * KV_DIM:], ((0, 0), (0, LANES - N_HEADS * N_BRANCH))).astype(BF16)
    conv_out, q, kv, gates = _proj_conv(x2, w_conv, w_q, w_kv, w_g, conv_w, batch, seq)
    kc, vc = _compress(kv[:, :KV_DIM], kv[:, KV_DIM:2 * KV_DIM], cmp_k, cmp_v, batch, seq)
    ncp = -(-kc.shape[1] // LANES) * LANES
    if ncp != kc.shape[1]:
        padn = ((0, 0), (0, ncp - kc.shape[1]), (0, 0))
        kc, vc = jnp.pad(kc, padn), jnp.pad(vc, padn)
    kv3 = kv.reshape(batch, seq, 6 * KV_DIM)
    nsa_out = _nsa(q, kc, vc, kv3, gates, _overlap_matrix(ncp), batch, seq)
    return conv_out, nsa_out


def _moe(x3, base, e_t, r_t, w_t, counts, w_gate, w_up, w_down, g2, b2):
    T = base.shape[0]
    A = T * TOP_K
    n_blocks = -(-(A + N_EXPERTS * (SLOT_BLOCK - 1)) // SLOT_BLOCK)
    cnt = counts[:, 0]
    padded = (cnt + SLOT_BLOCK - 1) // SLOT_BLOCK * SLOT_BLOCK
    pad_end = jnp.cumsum(padded)
    pad_start = (pad_end - padded).astype(I32)
    zero_off = jnp.where(padded > 0, pad_end - SLOT_BLOCK, -1).astype(I32)
    n_used = (pad_end[-1:] // SLOT_BLOCK).astype(I32)
    blk_start = jnp.arange(n_blocks, dtype=I32) * SLOT_BLOCK
    blk_e = jnp.minimum(jnp.sum((pad_end[None, :] <= blk_start[:, None]).astype(I32), axis=1),
                        N_EXPERTS - 1).astype(I32)
    e_flat = e_t.T.reshape(A)
    r_flat = r_t.T.reshape(A)
    xs = _push(pad_start, zero_off, e_flat, r_flat, x3, n_blocks * SLOT_BLOCK)
    ys = _experts(blk_e, n_used, xs, w_gate, w_up, w_down)
    return _combine(pad_start, e_flat, r_flat, ys, base, w_t.T, g2, b2)


def kernel(x, w_in, conv_w, ck_pos, ck_w1, ck_b1, ck_w2, cv_pos, cv_w1, cv_b1, cv_w2, w_out, ln1_g, ln1_b, router_w, router_bias, w_gate, w_up, w_down, ws_gate, ws_up, ws_down, ln2_g, ln2_b):
    batch, seq, D = x.shape
    depth = w_in.shape[0]
    alpha = (2.0 * depth) ** 0.25
    x2 = x.reshape(batch * seq, D)
    for l in range(depth):
        cmp_k = _compress_weights(ck_pos[l], ck_w1[l], ck_b1[l], ck_w2[l])
        cmp_v = _compress_weights(cv_pos[l], cv_w1[l], cv_b1[l], cv_w2[l])
        conv_out, nsa_out = _mixer(x2, batch, seq, w_in[l], conv_w[l], cmp_k, cmp_v)
        rw_hi, rw_lo = _split_bf16(router_w[l].T)
        rbias = jnp.broadcast_to(router_bias[l][:, None], (N_EXPERTS, LANES))
        x3, base, e_t, r_t, w_t, counts = _post(
            x2, conv_out, nsa_out, w_out[l].astype(BF16), ln1_g[l][None, :], ln1_b[l][None, :],
            rw_hi, rw_lo, rbias,
            ws_gate[l].astype(BF16), ws_up[l].astype(BF16), ws_down[l].astype(BF16), alpha)
        x2 = _moe(x3, base, e_t, r_t, w_t, counts, w_gate[l], w_up[l], w_down[l],
                  ln2_g[l][None, :], ln2_b[l][None, :])
    return x2.reshape(batch, seq, D)
```

```python
import functools
import math

import jax
import jax.numpy as jnp
import numpy as np
from jax import lax
from jax.experimental import pallas as pl
from jax.experimental.pallas import tpu as pltpu

F32 = jnp.float32
BF16 = jnp.bfloat16
I32 = jnp.int32

CONV_CH = 512
CONV_WIDTH = 3
N_HEADS = 8
HEAD_DIM = 64
N_KV_HEADS = 2
Q_PER_KV = N_HEADS // N_KV_HEADS
KV_DIM = N_KV_HEADS * HEAD_DIM
N_BRANCH = 3
CMP_LEN = 32
CMP_STRIDE = 16
SEL_BLOCK = 64
SEL_TOPK = 8
WINDOW = 512
FORCED_SCORE = 1e4
N_EXPERTS = 256
TOP_K = 8
N_GROUPS = 8
TOPK_GROUPS = 4
GROUP_SIZE = N_EXPERTS // N_GROUPS
ROUTED_SCALE = 2.5
LN_EPS = 1e-5
NEG_INF = -1e30
SEL_SHIFT = SEL_BLOCK.bit_length() - 1
GROUP_SHIFT = GROUP_SIZE.bit_length() - 1
TOPK_SHIFT = TOP_K.bit_length() - 1

LANES = 128
SUBLANES = 8
VMEM_LIMIT = 56 * 1024 * 1024
DMA_PRIORITIES = 2

PROJ_ROWS = 512
NSA_Q = 128
NSA_KC = 512
POST_ROWS = 256
SLOT_BLOCK = 256
PUSH_ROWS = 512
COMB_ROWS = 256


def _dot(a, b):
    return jnp.dot(a, b, preferred_element_type=F32)


def _dot_t(a, b):
    return lax.dot_general(a, b, (((1,), (1,)), ((), ())), preferred_element_type=F32)


def _split_bf16(x):
    hi = x.astype(BF16)
    lo = (x - hi.astype(F32)).astype(BF16)
    return hi, lo


def _cparams(sem):
    return pltpu.CompilerParams(dimension_semantics=sem, vmem_limit_bytes=VMEM_LIMIT)


def _proj_conv_kernel(x_ref, wc_ref, wq_ref, wkv_ref, wg_ref, cw_ref,
                      conv_ref, q_ref, kv_ref, gate_ref, carry_ref):
    rows = x_ref.shape[0]

    @pl.when(pl.program_id(1) == 0)
    def _():
        carry_ref[...] = jnp.zeros_like(carry_ref)

    xb = x_ref[...].astype(BF16)
    acc = _dot(xb, wc_ref[...])
    b_g = acc[:, :CONV_CH]
    u = acc[:, CONV_CH:2 * CONV_CH] * acc[:, 2 * CONV_CH:]
    prev2 = carry_ref[SUBLANES - 2:SUBLANES - 1, :]
    prev1 = carry_ref[SUBLANES - 1:SUBLANES, :]
    ri = lax.broadcasted_iota(I32, (rows, 1), 0)
    u1 = jnp.where(ri == 0, prev1, pltpu.roll(u, 1, 0))
    u2 = jnp.where(ri == 0, prev2, jnp.where(ri == 1, prev1, pltpu.roll(u, 2, 0)))
    y = cw_ref[0:1, :] * u2 + cw_ref[1:2, :] * u1 + cw_ref[2:3, :] * u
    conv_ref[...] = (b_g * y).astype(BF16)
    carry_ref[...] = u[rows - SUBLANES:, :]

    q_ref[...] = (_dot(xb, wq_ref[...]) * (HEAD_DIM ** -0.5)).astype(BF16)
    kv_ref[...] = _dot(xb, wkv_ref[...]).astype(BF16)
    gate_ref[...] = jax.nn.sigmoid(_dot(xb, wg_ref[...]))


def _proj_conv(x2, w_conv, w_q, w_kv, w_g, conv_w, batch, seq):
    T, D = x2.shape
    rows = min(PROJ_ROWS, seq)
    nt = seq // rows
    row_map = lambda b, i: (b * nt + i, 0)
    fixed = lambda b, i: (0, 0)
    return pl.pallas_call(
        _proj_conv_kernel,
        grid=(batch, nt),
        in_specs=[
            pl.BlockSpec((rows, D), row_map),
            pl.BlockSpec(w_conv.shape, fixed),
            pl.BlockSpec(w_q.shape, fixed),
            pl.BlockSpec(w_kv.shape, fixed),
            pl.BlockSpec(w_g.shape, fixed),
            pl.BlockSpec(conv_w.shape, fixed),
        ],
        out_specs=[
            pl.BlockSpec((rows, CONV_CH), row_map),
            pl.BlockSpec((rows, N_HEADS * HEAD_DIM), row_map),
            pl.BlockSpec((rows, 6 * KV_DIM), row_map),
            pl.BlockSpec((rows, LANES), row_map),
        ],
        out_shape=[
            jax.ShapeDtypeStruct((T, CONV_CH), BF16),
            jax.ShapeDtypeStruct((T, N_HEADS * HEAD_DIM), BF16),
            jax.ShapeDtypeStruct((T, 6 * KV_DIM), BF16),
            jax.ShapeDtypeStruct((T, LANES), F32),
        ],
        scratch_shapes=[pltpu.VMEM((SUBLANES, CONV_CH), F32)],
        compiler_params=_cparams(("arbitrary", "arbitrary")),
        name="proj_conv",
    )(x2, w_conv, w_q, w_kv, w_g, conv_w)


def _compress_kernel(ck_ref, cv_ref, wtk_ref, wbk_ref, w2k_ref, ptk_ref, pbk_ref, b1k_ref,
                     wtv_ref, wbv_ref, w2v_ref, ptv_ref, pbv_ref, b1v_ref, kc_ref, vc_ref):
    def one(c_ref, wt_ref, wb_ref, w2_ref, pt_ref, pb_ref, b1_ref, o_ref):
        c = c_ref[0]
        top = _dot(c, wt_ref[...])
        bot = _dot(c, wb_ref[...])
        c0 = _dot(pt_ref[...], wt_ref[...]) + _dot(pb_ref[...], wb_ref[...]) + b1_ref[...]
        n = top.shape[0]
        h = top + pltpu.roll(bot, n - 1, 0) + c0[0:1, :]
        g = jax.nn.gelu(h, approximate=True)
        o_ref[0] = _dot(g.astype(BF16), w2_ref[...]).astype(BF16)

    one(ck_ref, wtk_ref, wbk_ref, w2k_ref, ptk_ref, pbk_ref, b1k_ref, kc_ref)
    one(cv_ref, wtv_ref, wbv_ref, w2v_ref, ptv_ref, pbv_ref, b1v_ref, vc_ref)


def _blockdiag2(w):
    z = jnp.zeros_like(w)
    return jnp.concatenate([jnp.concatenate([w, z], 1), jnp.concatenate([z, w], 1)], 0)


def _compress_weights(pos, w1, b1, w2):
    w1r = w1.reshape(CMP_LEN, HEAD_DIM, HEAD_DIM)
    eye = jnp.eye(N_KV_HEADS, dtype=w1.dtype)
    wfull = (w1r[:, None, :, None, :] * eye[None, :, None, :, None]).reshape(CMP_LEN, KV_DIM, KV_DIM)
    w_top = wfull[:CMP_STRIDE].reshape(CMP_STRIDE * KV_DIM, KV_DIM).astype(BF16)
    w_bot = wfull[CMP_STRIDE:].reshape(CMP_STRIDE * KV_DIM, KV_DIM).astype(BF16)
    posr = jnp.tile(pos, (1, N_KV_HEADS))
    pos_top = jnp.tile(posr[:CMP_STRIDE].reshape(1, -1), (SUBLANES, 1)).astype(BF16)
    pos_bot = jnp.tile(posr[CMP_STRIDE:].reshape(1, -1), (SUBLANES, 1)).astype(BF16)
    b1r = jnp.tile(b1[None, :], (SUBLANES, N_KV_HEADS)).astype(F32)
    return w_top, w_bot, _blockdiag2(w2).astype(BF16), pos_top, pos_bot, b1r


def _compress(kc_raw, vc_raw, wk, wv, batch, seq):
    chunks = seq // CMP_STRIDE
    width = CMP_STRIDE * KV_DIM
    ck = kc_raw.reshape(batch, chunks, width)
    cv = vc_raw.reshape(batch, chunks, width)
    bmap = lambda b: (b, 0, 0)
    fixed = lambda b: (0, 0)
    wspecs = [pl.BlockSpec(w.shape, fixed) for w in wk]
    return pl.pallas_call(
        _compress_kernel,
        grid=(batch,),
        in_specs=[pl.BlockSpec((1, chunks, width), bmap), pl.BlockSpec((1, chunks, width), bmap)]
        + wspecs + wspecs,
        out_specs=[pl.BlockSpec((1, chunks, KV_DIM), bmap)] * 2,
        out_shape=[jax.ShapeDtypeStruct((batch, chunks, KV_DIM), BF16)] * 2,
        compiler_params=_cparams(("arbitrary",)),
        name="compress",
    )(ck, cv, *wk, *wv)


def _softmax_rows(s, valid):
    s = jnp.where(valid, s, NEG_INF)
    m = jnp.max(s, axis=-1, keepdims=True)
    p = jnp.where(valid, jnp.exp(s - m), 0.0)
    l = jnp.sum(p, axis=-1, keepdims=True)
    inv = jnp.where(l > 0.0, 1.0 / l, 0.0)
    return p, inv


def _nsa_kernel(q_ref, kc_ref, vc_ref, ks_ref, vs_ref, kw_ref, vw_ref, gate_ref, ovl_ref,
                o_ref, *, seq, n_sel):
    tq = q_ref.shape[0]
    ncp = kc_ref.shape[1]
    rows = Q_PER_KV * tq
    q0 = pl.program_id(1) * tq
    t_col = q0 + lax.broadcasted_iota(I32, (tq, 1), 0)
    t4 = jnp.concatenate([t_col] * Q_PER_KV, axis=0)
    row_i = lax.broadcasted_iota(I32, (rows, 1), 0)
    lane = lax.broadcasted_iota(I32, (1, LANES), 1)
    feat_lane = lax.broadcasted_iota(I32, (1, HEAD_DIM), 1)
    n_sel_pad = -(-n_sel // SUBLANES) * SUBLANES
    blk_row = lax.broadcasted_iota(I32, (n_sel_pad, tq), 0)
    blk_row_f = blk_row.astype(F32)
    gates = gate_ref[...]
    win_len = WINDOW + tq
    w_start = pl.multiple_of(jnp.maximum(q0 - WINDOW, 0), tq)
    n_chunks = (q0 + tq + NSA_KC - 1) // NSA_KC
    outs, stage = [], []
    for g in range(N_KV_HEADS):
        lo, hi = g * HEAD_DIM, (g + 1) * HEAD_DIM
        qg = jnp.concatenate(
            [q_ref[:, (g * Q_PER_KV + r) * HEAD_DIM:(g * Q_PER_KV + r + 1) * HEAD_DIM]
             for r in range(Q_PER_KV)], axis=0)
        slope = jnp.zeros((rows, 1), F32)
        for r in range(Q_PER_KV):
            h = g * Q_PER_KV + r
            in_head = (row_i >= r * tq) & (row_i < (r + 1) * tq)
            slope = jnp.where(in_head, 2.0 ** (-8.0 * (h + 1) / N_HEADS), slope)

        cmp_end = lax.broadcasted_iota(I32, (1, ncp), 1) * CMP_STRIDE + (CMP_LEN - 1)
        d_c = t4 - cmp_end
        s_c = _dot_t(qg, kc_ref[0, :, lo:hi]) - slope * d_c.astype(F32)
        p_c, inv_c = _softmax_rows(s_c, d_c >= 0)
        p_c = p_c * inv_c
        o_cmp = _dot(p_c.astype(BF16), vc_ref[0, :, lo:hi])

        ps = p_c[0:tq]
        for r in range(1, Q_PER_KV):
            ps = ps + p_c[r * tq:(r + 1) * tq]
        ps_hi, ps_lo = _split_bf16(ps)
        imp = _dot(ps_hi, ovl_ref[...]) + _dot(ps_lo, ovl_ref[...])
        forced = (lane == 0) | (lane == jnp.right_shift(t_col, SEL_SHIFT))
        causal = lane * SEL_BLOCK <= t_col
        score = jnp.where(forced, FORCED_SCORE, jnp.where(causal, imp, -1.0))
        score_t = score.T[:n_sel_pad, :]
        score_t = jnp.where(blk_row < n_sel, score_t, -jnp.inf)
        sel_t = jnp.zeros((n_sel_pad, tq), F32)
        for _ in range(min(SEL_TOPK, n_sel)):
            mx = jnp.max(score_t, axis=0, keepdims=True)
            first = jnp.min(jnp.where(score_t == mx, blk_row_f, float(LANES)), axis=0, keepdims=True)
            hit = blk_row_f == first
            sel_t = jnp.where(hit, 1.0, sel_t)
            score_t = jnp.where(hit, -jnp.inf, score_t)
        unsel_t = jnp.concatenate(
            [jnp.where(sel_t > 0.5, 0.0, NEG_INF), jnp.full((LANES - n_sel_pad, tq), NEG_INF, F32)], axis=0)
        unsel_b = unsel_t.T.astype(BF16)

        q_feat = jnp.where(feat_lane == 0, slope * float(SEL_BLOCK),
                           jnp.where(feat_lane == 1, slope, 0.0)).astype(BF16)
        q_aug = jnp.concatenate([qg, q_feat], axis=1)
        lhs = jnp.concatenate([q_aug, jnp.concatenate([unsel_b] * Q_PER_KV, axis=0)], axis=1)

        stage.append((q_aug, lhs, o_cmp))

    def key_feat(k0, n):
        pos = k0 + lax.broadcasted_iota(I32, (n, HEAD_DIM), 0)
        ln = lax.broadcasted_iota(I32, (n, HEAD_DIM), 1)
        return jnp.where(ln == 0, jnp.right_shift(pos, SEL_SHIFT),
                         jnp.where(ln == 1, jnp.bitwise_and(pos, SEL_BLOCK - 1), 0)
                         ).astype(F32).astype(BF16)

    def sel_chunk(c, carry, diagonal):
        k0 = pl.multiple_of(c * NSA_KC, NSA_KC)
        kblk = jnp.right_shift(k0 + lax.broadcasted_iota(I32, (NSA_KC, LANES), 0), SEL_SHIFT)
        onehot = jnp.where(kblk == lax.broadcasted_iota(I32, (NSA_KC, LANES), 1), 1.0, 0.0)
        shared = jnp.concatenate([key_feat(k0, NSA_KC), onehot.astype(BF16)], axis=1)
        new = []
        for g in range(N_KV_HEADS):
            m, l, acc = carry[g]
            kch = ks_ref[0, pl.ds(k0, NSA_KC), g * HEAD_DIM:(g + 1) * HEAD_DIM]
            vch = vs_ref[0, pl.ds(k0, NSA_KC), g * HEAD_DIM:(g + 1) * HEAD_DIM]
            s = _dot_t(stage[g][1], jnp.concatenate([kch, shared], axis=1))
            if diagonal:
                pos = k0 + lax.broadcasted_iota(I32, (1, NSA_KC), 1)
                s = jnp.where(pos <= t4, s, NEG_INF)
            m_new = jnp.maximum(m, jnp.max(s, axis=-1, keepdims=True))
            a = jnp.exp(m - m_new)
            p = jnp.exp(s - m_new)
            l = a * l + jnp.sum(p, axis=-1, keepdims=True)
            acc = a * acc + _dot(p.astype(BF16), vch)
            new.append((m_new, l, acc))
        return tuple(new)

    init = tuple((jnp.full((rows, 1), NEG_INF, F32), jnp.zeros((rows, 1), F32),
                  jnp.zeros((rows, HEAD_DIM), F32)) for _ in range(N_KV_HEADS))
    carry = lax.fori_loop(0, n_chunks - 1, functools.partial(sel_chunk, diagonal=False), init)
    final = sel_chunk(n_chunks - 1, carry, diagonal=True)

    d_w = t_col - (w_start + lax.broadcasted_iota(I32, (1, win_len), 1))
    band = jnp.where((d_w >= 0) & (d_w < WINDOW), 0.0, NEG_INF)
    band4 = jnp.concatenate([band] * Q_PER_KV, axis=0)
    win_feat = key_feat(w_start, win_len)
    for g in range(N_KV_HEADS):
        lo, hi = g * HEAD_DIM, (g + 1) * HEAD_DIM
        q_aug, _, o_cmp = stage[g]
        _, l_s, acc_s = final[g]
        o_slc = acc_s * jnp.where(l_s > 0.0, 1.0 / l_s, 0.0)

        kwb = kw_ref[0, pl.ds(w_start, win_len), lo:hi]
        vwb = vw_ref[0, pl.ds(w_start, win_len), lo:hi]
        s_w = _dot_t(q_aug, jnp.concatenate([kwb, win_feat], axis=1)) + band4
        m_w = jnp.max(s_w, axis=-1, keepdims=True)
        p_w = jnp.exp(s_w - m_w)
        o_win = _dot(p_w.astype(BF16), vwb) * (1.0 / jnp.sum(p_w, axis=-1, keepdims=True))

        for r in range(Q_PER_KV):
            h = g * Q_PER_KV + r
            sl = slice(r * tq, (r + 1) * tq)
            gc = gates[:, h * N_BRANCH:h * N_BRANCH + 1]
            gs = gates[:, h * N_BRANCH + 1:h * N_BRANCH + 2]
            gw = gates[:, h * N_BRANCH + 2:h * N_BRANCH + 3]
            outs.append(gc * o_cmp[sl] + gs * o_slc[sl] + gw * o_win[sl])
    o_ref[...] = jnp.concatenate(outs, axis=-1).astype(BF16)


def _nsa(q, kc, vc, kv3, gates, overlap, batch, seq):
    T = q.shape[0]
    tq = min(NSA_Q, seq)
    nq = seq // tq
    ncp = kc.shape[1]
    n_sel = seq // SEL_BLOCK
    row_map = lambda b, i: (b * nq + i, 0)
    bmap = lambda b, i: (b, 0, 0)
    kvspec = lambda j: pl.BlockSpec((1, seq, KV_DIM), lambda b, i, j=j: (b, 0, j))
    return pl.pallas_call(
        functools.partial(_nsa_kernel, seq=seq, n_sel=n_sel),
        grid=(batch, nq),
        in_specs=[
            pl.BlockSpec((tq, N_HEADS * HEAD_DIM), row_map),
            pl.BlockSpec((1, ncp, KV_DIM), bmap),
            pl.BlockSpec((1, ncp, KV_DIM), bmap),
            kvspec(2), kvspec(3), kvspec(4), kvspec(5),
            pl.BlockSpec((tq, LANES), row_map),
            pl.BlockSpec(overlap.shape, lambda b, i: (0, 0)),
        ],
        out_specs=pl.BlockSpec((tq, N_HEADS * HEAD_DIM), row_map),
        out_shape=jax.ShapeDtypeStruct((T, N_HEADS * HEAD_DIM), BF16),
        compiler_params=_cparams(("arbitrary", "arbitrary")),
        name="nsa",
    )(q, kc, vc, kv3, kv3, kv3, kv3, gates, overlap)


def _layer_norm(y, g, b):
    mu = jnp.mean(y, axis=-1, keepdims=True)
    yc = y - mu
    var = jnp.mean(yc * yc, axis=-1, keepdims=True)
    return yc * lax.rsqrt(var + LN_EPS) * g + b


def _post_kernel(x_ref, conv_ref, nsa_ref, wo_ref, g1_ref, b1_ref, rwh_ref, rwl_ref, rb_ref,
                 wsg_ref, wsu_ref, wsd_ref,
                 x3_ref, base_ref, e_ref, r_ref, w_ref, cnt_ref, carry_ref, *, alpha):
    rows, D = x_ref.shape

    @pl.when(pl.program_id(0) == 0)
    def _():
        carry_ref[...] = jnp.zeros_like(carry_ref)

    half = wo_ref.shape[0] // 2
    mix = _dot(conv_ref[...], wo_ref[:half, :]) + _dot(nsa_ref[...], wo_ref[half:, :])
    x1 = _layer_norm(alpha * x_ref[...] + mix, g1_ref[...], b1_ref[...])
    for s in range(D // LANES):
        x3_ref[pl.ds(s, rows, stride=SUBLANES), :] = x1[:, s * LANES:(s + 1) * LANES]

    xh, xl = _split_bf16(x1)
    hid = jax.nn.silu(_dot(xh, wsg_ref[...])) * _dot(xh, wsu_ref[...])
    base_ref[...] = alpha * x1 + _dot(hid.astype(BF16), wsd_ref[...])

    logits = (_dot_t(rwh_ref[...], xh) + _dot_t(rwh_ref[...], xl) + _dot_t(rwl_ref[...], xh))
    scores = jax.nn.sigmoid(logits)
    reps = rows // LANES
    biased = scores + jnp.concatenate([rb_ref[...]] * reps, axis=1)
    eidx = lax.broadcasted_iota(I32, (N_EXPERTS, rows), 0).astype(F32)
    gidx = lax.broadcasted_iota(I32, (GROUP_SIZE, rows), 0).astype(F32)
    gvals, gscore = [], []
    for gi in range(N_GROUPS):
        v = biased[gi * GROUP_SIZE:(gi + 1) * GROUP_SIZE, :]
        m1 = jnp.max(v, axis=0, keepdims=True)
        i1 = jnp.min(jnp.where(v == m1, gidx, float(GROUP_SIZE)), axis=0, keepdims=True)
        m2 = jnp.max(jnp.where(gidx == i1, -jnp.inf, v), axis=0, keepdims=True)
        gvals.append(v)
        gscore.append(m1 + m2)
    cands = []
    for gi in range(N_GROUPS):
        ahead = jnp.zeros((1, rows), F32)
        for gj in range(N_GROUPS):
            if gj == gi:
                continue
            beats = (gscore[gj] >= gscore[gi]) if gj < gi else (gscore[gj] > gscore[gi])
            ahead = ahead + jnp.where(beats, 1.0, 0.0)
        ahead_full = jnp.broadcast_to(ahead, gvals[gi].shape)
        cands.append(jnp.where(ahead_full < float(TOPK_GROUPS), gvals[gi], NEG_INF))
    cand = jnp.concatenate(cands, axis=0)
    onehot = jnp.zeros((N_EXPERTS, rows), F32)
    idx_rows, w_rows = [], []
    for _ in range(TOP_K):
        mx = jnp.max(cand, axis=0, keepdims=True)
        first = jnp.min(jnp.where(cand == mx, eidx, float(N_EXPERTS)), axis=0, keepdims=True)
        hit = eidx == first
        idx_rows.append(first)
        w_rows.append(jnp.sum(jnp.where(hit, scores, 0.0), axis=0, keepdims=True))
        onehot = jnp.where(hit, 1.0, onehot)
        cand = jnp.where(hit, -jnp.inf, cand)
    wsum = w_rows[0]
    for k in range(1, TOP_K):
        wsum = wsum + w_rows[k]

    ti = lax.broadcasted_iota(I32, (rows, rows), 0)
    tj = lax.broadcasted_iota(I32, (rows, rows), 1)
    earlier = jnp.where(ti < tj, 1.0, 0.0).astype(BF16)
    carry = carry_ref[...]
    before = _dot(onehot.astype(BF16), earlier) + jnp.concatenate([carry] * reps, axis=1)
    krow = lax.broadcasted_iota(I32, (TOP_K, rows), 0)
    e_out = jnp.zeros((TOP_K, rows), F32)
    r_out = jnp.zeros((TOP_K, rows), F32)
    w_out = jnp.zeros((TOP_K, rows), F32)
    for k in range(TOP_K):
        rank = jnp.sum(jnp.where(eidx == idx_rows[k], before, 0.0), axis=0, keepdims=True)
        e_out = jnp.where(krow == k, idx_rows[k], e_out)
        r_out = jnp.where(krow == k, rank, r_out)
        w_out = jnp.where(krow == k, w_rows[k] / wsum * ROUTED_SCALE, w_out)
    e_ref[...] = e_out.astype(I32)
    r_ref[...] = r_out.astype(I32)
    w_ref[...] = w_out
    total = carry + jnp.sum(onehot, axis=1, keepdims=True)
    carry_ref[...] = total
    cnt_ref[...] = total.astype(I32)


def _post(x2, conv_out, nsa_out, w_out, g1, b1, rw_hi, rw_lo, rbias, wsg, wsu, wsd, alpha):
    T, D = x2.shape
    rows = min(POST_ROWS, T)
    row_map = lambda i: (i, 0)
    col_map = lambda i: (0, i)
    fixed = lambda i: (0, 0)
    full = lambda a: pl.BlockSpec(a.shape, fixed)
    return pl.pallas_call(
        functools.partial(_post_kernel, alpha=alpha),
        grid=(T // rows,),
        in_specs=[
            pl.BlockSpec((rows, D), row_map),
            pl.BlockSpec((rows, CONV_CH), row_map),
            pl.BlockSpec((rows, N_HEADS * HEAD_DIM), row_map),
            full(w_out), full(g1), full(b1), full(rw_hi), full(rw_lo), full(rbias),
            full(wsg), full(wsu), full(wsd),
        ],
        out_specs=[
            pl.BlockSpec((rows * SUBLANES, LANES), row_map),
            pl.BlockSpec((rows, D), row_map),
            pl.BlockSpec((TOP_K, rows), col_map),
            pl.BlockSpec((TOP_K, rows), col_map),
            pl.BlockSpec((TOP_K, rows), col_map),
            pl.BlockSpec((N_EXPERTS, LANES), fixed),
        ],
        out_shape=[
            jax.ShapeDtypeStruct((T * SUBLANES, LANES), F32),
            jax.ShapeDtypeStruct((T, D), F32),
            jax.ShapeDtypeStruct((TOP_K, T), I32),
            jax.ShapeDtypeStruct((TOP_K, T), I32),
            jax.ShapeDtypeStruct((TOP_K, T), F32),
            jax.ShapeDtypeStruct((N_EXPERTS, LANES), I32),
        ],
        scratch_shapes=[pltpu.VMEM((N_EXPERTS, LANES), F32)],
        compiler_params=_cparams(("arbitrary",)),
        name="post_attn_router",
    )(x2, conv_out, nsa_out, w_out, g1, b1, rw_hi, rw_lo, rbias, wsg, wsu, wsd)


def _slot_rows_kernel(pstart_ref, e_ref, r_ref, o_ref):
    e = e_ref[...]

    def add_expert(j, acc):
        return acc + jnp.where(e == j, pstart_ref[j], 0)

    o_ref[...] = lax.fori_loop(0, N_EXPERTS, add_expert, r_ref[...]) * SUBLANES


def _slot_rows(pad_start, e_t, r_t):
    T = e_t.shape[1]
    cols = min(2048, T)
    spec = pl.BlockSpec((TOP_K, cols), lambda i, *_: (0, i))
    return pl.pallas_call(
        _slot_rows_kernel,
        grid_spec=pltpu.PrefetchScalarGridSpec(
            num_scalar_prefetch=1, grid=(T // cols,), in_specs=[spec, spec], out_specs=spec),
        out_shape=jax.ShapeDtypeStruct(e_t.shape, I32),
        compiler_params=_cparams(("arbitrary",)),
        name="moe_slot_rows",
    )(pad_start, e_t, r_t)


def _push_kernel(zoff_ref, d_ref, x3_ref, xs_ref, zero_ref, sem, zsem):
    toks = x3_ref.shape[0] // SUBLANES
    zrows = zero_ref.shape[0]

    def zero_copy(e):
        off = pl.multiple_of(zoff_ref[e] * SUBLANES, SUBLANES)
        return pltpu.make_async_copy(zero_ref, xs_ref.at[pl.ds(off, zrows), :], zsem)

    @pl.when(pl.program_id(0) == 0)
    def _():
        zero_ref[...] = jnp.zeros_like(zero_ref)

        def start(e, c):
            @pl.when(zoff_ref[e] >= 0)
            def _():
                zero_copy(e).start()
            return c

        def wait(e, c):
            @pl.when(zoff_ref[e] >= 0)
            def _():
                zero_copy(e).wait()
            return c

        lax.fori_loop(0, N_EXPERTS, start, 0)
        lax.fori_loop(0, N_EXPERTS, wait, 0)

    def push_token(t, c):
        src = x3_ref.at[pl.ds(pl.multiple_of(t * SUBLANES, SUBLANES), SUBLANES), :]
        for k in range(TOP_K):
            row = pl.multiple_of(d_ref[t * TOP_K + k], SUBLANES)
            dst = xs_ref.at[pl.ds(row, SUBLANES), :]
            pltpu.make_async_copy(src, dst, sem).start(priority=k % DMA_PRIORITIES)
        return c

    lax.fori_loop(0, toks, push_token, 0)
    for _ in range(TOP_K):
        pltpu.make_async_copy(x3_ref, xs_ref.at[pl.ds(0, toks * SUBLANES), :], sem).wait()


def _push(zero_off, d_flat, x3, n_slots):
    T = x3.shape[0] // SUBLANES
    toks = min(PUSH_ROWS, T)
    return pl.pallas_call(
        _push_kernel,
        grid_spec=pltpu.PrefetchScalarGridSpec(
            num_scalar_prefetch=1,
            grid=(T // toks,),
            in_specs=[
                pl.BlockSpec((toks * TOP_K,), lambda i, *_: (i,), memory_space=pltpu.SMEM),
                pl.BlockSpec((toks * SUBLANES, LANES), lambda i, *_: (i, 0)),
            ],
            out_specs=pl.BlockSpec(memory_space=pl.ANY),
            scratch_shapes=[
                pltpu.VMEM((SLOT_BLOCK * SUBLANES, LANES), F32),
                pltpu.SemaphoreType.DMA(()),
                pltpu.SemaphoreType.DMA(()),
            ],
        ),
        out_shape=jax.ShapeDtypeStruct((n_slots * SUBLANES, LANES), F32),
        compiler_params=_cparams(("arbitrary",)),
        name="moe_push",
    )(zero_off, d_flat, x3)


def _expert_kernel(blk_e_ref, nused_ref, xs_ref, wg_ref, wu_ref, wd_ref, ys_ref, wgu_s, wd_s):
    b = pl.program_id(0)
    rows = xs_ref.shape[0] // SUBLANES
    D = wg_ref.shape[1]
    H = wg_ref.shape[2]
    prev = blk_e_ref[jnp.maximum(b - 1, 0)]

    @pl.when((b == 0) | (blk_e_ref[b] != prev))
    def _():
        wgu_s[:, :H] = wg_ref[0].astype(BF16)
        wgu_s[:, H:] = wu_ref[0].astype(BF16)
        wd_s[...] = wd_ref[0].astype(BF16)

    @pl.when(b < nused_ref[0])
    def _():
        xb = jnp.concatenate(
            [xs_ref[pl.ds(s, rows, stride=SUBLANES), :].astype(BF16) for s in range(D // LANES)],
            axis=-1)
        h = _dot(xb, wgu_s[...])
        act = (jax.nn.silu(h[:, :H]) * h[:, H:]).astype(BF16)
        out = _dot(act, wd_s[...])
        for s in range(D // LANES):
            ys_ref[pl.ds(s, rows, stride=SUBLANES), :] = out[:, s * LANES:(s + 1) * LANES]


def _experts(blk_e, n_used, xs, w_gate, w_up, w_down):
    n_blocks = blk_e.shape[0]
    E, D, H = w_gate.shape
    rows = SLOT_BLOCK
    used_block = lambda b, be, nu: (jnp.minimum(b, nu[0] - 1), 0)
    return pl.pallas_call(
        _expert_kernel,
        grid_spec=pltpu.PrefetchScalarGridSpec(
            num_scalar_prefetch=2,
            grid=(n_blocks,),
            in_specs=[
                pl.BlockSpec((rows * SUBLANES, LANES), used_block),
                pl.BlockSpec((1, D, H), lambda b, be, nu: (be[b], 0, 0)),
                pl.BlockSpec((1, D, H), lambda b, be, nu: (be[b], 0, 0)),
                pl.BlockSpec((1, H, D), lambda b, be, nu: (be[b], 0, 0)),
            ],
            out_specs=pl.BlockSpec((rows * SUBLANES, LANES), used_block),
            scratch_shapes=[pltpu.VMEM((D, 2 * H), BF16), pltpu.VMEM((H, D), BF16)],
        ),
        out_shape=jax.ShapeDtypeStruct(xs.shape, F32),
        compiler_params=_cparams(("arbitrary",)),
        name="moe_experts",
    )(blk_e, n_used, xs, w_gate, w_up, w_down)


def _combine_kernel(d_ref, dn_ref, ys_ref, base_ref, rw_ref, g2_ref,
                    b2_ref, o_ref, buf0, buf1, sem0, sem1):
    toks, D = base_ref.shape
    i = pl.program_id(0)
    last = pl.num_programs(0) - 1

    def issue(dref, buf, sem):
        def gather_token(t, c):
            for k in range(TOP_K):
                row = pl.multiple_of(dref[t * TOP_K + k], SUBLANES)
                src = ys_ref.at[pl.ds(row, SUBLANES), :]
                dst = buf.at[pl.ds(pl.multiple_of((k * toks + t) * SUBLANES, SUBLANES), SUBLANES), :]
                pltpu.make_async_copy(src, dst, sem).start(priority=k % DMA_PRIORITIES)
            return c

        lax.fori_loop(0, toks, gather_token, 0)

    def finish(buf, sem):
        pltpu.make_async_copy(ys_ref.at[pl.ds(0, buf.shape[0]), :], buf, sem).wait()
        w = rw_ref[...]
        pieces = []
        for s in range(D // LANES):
            acc = jnp.zeros((toks, LANES), F32)
            for k in range(TOP_K):
                rows = buf[pl.ds(k * toks * SUBLANES + s, toks, stride=SUBLANES), :]
                acc = acc + w[:, k:k + 1] * rows
            pieces.append(acc)
        y = base_ref[...] + jnp.concatenate(pieces, axis=-1)
        o_ref[...] = _layer_norm(y, g2_ref[...], b2_ref[...])

    @pl.when(i == 0)
    def _():
        issue(d_ref, buf0, sem0)

    for parity, (cur, csem, nxt, nsem) in enumerate(((buf0, sem0, buf1, sem1), (buf1, sem1, buf0, sem0))):
        @pl.when(jnp.bitwise_and(i, 1) == parity)
        def _(cur=cur, csem=csem, nxt=nxt, nsem=nsem):
            @pl.when(i < last)
            def _():
                issue(dn_ref, nxt, nsem)
            finish(cur, csem)


def _combine(d_flat, ys, base, rw, g2, b2):
    T, D = base.shape
    toks = min(COMB_ROWS, T)
    steps = T // toks
    idx_now = pl.BlockSpec((toks * TOP_K,), lambda i: (i,), memory_space=pltpu.SMEM)
    idx_next = pl.BlockSpec((toks * TOP_K,), lambda i: (jnp.minimum(i + 1, steps - 1),),
                            memory_space=pltpu.SMEM)
    return pl.pallas_call(
        _combine_kernel,
        grid_spec=pltpu.PrefetchScalarGridSpec(
            num_scalar_prefetch=0,
            grid=(steps,),
            in_specs=[
                idx_now, idx_next,
                pl.BlockSpec(memory_space=pl.ANY),
                pl.BlockSpec((toks, D), lambda i: (i, 0)),
                pl.BlockSpec((toks, TOP_K), lambda i: (i, 0)),
                pl.BlockSpec(g2.shape, lambda i: (0, 0)),
                pl.BlockSpec(b2.shape, lambda i: (0, 0)),
            ],
            out_specs=pl.BlockSpec((toks, D), lambda i: (i, 0)),
            scratch_shapes=[
                pltpu.VMEM((TOP_K * toks * SUBLANES, LANES), F32),
                pltpu.VMEM((TOP_K * toks * SUBLANES, LANES), F32),
                pltpu.SemaphoreType.DMA(()),
                pltpu.SemaphoreType.DMA(()),
            ],
        ),
        out_shape=jax.ShapeDtypeStruct((T, D), F32),
        compiler_params=_cparams(("arbitrary",)),
        name="moe_combine",
    )(d_flat, d_flat, ys, base, rw, g2, b2)


def _overlap_matrix(ncp):
    n = np.arange(ncp)[:, None]
    j = np.arange(LANES)[None, :]
    start = n * CMP_STRIDE
    end = start + CMP_LEN - 1
    sel_start = j * SEL_BLOCK
    ovl = (start < sel_start + SEL_BLOCK) & (end >= sel_start)
    return jnp.asarray(ovl.astype(np.float32), dtype=BF16)


def _mixer(x2, batch, seq, w_in, conv_w, cmp_k, cmp_v):
    c3 = 3 * CONV_CH
    qd = N_HEADS * HEAD_DIM
    w_conv = w_in[:, :c3].astype(BF16)
    w_q = w_in[:, c3:c3 + qd].astype(BF16)
    w_kv = w_in[:, c3 + qd:c3 + qd + 6 * KV_DIM].astype(BF16)
    w_g = jnp.pad(w_in[:, c3 + qd + 6 * KV_DIM:], ((0, 0), (0, LANES - N_HEADS * N_BRANCH))).astype(BF16)
    conv_out, q, kv, gates = _proj_conv(x2, w_conv, w_q, w_kv, w_g, conv_w, batch, seq)
    kc, vc = _compress(kv[:, :KV_DIM], kv[:, KV_DIM:2 * KV_DIM], cmp_k, cmp_v, batch, seq)
    ncp = -(-kc.shape[1] // LANES) * LANES
    if ncp != kc.shape[1]:
        padn = ((0, 0), (0, ncp - kc.shape[1]), (0, 0))
        kc, vc = jnp.pad(kc, padn), jnp.pad(vc, padn)
    kv3 = kv.reshape(batch, seq, 6 * KV_DIM)
    nsa_out = _nsa(q, kc, vc, kv3, gates, _overlap_matrix(ncp), batch, seq)
    return conv_out, nsa_out


def _moe(x3, base, e_t, r_t, w_t, counts, w_gate, w_up, w_down, g2, b2):
    T = base.shape[0]
    A = T * TOP_K
    n_blocks = -(-(A + N_EXPERTS * (SLOT_BLOCK - 1)) // SLOT_BLOCK)
    cnt = counts[:, 0]
    padded = (cnt + SLOT_BLOCK - 1) // SLOT_BLOCK * SLOT_BLOCK
    pad_end = jnp.cumsum(padded)
    pad_start = (pad_end - padded).astype(I32)
    zero_off = jnp.where(padded > 0, pad_end - SLOT_BLOCK, -1).astype(I32)
    n_used = (pad_end[-1:] // SLOT_BLOCK).astype(I32)
    blk_start = jnp.arange(n_blocks, dtype=I32) * SLOT_BLOCK
    last_e = jnp.max(jnp.where(padded > 0, jnp.arange(N_EXPERTS, dtype=I32), 0))
    blk_e = jnp.minimum(jnp.sum((pad_end[None, :] <= blk_start[:, None]).astype(I32), axis=1),
                        last_e).astype(I32)
    d_flat = _slot_rows(pad_start, e_t, r_t).T.reshape(A)
    xs = _push(zero_off, d_flat, x3, n_blocks * SLOT_BLOCK)
    ys = _experts(blk_e, n_used, xs, w_gate, w_up, w_down)
    return _combine(d_flat, ys, base, w_t.T, g2, b2)


def kernel(x, w_in, conv_w, ck_pos, ck_w1, ck_b1, ck_w2, cv_pos, cv_w1, cv_b1, cv_w2, w_out, ln1_g, ln1_b, router_w, router_bias, w_gate, w_up, w_down, ws_gate, ws_up, ws_down, ln2_g, ln2_b):
    batch, seq, D = x.shape
    depth = w_in.shape[0]
    alpha = (2.0 * depth) ** 0.25
    x2 = x.reshape(batch * seq, D)
    for l in range(depth):
        cmp_k = _compress_weights(ck_pos[l], ck_w1[l], ck_b1[l], ck_w2[l])
        cmp_v = _compress_weights(cv_pos[l], cv_w1[l], cv_b1[l], cv_w2[l])
        conv_out, nsa_out = _mixer(x2, batch, seq, w_in[l], conv_w[l], cmp_k, cmp_v)
        rw_hi, rw_lo = _split_bf16(router_w[l].T)
        rbias = jnp.broadcast_to(router_bias[l][:, None], (N_EXPERTS, LANES))
        x3, base, e_t, r_t, w_t, counts = _post(
            x2, conv_out, nsa_out, w_out[l].astype(BF16), ln1_g[l][None, :], ln1_b[l][None, :],
            rw_hi, rw_lo, rbias,
            ws_gate[l].astype(BF16), ws_up[l].astype(BF16), ws_down[l].astype(BF16), alpha)
        x2 = _moe(x3, base, e_t, r_t, w_t, counts, w_gate[l], w_up[l], w_down[l],
                  ln2_g[l][None, :], ln2_b[l][None, :])
    return x2.reshape(batch, seq, D)
```

```python
import functools
import math

import jax
import jax.numpy as jnp
import numpy as np
from jax import lax
from jax.experimental import pallas as pl
from jax.experimental.pallas import tpu as pltpu

F32 = jnp.float32
BF16 = jnp.bfloat16
I32 = jnp.int32

CONV_CH = 512
CONV_WIDTH = 3
N_HEADS = 8
HEAD_DIM = 64
N_KV_HEADS = 2
Q_PER_KV = N_HEADS // N_KV_HEADS
KV_DIM = N_KV_HEADS * HEAD_DIM
N_BRANCH = 3
CMP_LEN = 32
CMP_STRIDE = 16
SEL_BLOCK = 64
SEL_TOPK = 8
WINDOW = 512
FORCED_SCORE = 1e4
N_EXPERTS = 256
TOP_K = 8
N_GROUPS = 8
TOPK_GROUPS = 4
GROUP_SIZE = N_EXPERTS // N_GROUPS
ROUTED_SCALE = 2.5
LN_EPS = 1e-5
NEG_INF = -1e30
SEL_SHIFT = SEL_BLOCK.bit_length() - 1
GROUP_SHIFT = GROUP_SIZE.bit_length() - 1
TOPK_SHIFT = TOP_K.bit_length() - 1

LANES = 128
SUBLANES = 8
VMEM_LIMIT = 56 * 1024 * 1024
DMA_PRIORITIES = 2

PROJ_ROWS = 512
NSA_Q = 128
NSA_KC = 512
POST_ROWS = 256
SLOT_BLOCK = 256
EXPERT_IN_BUFS = 4
EXPERT_OUT_BUFS = 2
PUSH_ROWS = 1024
COMB_ROWS = 256


def _dot(a, b):
    return jnp.dot(a, b, preferred_element_type=F32)


def _dot_t(a, b):
    return lax.dot_general(a, b, (((1,), (1,)), ((), ())), preferred_element_type=F32)


def _split_bf16(x):
    hi = x.astype(BF16)
    lo = (x - hi.astype(F32)).astype(BF16)
    return hi, lo


def _cparams(sem):
    return pltpu.CompilerParams(dimension_semantics=sem, vmem_limit_bytes=VMEM_LIMIT)


def _proj_conv_kernel(x_ref, wc_ref, wq_ref, wkv_ref, wg_ref, cw_ref,
                      conv_ref, q_ref, kv_ref, gate_ref, carry_ref):
    rows = x_ref.shape[0]

    @pl.when(pl.program_id(1) == 0)
    def _():
        carry_ref[...] = jnp.zeros_like(carry_ref)

    xb = x_ref[...].astype(BF16)
    acc = _dot(xb, wc_ref[...])
    b_g = acc[:, :CONV_CH]
    u = acc[:, CONV_CH:2 * CONV_CH] * acc[:, 2 * CONV_CH:]
    prev2 = carry_ref[SUBLANES - 2:SUBLANES - 1, :]
    prev1 = carry_ref[SUBLANES - 1:SUBLANES, :]
    ri = lax.broadcasted_iota(I32, (rows, 1), 0)
    u1 = jnp.where(ri == 0, prev1, pltpu.roll(u, 1, 0))
    u2 = jnp.where(ri == 0, prev2, jnp.where(ri == 1, prev1, pltpu.roll(u, 2, 0)))
    y = cw_ref[0:1, :] * u2 + cw_ref[1:2, :] * u1 + cw_ref[2:3, :] * u
    conv_ref[...] = (b_g * y).astype(BF16)
    carry_ref[...] = u[rows - SUBLANES:, :]

    q_ref[...] = (_dot(xb, wq_ref[...]) * (HEAD_DIM ** -0.5)).astype(BF16)
    kv_ref[...] = _dot(xb, wkv_ref[...]).astype(BF16)
    gate_ref[...] = jax.nn.sigmoid(_dot(xb, wg_ref[...]))


def _proj_conv(x2, w_conv, w_q, w_kv, w_g, conv_w, batch, seq):
    T, D = x2.shape
    rows = min(PROJ_ROWS, seq)
    nt = seq // rows
    row_map = lambda b, i: (b * nt + i, 0)
    fixed = lambda b, i: (0, 0)
    return pl.pallas_call(
        _proj_conv_kernel,
        grid=(batch, nt),
        in_specs=[
            pl.BlockSpec((rows, D), row_map),
            pl.BlockSpec(w_conv.shape, fixed),
            pl.BlockSpec(w_q.shape, fixed),
            pl.BlockSpec(w_kv.shape, fixed),
            pl.BlockSpec(w_g.shape, fixed),
            pl.BlockSpec(conv_w.shape, fixed),
        ],
        out_specs=[
            pl.BlockSpec((rows, CONV_CH), row_map),
            pl.BlockSpec((rows, N_HEADS * HEAD_DIM), row_map),
            pl.BlockSpec((rows, 6 * KV_DIM), row_map),
            pl.BlockSpec((rows, LANES), row_map),
        ],
        out_shape=[
            jax.ShapeDtypeStruct((T, CONV_CH), BF16),
            jax.ShapeDtypeStruct((T, N_HEADS * HEAD_DIM), BF16),
            jax.ShapeDtypeStruct((T, 6 * KV_DIM), BF16),
            jax.ShapeDtypeStruct((T, LANES), F32),
        ],
        scratch_shapes=[pltpu.VMEM((SUBLANES, CONV_CH), F32)],
        compiler_params=_cparams(("arbitrary", "arbitrary")),
        name="proj_conv",
    )(x2, w_conv, w_q, w_kv, w_g, conv_w)


def _compress_kernel(ck_ref, cv_ref, wtk_ref, wbk_ref, w2k_ref, ptk_ref, pbk_ref, b1k_ref,
                     wtv_ref, wbv_ref, w2v_ref, ptv_ref, pbv_ref, b1v_ref, kc_ref, vc_ref):
    def one(c_ref, wt_ref, wb_ref, w2_ref, pt_ref, pb_ref, b1_ref, o_ref):
        c = c_ref[0]
        top = _dot(c, wt_ref[...])
        bot = _dot(c, wb_ref[...])
        c0 = _dot(pt_ref[...], wt_ref[...]) + _dot(pb_ref[...], wb_ref[...]) + b1_ref[...]
        n = top.shape[0]
        h = top + pltpu.roll(bot, n - 1, 0) + c0[0:1, :]
        g = jax.nn.gelu(h, approximate=True)
        o_ref[0] = _dot(g.astype(BF16), w2_ref[...]).astype(BF16)

    one(ck_ref, wtk_ref, wbk_ref, w2k_ref, ptk_ref, pbk_ref, b1k_ref, kc_ref)
    one(cv_ref, wtv_ref, wbv_ref, w2v_ref, ptv_ref, pbv_ref, b1v_ref, vc_ref)


def _blockdiag2(w):
    z = jnp.zeros_like(w)
    return jnp.concatenate([jnp.concatenate([w, z], 1), jnp.concatenate([z, w], 1)], 0)


def _compress_weights(pos, w1, b1, w2):
    w1r = w1.reshape(CMP_LEN, HEAD_DIM, HEAD_DIM)
    eye = jnp.eye(N_KV_HEADS, dtype=w1.dtype)
    wfull = (w1r[:, None, :, None, :] * eye[None, :, None, :, None]).reshape(CMP_LEN, KV_DIM, KV_DIM)
    w_top = wfull[:CMP_STRIDE].reshape(CMP_STRIDE * KV_DIM, KV_DIM).astype(BF16)
    w_bot = wfull[CMP_STRIDE:].reshape(CMP_STRIDE * KV_DIM, KV_DIM).astype(BF16)
    posr = jnp.tile(pos, (1, N_KV_HEADS))
    pos_top = jnp.tile(posr[:CMP_STRIDE].reshape(1, -1), (SUBLANES, 1)).astype(BF16)
    pos_bot = jnp.tile(posr[CMP_STRIDE:].reshape(1, -1), (SUBLANES, 1)).astype(BF16)
    b1r = jnp.tile(b1[None, :], (SUBLANES, N_KV_HEADS)).astype(F32)
    return w_top, w_bot, _blockdiag2(w2).astype(BF16), pos_top, pos_bot, b1r


def _compress(kc_raw, vc_raw, wk, wv, batch, seq):
    chunks = seq // CMP_STRIDE
    width = CMP_STRIDE * KV_DIM
    ck = kc_raw.reshape(batch, chunks, width)
    cv = vc_raw.reshape(batch, chunks, width)
    bmap = lambda b: (b, 0, 0)
    fixed = lambda b: (0, 0)
    wspecs = [pl.BlockSpec(w.shape, fixed) for w in wk]
    return pl.pallas_call(
        _compress_kernel,
        grid=(batch,),
        in_specs=[pl.BlockSpec((1, chunks, width), bmap), pl.BlockSpec((1, chunks, width), bmap)]
        + wspecs + wspecs,
        out_specs=[pl.BlockSpec((1, chunks, KV_DIM), bmap)] * 2,
        out_shape=[jax.ShapeDtypeStruct((batch, chunks, KV_DIM), BF16)] * 2,
        compiler_params=_cparams(("arbitrary",)),
        name="compress",
    )(ck, cv, *wk, *wv)


def _softmax_rows(s, valid):
    s = jnp.where(valid, s, NEG_INF)
    m = jnp.max(s, axis=-1, keepdims=True)
    p = jnp.where(valid, jnp.exp(s - m), 0.0)
    l = jnp.sum(p, axis=-1, keepdims=True)
    inv = jnp.where(l > 0.0, 1.0 / l, 0.0)
    return p, inv


def _nsa_kernel(q_ref, kc_ref, vc_ref, ks_ref, vs_ref, kw_ref, vw_ref, gate_ref, ovl_ref,
                o_ref, *, seq, n_sel):
    tq = q_ref.shape[0]
    ncp = kc_ref.shape[1]
    rows = Q_PER_KV * tq
    q0 = pl.program_id(1) * tq
    t_col = q0 + lax.broadcasted_iota(I32, (tq, 1), 0)
    t4 = jnp.concatenate([t_col] * Q_PER_KV, axis=0)
    row_i = lax.broadcasted_iota(I32, (rows, 1), 0)
    lane = lax.broadcasted_iota(I32, (1, LANES), 1)
    feat_lane = lax.broadcasted_iota(I32, (1, HEAD_DIM), 1)
    n_sel_pad = -(-n_sel // SUBLANES) * SUBLANES
    blk_row = lax.broadcasted_iota(I32, (n_sel_pad, tq), 0)
    blk_row_f = blk_row.astype(F32)
    gates = gate_ref[...]
    win_len = WINDOW + tq
    w_start = pl.multiple_of(jnp.maximum(q0 - WINDOW, 0), tq)
    n_chunks = (q0 + tq + NSA_KC - 1) // NSA_KC
    outs, stage = [], []
    for g in range(N_KV_HEADS):
        lo, hi = g * HEAD_DIM, (g + 1) * HEAD_DIM
        qg = jnp.concatenate(
            [q_ref[:, (g * Q_PER_KV + r) * HEAD_DIM:(g * Q_PER_KV + r + 1) * HEAD_DIM]
             for r in range(Q_PER_KV)], axis=0)
        slope = jnp.zeros((rows, 1), F32)
        for r in range(Q_PER_KV):
            h = g * Q_PER_KV + r
            in_head = (row_i >= r * tq) & (row_i < (r + 1) * tq)
            slope = jnp.where(in_head, 2.0 ** (-8.0 * (h + 1) / N_HEADS), slope)

        cmp_end = lax.broadcasted_iota(I32, (1, ncp), 1) * CMP_STRIDE + (CMP_LEN - 1)
        d_c = t4 - cmp_end
        s_c = _dot_t(qg, kc_ref[0, :, lo:hi]) - slope * d_c.astype(F32)
        p_c, inv_c = _softmax_rows(s_c, d_c >= 0)
        p_c = p_c * inv_c
        o_cmp = _dot(p_c.astype(BF16), vc_ref[0, :, lo:hi])

        ps = p_c[0:tq]
        for r in range(1, Q_PER_KV):
            ps = ps + p_c[r * tq:(r + 1) * tq]
        ps_hi, ps_lo = _split_bf16(ps)
        imp = _dot(ps_hi, ovl_ref[...]) + _dot(ps_lo, ovl_ref[...])
        forced = (lane == 0) | (lane == jnp.right_shift(t_col, SEL_SHIFT))
        causal = lane * SEL_BLOCK <= t_col
        score = jnp.where(forced, FORCED_SCORE, jnp.where(causal, imp, -1.0))
        score_t = score.T[:n_sel_pad, :]
        score_t = jnp.where(blk_row < n_sel, score_t, -jnp.inf)
        sel_t = jnp.zeros((n_sel_pad, tq), F32)
        for _ in range(min(SEL_TOPK, n_sel)):
            mx = jnp.max(score_t, axis=0, keepdims=True)
            first = jnp.min(jnp.where(score_t == mx, blk_row_f, float(LANES)), axis=0, keepdims=True)
            hit = blk_row_f == first
            sel_t = jnp.where(hit, 1.0, sel_t)
            score_t = jnp.where(hit, -jnp.inf, score_t)
        unsel_t = jnp.concatenate(
            [jnp.where(sel_t > 0.5, 0.0, NEG_INF), jnp.full((LANES - n_sel_pad, tq), NEG_INF, F32)], axis=0)
        unsel_b = unsel_t.T.astype(BF16)

        q_feat = jnp.where(feat_lane == 0, slope * float(SEL_BLOCK),
                           jnp.where(feat_lane == 1, slope, 0.0)).astype(BF16)
        q_aug = jnp.concatenate([qg, q_feat], axis=1)
        lhs = jnp.concatenate([q_aug, jnp.concatenate([unsel_b] * Q_PER_KV, axis=0)], axis=1)

        stage.append((q_aug, lhs, o_cmp))

    def key_feat(k0, n):
        pos = k0 + lax.broadcasted_iota(I32, (n, HEAD_DIM), 0)
        ln = lax.broadcasted_iota(I32, (n, HEAD_DIM), 1)
        return jnp.where(ln == 0, jnp.right_shift(pos, SEL_SHIFT),
                         jnp.where(ln == 1, jnp.bitwise_and(pos, SEL_BLOCK - 1), 0)
                         ).astype(F32).astype(BF16)

    def sel_chunk(c, carry, diagonal):
        k0 = pl.multiple_of(c * NSA_KC, NSA_KC)
        kblk = jnp.right_shift(k0 + lax.broadcasted_iota(I32, (NSA_KC, LANES), 0), SEL_SHIFT)
        onehot = jnp.where(kblk == lax.broadcasted_iota(I32, (NSA_KC, LANES), 1), 1.0, 0.0)
        shared = jnp.concatenate([key_feat(k0, NSA_KC), onehot.astype(BF16)], axis=1)
        new = []
        for g in range(N_KV_HEADS):
            m, l, acc = carry[g]
            kch = ks_ref[0, pl.ds(k0, NSA_KC), g * HEAD_DIM:(g + 1) * HEAD_DIM]
            vch = vs_ref[0, pl.ds(k0, NSA_KC), g * HEAD_DIM:(g + 1) * HEAD_DIM]
            s = _dot_t(stage[g][1], jnp.concatenate([kch, shared], axis=1))
            if diagonal:
                pos = k0 + lax.broadcasted_iota(I32, (1, NSA_KC), 1)
                s = jnp.where(pos <= t4, s, NEG_INF)
            m_new = jnp.maximum(m, jnp.max(s, axis=-1, keepdims=True))
            a = jnp.exp(m - m_new)
            p = jnp.exp(s - m_new)
            l = a * l + jnp.sum(p, axis=-1, keepdims=True)
            acc = a * acc + _dot(p.astype(BF16), vch)
            new.append((m_new, l, acc))
        return tuple(new)

    init = tuple((jnp.full((rows, 1), NEG_INF, F32), jnp.zeros((rows, 1), F32),
                  jnp.zeros((rows, HEAD_DIM), F32)) for _ in range(N_KV_HEADS))
    carry = lax.fori_loop(0, n_chunks - 1, functools.partial(sel_chunk, diagonal=False), init)
    final = sel_chunk(n_chunks - 1, carry, diagonal=True)

    d_w = t_col - (w_start + lax.broadcasted_iota(I32, (1, win_len), 1))
    band = jnp.where((d_w >= 0) & (d_w < WINDOW), 0.0, NEG_INF)
    band4 = jnp.concatenate([band] * Q_PER_KV, axis=0)
    win_feat = key_feat(w_start, win_len)
    for g in range(N_KV_HEADS):
        lo, hi = g * HEAD_DIM, (g + 1) * HEAD_DIM
        q_aug, _, o_cmp = stage[g]
        _, l_s, acc_s = final[g]
        o_slc = acc_s * jnp.where(l_s > 0.0, 1.0 / l_s, 0.0)

        kwb = kw_ref[0, pl.ds(w_start, win_len), lo:hi]
        vwb = vw_ref[0, pl.ds(w_start, win_len), lo:hi]
        s_w = _dot_t(q_aug, jnp.concatenate([kwb, win_feat], axis=1)) + band4
        m_w = jnp.max(s_w, axis=-1, keepdims=True)
        p_w = jnp.exp(s_w - m_w)
        o_win = _dot(p_w.astype(BF16), vwb) * (1.0 / jnp.sum(p_w, axis=-1, keepdims=True))

        for r in range(Q_PER_KV):
            h = g * Q_PER_KV + r
            sl = slice(r * tq, (r + 1) * tq)
            gc = gates[:, h * N_BRANCH:h * N_BRANCH + 1]
            gs = gates[:, h * N_BRANCH + 1:h * N_BRANCH + 2]
            gw = gates[:, h * N_BRANCH + 2:h * N_BRANCH + 3]
            outs.append(gc * o_cmp[sl] + gs * o_slc[sl] + gw * o_win[sl])
    o_ref[...] = jnp.concatenate(outs, axis=-1).astype(BF16)


def _nsa(q, kc, vc, kv3, gates, overlap, batch, seq):
    T = q.shape[0]
    tq = min(NSA_Q, seq)
    nq = seq // tq
    ncp = kc.shape[1]
    n_sel = seq // SEL_BLOCK
    row_map = lambda b, i: (b * nq + i, 0)
    bmap = lambda b, i: (b, 0, 0)
    kvspec = lambda j: pl.BlockSpec((1, seq, KV_DIM), lambda b, i, j=j: (b, 0, j))
    return pl.pallas_call(
        functools.partial(_nsa_kernel, seq=seq, n_sel=n_sel),
        grid=(batch, nq),
        in_specs=[
            pl.BlockSpec((tq, N_HEADS * HEAD_DIM), row_map),
            pl.BlockSpec((1, ncp, KV_DIM), bmap),
            pl.BlockSpec((1, ncp, KV_DIM), bmap),
            kvspec(2), kvspec(3), kvspec(4), kvspec(5),
            pl.BlockSpec((tq, LANES), row_map),
            pl.BlockSpec(overlap.shape, lambda b, i: (0, 0)),
        ],
        out_specs=pl.BlockSpec((tq, N_HEADS * HEAD_DIM), row_map),
        out_shape=jax.ShapeDtypeStruct((T, N_HEADS * HEAD_DIM), BF16),
        compiler_params=_cparams(("arbitrary", "arbitrary")),
        name="nsa",
    )(q, kc, vc, kv3, kv3, kv3, kv3, gates, overlap)


def _layer_norm(y, g, b):
    mu = jnp.mean(y, axis=-1, keepdims=True)
    yc = y - mu
    var = jnp.mean(yc * yc, axis=-1, keepdims=True)
    return yc * lax.rsqrt(var + LN_EPS) * g + b


def _post_kernel(x_ref, conv_ref, nsa_ref, wo_ref, g1_ref, b1_ref, rwh_ref, rwl_ref, rb_ref,
                 wsg_ref, wsu_ref, wsd_ref,
                 x3_ref, base_ref, e_ref, r_ref, w_ref, cnt_ref, carry_ref, *, alpha):
    rows, D = x_ref.shape

    @pl.when(pl.program_id(0) == 0)
    def _():
        carry_ref[...] = jnp.zeros_like(carry_ref)

    half = wo_ref.shape[0] // 2
    mix = _dot(conv_ref[...], wo_ref[:half, :]) + _dot(nsa_ref[...], wo_ref[half:, :])
    x1 = _layer_norm(alpha * x_ref[...] + mix, g1_ref[...], b1_ref[...])
    for s in range(D // LANES):
        x3_ref[pl.ds(s, rows, stride=SUBLANES), :] = x1[:, s * LANES:(s + 1) * LANES]

    xh, xl = _split_bf16(x1)
    hid = jax.nn.silu(_dot(xh, wsg_ref[...])) * _dot(xh, wsu_ref[...])
    base_ref[...] = alpha * x1 + _dot(hid.astype(BF16), wsd_ref[...])

    logits = (_dot_t(rwh_ref[...], xh) + _dot_t(rwh_ref[...], xl) + _dot_t(rwl_ref[...], xh))
    scores = jax.nn.sigmoid(logits)
    reps = rows // LANES
    biased = scores + jnp.concatenate([rb_ref[...]] * reps, axis=1)
    eidx = lax.broadcasted_iota(I32, (N_EXPERTS, rows), 0).astype(F32)
    gidx = lax.broadcasted_iota(I32, (GROUP_SIZE, rows), 0).astype(F32)
    gvals, gscore = [], []
    for gi in range(N_GROUPS):
        v = biased[gi * GROUP_SIZE:(gi + 1) * GROUP_SIZE, :]
        m1 = jnp.max(v, axis=0, keepdims=True)
        i1 = jnp.min(jnp.where(v == m1, gidx, float(GROUP_SIZE)), axis=0, keepdims=True)
        m2 = jnp.max(jnp.where(gidx == i1, -jnp.inf, v), axis=0, keepdims=True)
        gvals.append(v)
        gscore.append(m1 + m2)
    cands = []
    for gi in range(N_GROUPS):
        ahead = jnp.zeros((1, rows), F32)
        for gj in range(N_GROUPS):
            if gj == gi:
                continue
            beats = (gscore[gj] >= gscore[gi]) if gj < gi else (gscore[gj] > gscore[gi])
            ahead = ahead + jnp.where(beats, 1.0, 0.0)
        ahead_full = jnp.broadcast_to(ahead, gvals[gi].shape)
        cands.append(jnp.where(ahead_full < float(TOPK_GROUPS), gvals[gi], NEG_INF))
    cand = jnp.concatenate(cands, axis=0)
    onehot = jnp.zeros((N_EXPERTS, rows), F32)
    idx_rows, w_rows = [], []
    for _ in range(TOP_K):
        mx = jnp.max(cand, axis=0, keepdims=True)
        first = jnp.min(jnp.where(cand == mx, eidx, float(N_EXPERTS)), axis=0, keepdims=True)
        hit = eidx == first
        idx_rows.append(first)
        w_rows.append(jnp.sum(jnp.where(hit, scores, 0.0), axis=0, keepdims=True))
        onehot = jnp.where(hit, 1.0, onehot)
        cand = jnp.where(hit, -jnp.inf, cand)
    wsum = w_rows[0]
    for k in range(1, TOP_K):
        wsum = wsum + w_rows[k]

    ti = lax.broadcasted_iota(I32, (rows, rows), 0)
    tj = lax.broadcasted_iota(I32, (rows, rows), 1)
    earlier = jnp.where(ti < tj, 1.0, 0.0).astype(BF16)
    carry = carry_ref[...]
    before = _dot(onehot.astype(BF16), earlier) + jnp.concatenate([carry] * reps, axis=1)
    krow = lax.broadcasted_iota(I32, (TOP_K, rows), 0)
    e_out = jnp.zeros((TOP_K, rows), F32)
    r_out = jnp.zeros((TOP_K, rows), F32)
    w_out = jnp.zeros((TOP_K, rows), F32)
    for k in range(TOP_K):
        rank = jnp.sum(jnp.where(eidx == idx_rows[k], before, 0.0), axis=0, keepdims=True)
        e_out = jnp.where(krow == k, idx_rows[k], e_out)
        r_out = jnp.where(krow == k, rank, r_out)
        w_out = jnp.where(krow == k, w_rows[k] / wsum * ROUTED_SCALE, w_out)
    e_ref[...] = e_out.astype(I32)
    r_ref[...] = r_out.astype(I32)
    w_ref[...] = w_out
    total = carry + jnp.sum(onehot, axis=1, keepdims=True)
    carry_ref[...] = total
    cnt_ref[...] = total.astype(I32)


def _post(x2, conv_out, nsa_out, w_out, g1, b1, rw_hi, rw_lo, rbias, wsg, wsu, wsd, alpha):
    T, D = x2.shape
    rows = min(POST_ROWS, T)
    row_map = lambda i: (i, 0)
    col_map = lambda i: (0, i)
    fixed = lambda i: (0, 0)
    full = lambda a: pl.BlockSpec(a.shape, fixed)
    return pl.pallas_call(
        functools.partial(_post_kernel, alpha=alpha),
        grid=(T // rows,),
        in_specs=[
            pl.BlockSpec((rows, D), row_map),
            pl.BlockSpec((rows, CONV_CH), row_map),
            pl.BlockSpec((rows, N_HEADS * HEAD_DIM), row_map),
            full(w_out), full(g1), full(b1), full(rw_hi), full(rw_lo), full(rbias),
            full(wsg), full(wsu), full(wsd),
        ],
        out_specs=[
            pl.BlockSpec((rows * SUBLANES, LANES), row_map),
            pl.BlockSpec((rows, D), row_map),
            pl.BlockSpec((TOP_K, rows), col_map),
            pl.BlockSpec((TOP_K, rows), col_map),
            pl.BlockSpec((TOP_K, rows), col_map),
            pl.BlockSpec((N_EXPERTS, LANES), fixed),
        ],
        out_shape=[
            jax.ShapeDtypeStruct((T * SUBLANES, LANES), F32),
            jax.ShapeDtypeStruct((T, D), F32),
            jax.ShapeDtypeStruct((TOP_K, T), I32),
            jax.ShapeDtypeStruct((TOP_K, T), I32),
            jax.ShapeDtypeStruct((TOP_K, T), F32),
            jax.ShapeDtypeStruct((N_EXPERTS, LANES), I32),
        ],
        scratch_shapes=[pltpu.VMEM((N_EXPERTS, LANES), F32)],
        compiler_params=_cparams(("arbitrary",)),
        name="post_attn_router",
    )(x2, conv_out, nsa_out, w_out, g1, b1, rw_hi, rw_lo, rbias, wsg, wsu, wsd)


def _slot_rows_kernel(pstart_ref, e_ref, r_ref, o_ref):
    e = e_ref[...]

    def add_expert(j, acc):
        return acc + jnp.where(e == j, pstart_ref[j], 0)

    o_ref[...] = lax.fori_loop(0, N_EXPERTS, add_expert, r_ref[...]) * SUBLANES


def _slot_rows(pad_start, e_t, r_t):
    T = e_t.shape[1]
    cols = min(2048, T)
    spec = pl.BlockSpec((TOP_K, cols), lambda i, *_: (0, i))
    return pl.pallas_call(
        _slot_rows_kernel,
        grid_spec=pltpu.PrefetchScalarGridSpec(
            num_scalar_prefetch=1, grid=(T // cols,), in_specs=[spec, spec], out_specs=spec),
        out_shape=jax.ShapeDtypeStruct(e_t.shape, I32),
        compiler_params=_cparams(("arbitrary",)),
        name="moe_slot_rows",
    )(pad_start, e_t, r_t)


def _push_kernel(zoff_ref, d_ref, x3_ref, xs_ref, zero_ref, sem, zsem):
    toks = x3_ref.shape[0] // SUBLANES
    zrows = zero_ref.shape[0]

    def zero_copy(e):
        off = pl.multiple_of(zoff_ref[e] * SUBLANES, SUBLANES)
        return pltpu.make_async_copy(zero_ref, xs_ref.at[pl.ds(off, zrows), :], zsem)

    @pl.when(pl.program_id(0) == 0)
    def _():
        zero_ref[...] = jnp.zeros_like(zero_ref)

        def start(e, c):
            @pl.when(zoff_ref[e] >= 0)
            def _():
                zero_copy(e).start()
            return c

        def wait(e, c):
            @pl.when(zoff_ref[e] >= 0)
            def _():
                zero_copy(e).wait()
            return c

        lax.fori_loop(0, N_EXPERTS, start, 0)
        lax.fori_loop(0, N_EXPERTS, wait, 0)

    def push_token(t, c):
        src = x3_ref.at[pl.ds(pl.multiple_of(t * SUBLANES, SUBLANES), SUBLANES), :]
        for k in range(TOP_K):
            row = pl.multiple_of(d_ref[t * TOP_K + k], SUBLANES)
            dst = xs_ref.at[pl.ds(row, SUBLANES), :]
            pltpu.make_async_copy(src, dst, sem).start(priority=k % DMA_PRIORITIES)
        return c

    lax.fori_loop(0, toks, push_token, 0)
    for _ in range(TOP_K):
        pltpu.make_async_copy(x3_ref, xs_ref.at[pl.ds(0, toks * SUBLANES), :], sem).wait()


def _push(zero_off, d_flat, x3, n_slots):
    T = x3.shape[0] // SUBLANES
    toks = min(PUSH_ROWS, T)
    return pl.pallas_call(
        _push_kernel,
        grid_spec=pltpu.PrefetchScalarGridSpec(
            num_scalar_prefetch=1,
            grid=(T // toks,),
            in_specs=[
                pl.BlockSpec((toks * TOP_K,), lambda i, *_: (i,), memory_space=pltpu.SMEM),
                pl.BlockSpec((toks * SUBLANES, LANES), lambda i, *_: (i, 0)),
            ],
            out_specs=pl.BlockSpec(memory_space=pl.ANY),
            scratch_shapes=[
                pltpu.VMEM((SLOT_BLOCK * SUBLANES, LANES), F32),
                pltpu.SemaphoreType.DMA(()),
                pltpu.SemaphoreType.DMA(()),
            ],
        ),
        out_shape=jax.ShapeDtypeStruct((n_slots * SUBLANES, LANES), F32),
        compiler_params=_cparams(("arbitrary",)),
        name="moe_push",
    )(zero_off, d_flat, x3)


def _expert_kernel(blk_e_ref, nused_ref, xs_hbm, wg_ref, wu_ref, wd_ref, ys_hbm,
                   wgu_s, wd_s, xbuf, ybuf, xsem, ysem):
    b = pl.program_id(0)
    n_used = nused_ref[0]
    blk_rows = xbuf.shape[1]
    rows = blk_rows // SUBLANES
    D = wg_ref.shape[1]
    H = wg_ref.shape[2]

    def x_copy(blk, slot):
        src = xs_hbm.at[pl.ds(pl.multiple_of(blk * blk_rows, blk_rows), blk_rows), :]
        return pltpu.make_async_copy(src, xbuf.at[slot], xsem.at[slot])

    def y_copy(blk, slot):
        dst = ys_hbm.at[pl.ds(pl.multiple_of(blk * blk_rows, blk_rows), blk_rows), :]
        return pltpu.make_async_copy(ybuf.at[slot], dst, ysem.at[slot])

    @pl.when(b == 0)
    def _():
        for j in range(EXPERT_IN_BUFS - 1):
            @pl.when(j < n_used)
            def _(j=j):
                x_copy(j, j).start()

    prev = blk_e_ref[jnp.maximum(b - 1, 0)]

    @pl.when((b == 0) | (blk_e_ref[b] != prev))
    def _():
        wgu_s[:, :H] = wg_ref[0].astype(BF16)
        wgu_s[:, H:] = wu_ref[0].astype(BF16)
        wd_s[...] = wd_ref[0].astype(BF16)

    @pl.when(b < n_used)
    def _():
        ahead = b + (EXPERT_IN_BUFS - 1)

        @pl.when(ahead < n_used)
        def _():
            x_copy(ahead, jnp.bitwise_and(ahead, EXPERT_IN_BUFS - 1)).start()

        slot = jnp.bitwise_and(b, EXPERT_IN_BUFS - 1)
        x_copy(b, slot).wait()
        xv = xbuf.at[slot]
        xb = jnp.concatenate(
            [xv[pl.ds(s, rows, stride=SUBLANES), :].astype(BF16) for s in range(D // LANES)],
            axis=-1)
        h = _dot(xb, wgu_s[...])
        act = (jax.nn.silu(h[:, :H]) * h[:, H:]).astype(BF16)
        out = _dot(act, wd_s[...])

        yslot = jnp.bitwise_and(b, EXPERT_OUT_BUFS - 1)

        @pl.when(b >= EXPERT_OUT_BUFS)
        def _():
            y_copy(b - EXPERT_OUT_BUFS, yslot).wait()

        yv = ybuf.at[yslot]
        for s in range(D // LANES):
            yv[pl.ds(s, rows, stride=SUBLANES), :] = out[:, s * LANES:(s + 1) * LANES]
        y_copy(b, yslot).start()

    @pl.when(b == pl.num_programs(0) - 1)
    def _():
        for j in range(EXPERT_OUT_BUFS):
            blk = n_used - 1 - j

            @pl.when(blk >= 0)
            def _(blk=blk):
                y_copy(blk, jnp.bitwise_and(blk, EXPERT_OUT_BUFS - 1)).wait()


def _experts(blk_e, n_used, xs, w_gate, w_up, w_down):
    n_blocks = blk_e.shape[0]
    E, D, H = w_gate.shape
    blk_rows = SLOT_BLOCK * SUBLANES
    return pl.pallas_call(
        _expert_kernel,
        grid_spec=pltpu.PrefetchScalarGridSpec(
            num_scalar_prefetch=2,
            grid=(n_blocks,),
            in_specs=[
                pl.BlockSpec(memory_space=pl.ANY),
                pl.BlockSpec((1, D, H), lambda b, be, nu: (be[b], 0, 0)),
                pl.BlockSpec((1, D, H), lambda b, be, nu: (be[b], 0, 0)),
                pl.BlockSpec((1, H, D), lambda b, be, nu: (be[b], 0, 0)),
            ],
            out_specs=pl.BlockSpec(memory_space=pl.ANY),
            scratch_shapes=[
                pltpu.VMEM((D, 2 * H), BF16), pltpu.VMEM((H, D), BF16),
                pltpu.VMEM((EXPERT_IN_BUFS, blk_rows, LANES), F32),
                pltpu.VMEM((EXPERT_OUT_BUFS, blk_rows, LANES), F32),
                pltpu.SemaphoreType.DMA((EXPERT_IN_BUFS,)),
                pltpu.SemaphoreType.DMA((EXPERT_OUT_BUFS,)),
            ],
        ),
        out_shape=jax.ShapeDtypeStruct(xs.shape, F32),
        compiler_params=_cparams(("arbitrary",)),
        name="moe_experts",
    )(blk_e, n_used, xs, w_gate, w_up, w_down)


def _combine_kernel(d_ref, dn_ref, ys_ref, base_ref, rw_ref, g2_ref,
                    b2_ref, o_ref, buf0, buf1, sem0, sem1):
    toks, D = base_ref.shape
    i = pl.program_id(0)
    last = pl.num_programs(0) - 1

    def issue(dref, buf, sem):
        def gather_token(t, c):
            for k in range(TOP_K):
                row = pl.multiple_of(dref[t * TOP_K + k], SUBLANES)
                src = ys_ref.at[pl.ds(row, SUBLANES), :]
                dst = buf.at[pl.ds(pl.multiple_of((k * toks + t) * SUBLANES, SUBLANES), SUBLANES), :]
                pltpu.make_async_copy(src, dst, sem).start(priority=k % DMA_PRIORITIES)
            return c

        lax.fori_loop(0, toks, gather_token, 0)

    def finish(buf, sem):
        pltpu.make_async_copy(ys_ref.at[pl.ds(0, buf.shape[0]), :], buf, sem).wait()
        w = rw_ref[...]
        pieces = []
        for s in range(D // LANES):
            acc = jnp.zeros((toks, LANES), F32)
            for k in range(TOP_K):
                rows = buf[pl.ds(k * toks * SUBLANES + s, toks, stride=SUBLANES), :]
                acc = acc + w[:, k:k + 1] * rows
            pieces.append(acc)
        y = base_ref[...] + jnp.concatenate(pieces, axis=-1)
        o_ref[...] = _layer_norm(y, g2_ref[...], b2_ref[...])

    @pl.when(i == 0)
    def _():
        issue(d_ref, buf0, sem0)

    for parity, (cur, csem, nxt, nsem) in enumerate(((buf0, sem0, buf1, sem1), (buf1, sem1, buf0, sem0))):
        @pl.when(jnp.bitwise_and(i, 1) == parity)
        def _(cur=cur, csem=csem, nxt=nxt, nsem=nsem):
            @pl.when(i < last)
            def _():
                issue(dn_ref, nxt, nsem)
            finish(cur, csem)


def _combine(d_flat, ys, base, rw, g2, b2):
    T, D = base.shape
    toks = min(COMB_ROWS, T)
    steps = T // toks
    idx_now = pl.BlockSpec((toks * TOP_K,), lambda i: (i,), memory_space=pltpu.SMEM)
    idx_next = pl.BlockSpec((toks * TOP_K,), lambda i: (jnp.minimum(i + 1, steps - 1),),
                            memory_space=pltpu.SMEM)
    return pl.pallas_call(
        _combine_kernel,
        grid_spec=pltpu.PrefetchScalarGridSpec(
            num_scalar_prefetch=0,
            grid=(steps,),
            in_specs=[
                idx_now, idx_next,
                pl.BlockSpec(memory_space=pl.ANY),
                pl.BlockSpec((toks, D), lambda i: (i, 0)),
                pl.BlockSpec((toks, TOP_K), lambda i: (i, 0)),
                pl.BlockSpec(g2.shape, lambda i: (0, 0)),
                pl.BlockSpec(b2.shape, lambda i: (0, 0)),
            ],
            out_specs=pl.BlockSpec((toks, D), lambda i: (i, 0)),
            scratch_shapes=[
                pltpu.VMEM((TOP_K * toks * SUBLANES, LANES), F32),
                pltpu.VMEM((TOP_K * toks * SUBLANES, LANES), F32),
                pltpu.SemaphoreType.DMA(()),
                pltpu.SemaphoreType.DMA(()),
            ],
        ),
        out_shape=jax.ShapeDtypeStruct((T, D), F32),
        compiler_params=_cparams(("arbitrary",)),
        name="moe_combine",
    )(d_flat, d_flat, ys, base, rw, g2, b2)


def _overlap_matrix(ncp):
    n = np.arange(ncp)[:, None]
    j = np.arange(LANES)[None, :]
    start = n * CMP_STRIDE
    end = start + CMP_LEN - 1
    sel_start = j * SEL_BLOCK
    ovl = (start < sel_start + SEL_BLOCK) & (end >= sel_start)
    return jnp.asarray(ovl.astype(np.float32), dtype=BF16)


def _mixer(x2, batch, seq, w_in, conv_w, cmp_k, cmp_v):
    c3 = 3 * CONV_CH
    qd = N_HEADS * HEAD_DIM
    w_conv = w_in[:, :c3].astype(BF16)
    w_q = w_in[:, c3:c3 + qd].astype(BF16)
    w_kv = w_in[:, c3 + qd:c3 + qd + 6 * KV_DIM].astype(BF16)
    w_g = jnp.pad(w_in[:, c3 + qd + 6 * KV_DIM:], ((0, 0), (0, LANES - N_HEADS * N_BRANCH))).astype(BF16)
    conv_out, q, kv, gates = _proj_conv(x2, w_conv, w_q, w_kv, w_g, conv_w, batch, seq)
    kc, vc = _compress(kv[:, :KV_DIM], kv[:, KV_DIM:2 * KV_DIM], cmp_k, cmp_v, batch, seq)
    ncp = -(-kc.shape[1] // LANES) * LANES
    if ncp != kc.shape[1]:
        padn = ((0, 0), (0, ncp - kc.shape[1]), (0, 0))
        kc, vc = jnp.pad(kc, padn), jnp.pad(vc, padn)
    kv3 = kv.reshape(batch, seq, 6 * KV_DIM)
    nsa_out = _nsa(q, kc, vc, kv3, gates, _overlap_matrix(ncp), batch, seq)
    return conv_out, nsa_out


def _moe(x3, base, e_t, r_t, w_t, counts, w_gate, w_up, w_down, g2, b2):
    T = base.shape[0]
    A = T * TOP_K
    n_blocks = -(-(A + N_EXPERTS * (SLOT_BLOCK - 1)) // SLOT_BLOCK)
    cnt = counts[:, 0]
    padded = (cnt + SLOT_BLOCK - 1) // SLOT_BLOCK * SLOT_BLOCK
    pad_end = jnp.cumsum(padded)
    pad_start = (pad_end - padded).astype(I32)
    zero_off = jnp.where(padded > 0, pad_end - SLOT_BLOCK, -1).astype(I32)
    n_used = (pad_end[-1:] // SLOT_BLOCK).astype(I32)
    blk_start = jnp.arange(n_blocks, dtype=I32) * SLOT_BLOCK
    last_e = jnp.max(jnp.where(padded > 0, jnp.arange(N_EXPERTS, dtype=I32), 0))
    blk_e = jnp.minimum(jnp.sum((pad_end[None, :] <= blk_start[:, None]).astype(I32), axis=1),
                        last_e).astype(I32)
    d_flat = _slot_rows(pad_start, e_t, r_t).T.reshape(A)
    xs = _push(zero_off, d_flat, x3, n_blocks * SLOT_BLOCK)
    ys = _experts(blk_e, n_used, xs, w_gate, w_up, w_down)
    return _combine(d_flat, ys, base, w_t.T, g2, b2)


def kernel(x, w_in, conv_w, ck_pos, ck_w1, ck_b1, ck_w2, cv_pos, cv_w1, cv_b1, cv_w2, w_out, ln1_g, ln1_b, router_w, router_bias, w_gate, w_up, w_down, ws_gate, ws_up, ws_down, ln2_g, ln2_b):
    batch, seq, D = x.shape
    depth = w_in.shape[0]
    alpha = (2.0 * depth) ** 0.25
    x2 = x.reshape(batch * seq, D)
    for l in range(depth):
        cmp_k = _compress_weights(ck_pos[l], ck_w1[l], ck_b1[l], ck_w2[l])
        cmp_v = _compress_weights(cv_pos[l], cv_w1[l], cv_b1[l], cv_w2[l])
        conv_out, nsa_out = _mixer(x2, batch, seq, w_in[l], conv_w[l], cmp_k, cmp_v)
        rw_hi, rw_lo = _split_bf16(router_w[l].T)
        rbias = jnp.broadcast_to(router_bias[l][:, None], (N_EXPERTS, LANES))
        x3, base, e_t, r_t, w_t, counts = _post(
            x2, conv_out, nsa_out, w_out[l].astype(BF16), ln1_g[l][None, :], ln1_b[l][None, :],
            rw_hi, rw_lo, rbias,
            ws_gate[l].astype(BF16), ws_up[l].astype(BF16), ws_down[l].astype(BF16), alpha)
        x2 = _moe(x3, base, e_t, r_t, w_t, counts, w_gate[l], w_up[l], w_down[l],
                  ln2_g[l][None, :], ln2_b[l][None, :])
    return x2.reshape(batch, seq, D)
```

```python
import functools
import math

import jax
import jax.numpy as jnp
import numpy as np
from jax import lax
from jax.experimental import pallas as pl
from jax.experimental.pallas import tpu as pltpu

F32 = jnp.float32
BF16 = jnp.bfloat16
I32 = jnp.int32

CONV_CH = 512
CONV_WIDTH = 3
N_HEADS = 8
HEAD_DIM = 64
N_KV_HEADS = 2
Q_PER_KV = N_HEADS // N_KV_HEADS
KV_DIM = N_KV_HEADS * HEAD_DIM
N_BRANCH = 3
CMP_LEN = 32
CMP_STRIDE = 16
SEL_BLOCK = 64
SEL_TOPK = 8
WINDOW = 512
FORCED_SCORE = 1e4
N_EXPERTS = 256
TOP_K = 8
N_GROUPS = 8
TOPK_GROUPS = 4
GROUP_SIZE = N_EXPERTS // N_GROUPS
ROUTED_SCALE = 2.5
LN_EPS = 1e-5
NEG_INF = -1e30
SEL_SHIFT = SEL_BLOCK.bit_length() - 1
GROUP_SHIFT = GROUP_SIZE.bit_length() - 1
TOPK_SHIFT = TOP_K.bit_length() - 1

LANES = 128
SUBLANES = 8
VMEM_LIMIT = 56 * 1024 * 1024
DMA_PRIORITIES = 2

PROJ_ROWS = 512
NSA_Q = 128
NSA_KC = 512
POST_ROWS = 256
SLOT_BLOCK = 256
EXPERT_IN_BUFS = 4
EXPERT_OUT_BUFS = 2
PUSH_ROWS = 1024
COMB_ROWS = 256


def _dot(a, b):
    return jnp.dot(a, b, preferred_element_type=F32)


def _dot_t(a, b):
    return lax.dot_general(a, b, (((1,), (1,)), ((), ())), preferred_element_type=F32)


def _split_bf16(x):
    hi = x.astype(BF16)
    lo = (x - hi.astype(F32)).astype(BF16)
    return hi, lo


def _cparams(sem):
    return pltpu.CompilerParams(dimension_semantics=sem, vmem_limit_bytes=VMEM_LIMIT)


def _proj_conv_kernel(x_ref, wc_ref, wq_ref, wkv_ref, wg_ref, cw_ref,
                      conv_ref, q_ref, kv_ref, gate_ref, carry_ref):
    rows = x_ref.shape[0]

    @pl.when(pl.program_id(1) == 0)
    def _():
        carry_ref[...] = jnp.zeros_like(carry_ref)

    xb = x_ref[...].astype(BF16)
    acc = _dot(xb, wc_ref[...])
    b_g = acc[:, :CONV_CH]
    u = acc[:, CONV_CH:2 * CONV_CH] * acc[:, 2 * CONV_CH:]
    prev2 = carry_ref[SUBLANES - 2:SUBLANES - 1, :]
    prev1 = carry_ref[SUBLANES - 1:SUBLANES, :]
    ri = lax.broadcasted_iota(I32, (rows, 1), 0)
    u1 = jnp.where(ri == 0, prev1, pltpu.roll(u, 1, 0))
    u2 = jnp.where(ri == 0, prev2, jnp.where(ri == 1, prev1, pltpu.roll(u, 2, 0)))
    y = cw_ref[0:1, :] * u2 + cw_ref[1:2, :] * u1 + cw_ref[2:3, :] * u
    conv_ref[...] = (b_g * y).astype(BF16)
    carry_ref[...] = u[rows - SUBLANES:, :]

    q_ref[...] = (_dot(xb, wq_ref[...]) * (HEAD_DIM ** -0.5)).astype(BF16)
    gate_ref[...] = jax.nn.sigmoid(_dot(xb, wg_ref[...]))

    kv = _dot(xb, wkv_ref[...])
    n_plain, n_key = 2, 2 * N_KV_HEADS
    key_w = n_key * LANES
    pos = pl.program_id(1) * rows + lax.broadcasted_iota(I32, (rows, key_w), 0)
    l128 = jnp.bitwise_and(lax.broadcasted_iota(I32, (rows, key_w), 1), LANES - 1)
    feat = jnp.where(l128 == HEAD_DIM, jnp.right_shift(pos, SEL_SHIFT),
                     jnp.where(l128 == HEAD_DIM + 1, jnp.bitwise_and(pos, SEL_BLOCK - 1), 0))
    ones = jnp.where(l128 >= HEAD_DIM, 1.0, 0.0)
    k0, v0 = n_plain * LANES, n_plain * LANES + key_w
    kv_ref[:, :k0] = kv[:, :k0].astype(BF16)
    kv_ref[:, k0:v0] = (kv[:, k0:v0] + feat.astype(F32)).astype(BF16)
    kv_ref[:, v0:] = (kv[:, v0:] + ones).astype(BF16)


def _proj_conv(x2, w_conv, w_q, w_kv, w_g, conv_w, batch, seq):
    T, D = x2.shape
    rows = min(PROJ_ROWS, seq)
    nt = seq // rows
    row_map = lambda b, i: (b * nt + i, 0)
    fixed = lambda b, i: (0, 0)
    return pl.pallas_call(
        _proj_conv_kernel,
        grid=(batch, nt),
        in_specs=[
            pl.BlockSpec((rows, D), row_map),
            pl.BlockSpec(w_conv.shape, fixed),
            pl.BlockSpec(w_q.shape, fixed),
            pl.BlockSpec(w_kv.shape, fixed),
            pl.BlockSpec(w_g.shape, fixed),
            pl.BlockSpec(conv_w.shape, fixed),
        ],
        out_specs=[
            pl.BlockSpec((rows, CONV_CH), row_map),
            pl.BlockSpec((rows, N_HEADS * HEAD_DIM), row_map),
            pl.BlockSpec((rows, w_kv.shape[1]), row_map),
            pl.BlockSpec((rows, LANES), row_map),
        ],
        out_shape=[
            jax.ShapeDtypeStruct((T, CONV_CH), BF16),
            jax.ShapeDtypeStruct((T, N_HEADS * HEAD_DIM), BF16),
            jax.ShapeDtypeStruct((T, w_kv.shape[1]), BF16),
            jax.ShapeDtypeStruct((T, LANES), F32),
        ],
        scratch_shapes=[pltpu.VMEM((SUBLANES, CONV_CH), F32)],
        compiler_params=_cparams(("arbitrary", "arbitrary")),
        name="proj_conv",
    )(x2, w_conv, w_q, w_kv, w_g, conv_w)


def _compress_kernel(ck_ref, cv_ref, wtk_ref, wbk_ref, w2k_ref, ptk_ref, pbk_ref, b1k_ref,
                     wtv_ref, wbv_ref, w2v_ref, ptv_ref, pbv_ref, b1v_ref, kc_ref, vc_ref):
    def one(c_ref, wt_ref, wb_ref, w2_ref, pt_ref, pb_ref, b1_ref, o_ref):
        c = c_ref[0]
        top = _dot(c, wt_ref[...])
        bot = _dot(c, wb_ref[...])
        c0 = _dot(pt_ref[...], wt_ref[...]) + _dot(pb_ref[...], wb_ref[...]) + b1_ref[...]
        n = top.shape[0]
        h = top + pltpu.roll(bot, n - 1, 0) + c0[0:1, :]
        g = jax.nn.gelu(h, approximate=True)
        o_ref[0] = _dot(g.astype(BF16), w2_ref[...]).astype(BF16)

    one(ck_ref, wtk_ref, wbk_ref, w2k_ref, ptk_ref, pbk_ref, b1k_ref, kc_ref)
    one(cv_ref, wtv_ref, wbv_ref, w2v_ref, ptv_ref, pbv_ref, b1v_ref, vc_ref)


def _blockdiag2(w):
    z = jnp.zeros_like(w)
    return jnp.concatenate([jnp.concatenate([w, z], 1), jnp.concatenate([z, w], 1)], 0)


def _compress_weights(pos, w1, b1, w2):
    w1r = w1.reshape(CMP_LEN, HEAD_DIM, HEAD_DIM)
    eye = jnp.eye(N_KV_HEADS, dtype=w1.dtype)
    wfull = (w1r[:, None, :, None, :] * eye[None, :, None, :, None]).reshape(CMP_LEN, KV_DIM, KV_DIM)
    w_top = wfull[:CMP_STRIDE].reshape(CMP_STRIDE * KV_DIM, KV_DIM).astype(BF16)
    w_bot = wfull[CMP_STRIDE:].reshape(CMP_STRIDE * KV_DIM, KV_DIM).astype(BF16)
    posr = jnp.tile(pos, (1, N_KV_HEADS))
    pos_top = jnp.tile(posr[:CMP_STRIDE].reshape(1, -1), (SUBLANES, 1)).astype(BF16)
    pos_bot = jnp.tile(posr[CMP_STRIDE:].reshape(1, -1), (SUBLANES, 1)).astype(BF16)
    b1r = jnp.tile(b1[None, :], (SUBLANES, N_KV_HEADS)).astype(F32)
    return w_top, w_bot, _blockdiag2(w2).astype(BF16), pos_top, pos_bot, b1r


def _compress(kc_raw, vc_raw, wk, wv, batch, seq):
    chunks = seq // CMP_STRIDE
    width = CMP_STRIDE * KV_DIM
    ck = kc_raw.reshape(batch, chunks, width)
    cv = vc_raw.reshape(batch, chunks, width)
    bmap = lambda b: (b, 0, 0)
    fixed = lambda b: (0, 0)
    wspecs = [pl.BlockSpec(w.shape, fixed) for w in wk]
    return pl.pallas_call(
        _compress_kernel,
        grid=(batch,),
        in_specs=[pl.BlockSpec((1, chunks, width), bmap), pl.BlockSpec((1, chunks, width), bmap)]
        + wspecs + wspecs,
        out_specs=[pl.BlockSpec((1, chunks, KV_DIM), bmap)] * 2,
        out_shape=[jax.ShapeDtypeStruct((batch, chunks, KV_DIM), BF16)] * 2,
        compiler_params=_cparams(("arbitrary",)),
        name="compress",
    )(ck, cv, *wk, *wv)


def _softmax_rows(s, valid):
    s = jnp.where(valid, s, NEG_INF)
    m = jnp.max(s, axis=-1, keepdims=True)
    p = jnp.where(valid, jnp.exp(s - m), 0.0)
    l = jnp.sum(p, axis=-1, keepdims=True)
    inv = jnp.where(l > 0.0, 1.0 / l, 0.0)
    return p, inv


def _nsa_kernel(q_ref, kc_ref, vc_ref, ks0_ref, ks1_ref, kw0_ref, kw1_ref, vs0_ref, vs1_ref,
                vw0_ref, vw1_ref, gate_ref, ovl_ref, hot_ref, o_ref, *, seq, n_sel):
    ks_refs, kw_refs = (ks0_ref, ks1_ref), (kw0_ref, kw1_ref)
    vs_refs, vw_refs = (vs0_ref, vs1_ref), (vw0_ref, vw1_ref)
    tq = q_ref.shape[0]
    ncp = kc_ref.shape[1]
    rows = Q_PER_KV * tq
    q0 = pl.program_id(1) * tq
    t_col = q0 + lax.broadcasted_iota(I32, (tq, 1), 0)
    t4 = jnp.concatenate([t_col] * Q_PER_KV, axis=0)
    row_i = lax.broadcasted_iota(I32, (rows, 1), 0)
    lane = lax.broadcasted_iota(I32, (1, LANES), 1)
    feat_lane = lax.broadcasted_iota(I32, (1, HEAD_DIM), 1)
    n_sel_pad = -(-n_sel // SUBLANES) * SUBLANES
    blk_row = lax.broadcasted_iota(I32, (n_sel_pad, tq), 0)
    blk_row_f = blk_row.astype(F32)
    gates = gate_ref[...]
    win_len = WINDOW + tq
    w_start = pl.multiple_of(jnp.maximum(q0 - WINDOW, 0), tq)
    n_chunks = (q0 + tq + NSA_KC - 1) // NSA_KC
    outs, stage = [], []
    for g in range(N_KV_HEADS):
        lo, hi = g * HEAD_DIM, (g + 1) * HEAD_DIM
        qg = jnp.concatenate(
            [q_ref[:, (g * Q_PER_KV + r) * HEAD_DIM:(g * Q_PER_KV + r + 1) * HEAD_DIM]
             for r in range(Q_PER_KV)], axis=0)
        slope = jnp.zeros((rows, 1), F32)
        for r in range(Q_PER_KV):
            h = g * Q_PER_KV + r
            in_head = (row_i >= r * tq) & (row_i < (r + 1) * tq)
            slope = jnp.where(in_head, 2.0 ** (-8.0 * (h + 1) / N_HEADS), slope)

        cmp_end = lax.broadcasted_iota(I32, (1, ncp), 1) * CMP_STRIDE + (CMP_LEN - 1)
        d_c = t4 - cmp_end
        s_c = _dot_t(qg, kc_ref[0, :, lo:hi]) - slope * d_c.astype(F32)
        p_c, inv_c = _softmax_rows(s_c, d_c >= 0)
        p_c = p_c * inv_c
        o_cmp = _dot(p_c.astype(BF16), vc_ref[0, :, lo:hi])

        ps = p_c[0:tq]
        for r in range(1, Q_PER_KV):
            ps = ps + p_c[r * tq:(r + 1) * tq]
        ps_hi, ps_lo = _split_bf16(ps)
        imp = _dot(ps_hi, ovl_ref[...]) + _dot(ps_lo, ovl_ref[...])
        forced = (lane == 0) | (lane == jnp.right_shift(t_col, SEL_SHIFT))
        causal = lane * SEL_BLOCK <= t_col
        score = jnp.where(forced, FORCED_SCORE, jnp.where(causal, imp, -1.0))
        score_t = score.T[:n_sel_pad, :]
        score_t = jnp.where(blk_row < n_sel, score_t, -jnp.inf)
        sel_t = jnp.zeros((n_sel_pad, tq), F32)
        for _ in range(min(SEL_TOPK, n_sel)):
            mx = jnp.max(score_t, axis=0, keepdims=True)
            first = jnp.min(jnp.where(score_t == mx, blk_row_f, float(LANES)), axis=0, keepdims=True)
            hit = blk_row_f == first
            sel_t = jnp.where(hit, 1.0, sel_t)
            score_t = jnp.where(hit, -jnp.inf, score_t)
        unsel_t = jnp.concatenate(
            [jnp.where(sel_t > 0.5, 0.0, NEG_INF), jnp.full((LANES - n_sel_pad, tq), NEG_INF, F32)], axis=0)
        unsel_b = unsel_t.T.astype(BF16)

        q_feat = jnp.where(feat_lane == 0, slope * float(SEL_BLOCK),
                           jnp.where(feat_lane == 1, slope, 0.0)).astype(BF16)
        q_aug = jnp.concatenate([qg, q_feat], axis=1)
        lhs = jnp.concatenate([q_aug, jnp.concatenate([unsel_b] * Q_PER_KV, axis=0)], axis=1)

        stage.append((q_aug, lhs, o_cmp))

    def sel_chunk(c, carry, diagonal):
        k0 = pl.multiple_of(c * NSA_KC, NSA_KC)
        hot = hot_ref[pl.ds(k0, NSA_KC), :]
        new = []
        for g in range(N_KV_HEADS):
            m, acc = carry[g]
            kch = ks_refs[g][0, pl.ds(k0, NSA_KC), :]
            s = _dot_t(stage[g][1], jnp.concatenate([kch, hot], axis=1))
            if diagonal:
                pos = k0 + lax.broadcasted_iota(I32, (1, NSA_KC), 1)
                s = jnp.where(pos <= t4, s, NEG_INF)
            m_new = jnp.maximum(m, jnp.max(s, axis=-1, keepdims=True))
            a = jnp.exp(m - m_new)
            p = jnp.exp((s - m_new).astype(BF16))
            acc = a * acc + _dot(p, vs_refs[g][0, pl.ds(k0, NSA_KC), :])
            new.append((m_new, acc))
        return tuple(new)

    init = tuple((jnp.full((rows, 1), NEG_INF, F32), jnp.zeros((rows, 2 * HEAD_DIM), F32))
                 for _ in range(N_KV_HEADS))
    carry = lax.fori_loop(0, n_chunks - 1, functools.partial(sel_chunk, diagonal=False), init)
    final = sel_chunk(n_chunks - 1, carry, diagonal=True)

    d_w = t_col - (w_start + lax.broadcasted_iota(I32, (1, win_len), 1))
    band = jnp.where((d_w >= 0) & (d_w < WINDOW), 0.0, NEG_INF)
    band4 = jnp.concatenate([band] * Q_PER_KV, axis=0)
    for g in range(N_KV_HEADS):
        q_aug, _, o_cmp = stage[g]
        acc_s = final[g][1]
        o_slc = acc_s[:, :HEAD_DIM] * (1.0 / acc_s[:, HEAD_DIM:])

        s_w = _dot_t(q_aug, kw_refs[g][0, pl.ds(w_start, win_len), :]) + band4
        m_w = jnp.max(s_w, axis=-1, keepdims=True)
        p_w = jnp.exp((s_w - m_w).astype(BF16))
        acc_w = _dot(p_w, vw_refs[g][0, pl.ds(w_start, win_len), :])
        o_win = acc_w[:, :HEAD_DIM] * (1.0 / acc_w[:, HEAD_DIM:])

        for r in range(Q_PER_KV):
            h = g * Q_PER_KV + r
            sl = slice(r * tq, (r + 1) * tq)
            gc = gates[:, h * N_BRANCH:h * N_BRANCH + 1]
            gs = gates[:, h * N_BRANCH + 1:h * N_BRANCH + 2]
            gw = gates[:, h * N_BRANCH + 2:h * N_BRANCH + 3]
            outs.append(gc * o_cmp[sl] + gs * o_slc[sl] + gw * o_win[sl])
    o_ref[...] = jnp.concatenate(outs, axis=-1).astype(BF16)


def _nsa(q, kc, vc, kv3, gates, overlap, blk_onehot, batch, seq):
    T = q.shape[0]
    tq = min(NSA_Q, seq)
    assert NSA_KC % tq == 0 and seq % NSA_KC == 0 and seq >= WINDOW + tq
    nq = seq // tq
    ncp = kc.shape[1]
    n_sel = seq // SEL_BLOCK
    row_map = lambda b, i: (b * nq + i, 0)
    bmap = lambda b, i: (b, 0, 0)
    fixed = lambda b, i: (0, 0)
    kvspec = lambda j: pl.BlockSpec((1, seq, LANES), lambda b, i, j=j: (b, 0, j))
    n_kv = 4 * N_KV_HEADS
    return pl.pallas_call(
        functools.partial(_nsa_kernel, seq=seq, n_sel=n_sel),
        grid=(batch, nq),
        in_specs=[
            pl.BlockSpec((tq, N_HEADS * HEAD_DIM), row_map),
            pl.BlockSpec((1, ncp, KV_DIM), bmap),
            pl.BlockSpec((1, ncp, KV_DIM), bmap),
        ] + [kvspec(2 + j) for j in range(n_kv)] + [
            pl.BlockSpec((tq, LANES), row_map),
            pl.BlockSpec(overlap.shape, fixed),
            pl.BlockSpec(blk_onehot.shape, fixed),
        ],
        out_specs=pl.BlockSpec((tq, N_HEADS * HEAD_DIM), row_map),
        out_shape=jax.ShapeDtypeStruct((T, N_HEADS * HEAD_DIM), BF16),
        compiler_params=_cparams(("arbitrary", "arbitrary")),
        name="nsa",
    )(q, kc, vc, *([kv3] * n_kv), gates, overlap, blk_onehot)


def _layer_norm(y, g, b):
    mu = jnp.mean(y, axis=-1, keepdims=True)
    yc = y - mu
    var = jnp.mean(yc * yc, axis=-1, keepdims=True)
    return yc * lax.rsqrt(var + LN_EPS) * g + b


def _post_kernel(x_ref, conv_ref, nsa_ref, wo_ref, g1_ref, b1_ref, rwh_ref, rwl_ref, rb_ref,
                 wsg_ref, wsu_ref, wsd_ref,
                 x3_ref, base_ref, e_ref, r_ref, w_ref, cnt_ref, carry_ref, *, alpha):
    rows, D = x_ref.shape

    @pl.when(pl.program_id(0) == 0)
    def _():
        carry_ref[...] = jnp.zeros_like(carry_ref)

    half = wo_ref.shape[0] // 2
    mix = _dot(conv_ref[...], wo_ref[:half, :]) + _dot(nsa_ref[...], wo_ref[half:, :])
    x1 = _layer_norm(alpha * x_ref[...] + mix, g1_ref[...], b1_ref[...])
    for s in range(D // LANES):
        x3_ref[pl.ds(s, rows, stride=SUBLANES), :] = x1[:, s * LANES:(s + 1) * LANES]

    xh, xl = _split_bf16(x1)
    hid = jax.nn.silu(_dot(xh, wsg_ref[...])) * _dot(xh, wsu_ref[...])
    base_ref[...] = alpha * x1 + _dot(hid.astype(BF16), wsd_ref[...])

    logits = (_dot_t(rwh_ref[...], xh) + _dot_t(rwh_ref[...], xl) + _dot_t(rwl_ref[...], xh))
    scores = jax.nn.sigmoid(logits)
    reps = rows // LANES
    biased = scores + jnp.concatenate([rb_ref[...]] * reps, axis=1)
    eidx = lax.broadcasted_iota(I32, (N_EXPERTS, rows), 0).astype(F32)
    gidx = lax.broadcasted_iota(I32, (GROUP_SIZE, rows), 0).astype(F32)
    gvals, gscore = [], []
    for gi in range(N_GROUPS):
        v = biased[gi * GROUP_SIZE:(gi + 1) * GROUP_SIZE, :]
        m1 = jnp.max(v, axis=0, keepdims=True)
        i1 = jnp.min(jnp.where(v == m1, gidx, float(GROUP_SIZE)), axis=0, keepdims=True)
        m2 = jnp.max(jnp.where(gidx == i1, -jnp.inf, v), axis=0, keepdims=True)
        gvals.append(v)
        gscore.append(m1 + m2)
    cands = []
    for gi in range(N_GROUPS):
        ahead = jnp.zeros((1, rows), F32)
        for gj in range(N_GROUPS):
            if gj == gi:
                continue
            beats = (gscore[gj] >= gscore[gi]) if gj < gi else (gscore[gj] > gscore[gi])
            ahead = ahead + jnp.where(beats, 1.0, 0.0)
        ahead_full = jnp.broadcast_to(ahead, gvals[gi].shape)
        cands.append(jnp.where(ahead_full < float(TOPK_GROUPS), gvals[gi], NEG_INF))
    cand = jnp.concatenate(cands, axis=0)
    onehot = jnp.zeros((N_EXPERTS, rows), F32)
    idx_rows, w_rows = [], []
    for _ in range(TOP_K):
        mx = jnp.max(cand, axis=0, keepdims=True)
        first = jnp.min(jnp.where(cand == mx, eidx, float(N_EXPERTS)), axis=0, keepdims=True)
        hit = eidx == first
        idx_rows.append(first)
        w_rows.append(jnp.sum(jnp.where(hit, scores, 0.0), axis=0, keepdims=True))
        onehot = jnp.where(hit, 1.0, onehot)
        cand = jnp.where(hit, -jnp.inf, cand)
    wsum = w_rows[0]
    for k in range(1, TOP_K):
        wsum = wsum + w_rows[k]

    ti = lax.broadcasted_iota(I32, (rows, rows), 0)
    tj = lax.broadcasted_iota(I32, (rows, rows), 1)
    earlier = jnp.where(ti < tj, 1.0, 0.0).astype(BF16)
    carry = carry_ref[...]
    before = _dot(onehot.astype(BF16), earlier) + jnp.concatenate([carry] * reps, axis=1)
    krow = lax.broadcasted_iota(I32, (TOP_K, rows), 0)
    e_out = jnp.zeros((TOP_K, rows), F32)
    r_out = jnp.zeros((TOP_K, rows), F32)
    w_out = jnp.zeros((TOP_K, rows), F32)
    for k in range(TOP_K):
        rank = jnp.sum(jnp.where(eidx == idx_rows[k], before, 0.0), axis=0, keepdims=True)
        e_out = jnp.where(krow == k, idx_rows[k], e_out)
        r_out = jnp.where(krow == k, rank, r_out)
        w_out = jnp.where(krow == k, w_rows[k] / wsum * ROUTED_SCALE, w_out)
    e_ref[...] = e_out.astype(I32)
    r_ref[...] = r_out.astype(I32)
    w_ref[...] = w_out
    total = carry + jnp.sum(onehot, axis=1, keepdims=True)
    carry_ref[...] = total
    cnt_ref[...] = total.astype(I32)


def _post(x2, conv_out, nsa_out, w_out, g1, b1, rw_hi, rw_lo, rbias, wsg, wsu, wsd, alpha):
    T, D = x2.shape
    rows = min(POST_ROWS, T)
    row_map = lambda i: (i, 0)
    col_map = lambda i: (0, i)
    fixed = lambda i: (0, 0)
    full = lambda a: pl.BlockSpec(a.shape, fixed)
    return pl.pallas_call(
        functools.partial(_post_kernel, alpha=alpha),
        grid=(T // rows,),
        in_specs=[
            pl.BlockSpec((rows, D), row_map),
            pl.BlockSpec((rows, CONV_CH), row_map),
            pl.BlockSpec((rows, N_HEADS * HEAD_DIM), row_map),
            full(w_out), full(g1), full(b1), full(rw_hi), full(rw_lo), full(rbias),
            full(wsg), full(wsu), full(wsd),
        ],
        out_specs=[
            pl.BlockSpec((rows * SUBLANES, LANES), row_map),
            pl.BlockSpec((rows, D), row_map),
            pl.BlockSpec((TOP_K, rows), col_map),
            pl.BlockSpec((TOP_K, rows), col_map),
            pl.BlockSpec((TOP_K, rows), col_map),
            pl.BlockSpec((N_EXPERTS, LANES), fixed),
        ],
        out_shape=[
            jax.ShapeDtypeStruct((T * SUBLANES, LANES), F32),
            jax.ShapeDtypeStruct((T, D), F32),
            jax.ShapeDtypeStruct((TOP_K, T), I32),
            jax.ShapeDtypeStruct((TOP_K, T), I32),
            jax.ShapeDtypeStruct((TOP_K, T), F32),
            jax.ShapeDtypeStruct((N_EXPERTS, LANES), I32),
        ],
        scratch_shapes=[pltpu.VMEM((N_EXPERTS, LANES), F32)],
        compiler_params=_cparams(("arbitrary",)),
        name="post_attn_router",
    )(x2, conv_out, nsa_out, w_out, g1, b1, rw_hi, rw_lo, rbias, wsg, wsu, wsd)


def _slot_rows_kernel(pstart_ref, e_ref, r_ref, o_ref):
    e = e_ref[...]

    def add_expert(j, acc):
        return acc + jnp.where(e == j, pstart_ref[j], 0)

    o_ref[...] = lax.fori_loop(0, N_EXPERTS, add_expert, r_ref[...]) * SUBLANES


def _slot_rows(pad_start, e_t, r_t):
    T = e_t.shape[1]
    cols = min(2048, T)
    spec = pl.BlockSpec((TOP_K, cols), lambda i, *_: (0, i))
    return pl.pallas_call(
        _slot_rows_kernel,
        grid_spec=pltpu.PrefetchScalarGridSpec(
            num_scalar_prefetch=1, grid=(T // cols,), in_specs=[spec, spec], out_specs=spec),
        out_shape=jax.ShapeDtypeStruct(e_t.shape, I32),
        compiler_params=_cparams(("arbitrary",)),
        name="moe_slot_rows",
    )(pad_start, e_t, r_t)


def _push_kernel(zoff_ref, d_ref, x3_ref, xs_ref, zero_ref, sem, zsem):
    toks = x3_ref.shape[0] // SUBLANES
    zrows = zero_ref.shape[0]

    def zero_copy(e):
        off = pl.multiple_of(zoff_ref[e] * SUBLANES, SUBLANES)
        return pltpu.make_async_copy(zero_ref, xs_ref.at[pl.ds(off, zrows), :], zsem)

    @pl.when(pl.program_id(0) == 0)
    def _():
        zero_ref[...] = jnp.zeros_like(zero_ref)

        def start(e, c):
            @pl.when(zoff_ref[e] >= 0)
            def _():
                zero_copy(e).start()
            return c

        def wait(e, c):
            @pl.when(zoff_ref[e] >= 0)
            def _():
                zero_copy(e).wait()
            return c

        lax.fori_loop(0, N_EXPERTS, start, 0)
        lax.fori_loop(0, N_EXPERTS, wait, 0)

    def push_token(t, c):
        src = x3_ref.at[pl.ds(pl.multiple_of(t * SUBLANES, SUBLANES), SUBLANES), :]
        for k in range(TOP_K):
            row = pl.multiple_of(d_ref[t * TOP_K + k], SUBLANES)
            dst = xs_ref.at[pl.ds(row, SUBLANES), :]
            pltpu.make_async_copy(src, dst, sem).start(priority=k % DMA_PRIORITIES)
        return c

    lax.fori_loop(0, toks, push_token, 0)
    for _ in range(TOP_K):
        pltpu.make_async_copy(x3_ref, xs_ref.at[pl.ds(0, toks * SUBLANES), :], sem).wait()


def _push(zero_off, d_flat, x3, n_slots):
    T = x3.shape[0] // SUBLANES
    toks = min(PUSH_ROWS, T)
    return pl.pallas_call(
        _push_kernel,
        grid_spec=pltpu.PrefetchScalarGridSpec(
            num_scalar_prefetch=1,
            grid=(T // toks,),
            in_specs=[
                pl.BlockSpec((toks * TOP_K,), lambda i, *_: (i,), memory_space=pltpu.SMEM),
                pl.BlockSpec((toks * SUBLANES, LANES), lambda i, *_: (i, 0)),
            ],
            out_specs=pl.BlockSpec(memory_space=pl.ANY),
            scratch_shapes=[
                pltpu.VMEM((SLOT_BLOCK * SUBLANES, LANES), F32),
                pltpu.SemaphoreType.DMA(()),
                pltpu.SemaphoreType.DMA(()),
            ],
        ),
        out_shape=jax.ShapeDtypeStruct((n_slots * SUBLANES, LANES), F32),
        compiler_params=_cparams(("arbitrary",)),
        name="moe_push",
    )(zero_off, d_flat, x3)


def _expert_kernel(blk_e_ref, nused_ref, xs_hbm, wg_ref, wu_ref, wd_ref, ys_hbm,
                   wgu_s, wd_s, xbuf, ybuf, xsem, ysem):
    b = pl.program_id(0)
    n_used = nused_ref[0]
    blk_rows = xbuf.shape[1]
    rows = blk_rows // SUBLANES
    D = wg_ref.shape[1]
    H = wg_ref.shape[2]

    def x_copy(blk, slot):
        src = xs_hbm.at[pl.ds(pl.multiple_of(blk * blk_rows, blk_rows), blk_rows), :]
        return pltpu.make_async_copy(src, xbuf.at[slot], xsem.at[slot])

    def y_copy(blk, slot):
        dst = ys_hbm.at[pl.ds(pl.multiple_of(blk * blk_rows, blk_rows), blk_rows), :]
        return pltpu.make_async_copy(ybuf.at[slot], dst, ysem.at[slot])

    @pl.when(b == 0)
    def _():
        for j in range(EXPERT_IN_BUFS - 1):
            @pl.when(j < n_used)
            def _(j=j):
                x_copy(j, j).start()

    prev = blk_e_ref[jnp.maximum(b - 1, 0)]

    @pl.when((b == 0) | (blk_e_ref[b] != prev))
    def _():
        wgu_s[:, :H] = wg_ref[0].astype(BF16)
        wgu_s[:, H:] = wu_ref[0].astype(BF16)
        wd_s[...] = wd_ref[0].astype(BF16)

    @pl.when(b < n_used)
    def _():
        ahead = b + (EXPERT_IN_BUFS - 1)

        @pl.when(ahead < n_used)
        def _():
            x_copy(ahead, jnp.bitwise_and(ahead, EXPERT_IN_BUFS - 1)).start()

        slot = jnp.bitwise_and(b, EXPERT_IN_BUFS - 1)
        x_copy(b, slot).wait()
        xv = xbuf.at[slot]
        xb = jnp.concatenate(
            [xv[pl.ds(s, rows, stride=SUBLANES), :].astype(BF16) for s in range(D // LANES)],
            axis=-1)
        h = _dot(xb, wgu_s[...])
        act = (jax.nn.silu(h[:, :H]) * h[:, H:]).astype(BF16)
        out = _dot(act, wd_s[...])

        yslot = jnp.bitwise_and(b, EXPERT_OUT_BUFS - 1)

        @pl.when(b >= EXPERT_OUT_BUFS)
        def _():
            y_copy(b - EXPERT_OUT_BUFS, yslot).wait()

        yv = ybuf.at[yslot]
        for s in range(D // LANES):
            yv[pl.ds(s, rows, stride=SUBLANES), :] = out[:, s * LANES:(s + 1) * LANES]
        y_copy(b, yslot).start()

    @pl.when(b == pl.num_programs(0) - 1)
    def _():
        for j in range(EXPERT_OUT_BUFS):
            blk = n_used - 1 - j

            @pl.when(blk >= 0)
            def _(blk=blk):
                y_copy(blk, jnp.bitwise_and(blk, EXPERT_OUT_BUFS - 1)).wait()


def _experts(blk_e, n_used, xs, w_gate, w_up, w_down):
    n_blocks = blk_e.shape[0]
    E, D, H = w_gate.shape
    blk_rows = SLOT_BLOCK * SUBLANES
    return pl.pallas_call(
        _expert_kernel,
        grid_spec=pltpu.PrefetchScalarGridSpec(
            num_scalar_prefetch=2,
            grid=(n_blocks,),
            in_specs=[
                pl.BlockSpec(memory_space=pl.ANY),
                pl.BlockSpec((1, D, H), lambda b, be, nu: (be[b], 0, 0)),
                pl.BlockSpec((1, D, H), lambda b, be, nu: (be[b], 0, 0)),
                pl.BlockSpec((1, H, D), lambda b, be, nu: (be[b], 0, 0)),
            ],
            out_specs=pl.BlockSpec(memory_space=pl.ANY),
            scratch_shapes=[
                pltpu.VMEM((D, 2 * H), BF16), pltpu.VMEM((H, D), BF16),
                pltpu.VMEM((EXPERT_IN_BUFS, blk_rows, LANES), F32),
                pltpu.VMEM((EXPERT_OUT_BUFS, blk_rows, LANES), F32),
                pltpu.SemaphoreType.DMA((EXPERT_IN_BUFS,)),
                pltpu.SemaphoreType.DMA((EXPERT_OUT_BUFS,)),
            ],
        ),
        out_shape=jax.ShapeDtypeStruct(xs.shape, F32),
        compiler_params=_cparams(("arbitrary",)),
        name="moe_experts",
    )(blk_e, n_used, xs, w_gate, w_up, w_down)


def _combine_kernel(d_ref, dn_ref, ys_ref, base_ref, rw_ref, g2_ref,
                    b2_ref, o_ref, buf0, buf1, sem0, sem1):
    toks, D = base_ref.shape
    i = pl.program_id(0)
    last = pl.num_programs(0) - 1

    def issue(dref, buf, sem):
        def gather_token(t, c):
            for k in range(TOP_K):
                row = pl.multiple_of(dref[t * TOP_K + k], SUBLANES)
                src = ys_ref.at[pl.ds(row, SUBLANES), :]
                dst = buf.at[pl.ds(pl.multiple_of((k * toks + t) * SUBLANES, SUBLANES), SUBLANES), :]
                pltpu.make_async_copy(src, dst, sem).start(priority=k % DMA_PRIORITIES)
            return c

        lax.fori_loop(0, toks, gather_token, 0)

    def finish(buf, sem):
        pltpu.make_async_copy(ys_ref.at[pl.ds(0, buf.shape[0]), :], buf, sem).wait()
        w = rw_ref[...]
        pieces = []
        for s in range(D // LANES):
            acc = jnp.zeros((toks, LANES), F32)
            for k in range(TOP_K):
                rows = buf[pl.ds(k * toks * SUBLANES + s, toks, stride=SUBLANES), :]
                acc = acc + w[:, k:k + 1] * rows
            pieces.append(acc)
        y = base_ref[...] + jnp.concatenate(pieces, axis=-1)
        o_ref[...] = _layer_norm(y, g2_ref[...], b2_ref[...])

    @pl.when(i == 0)
    def _():
        issue(d_ref, buf0, sem0)

    for parity, (cur, csem, nxt, nsem) in enumerate(((buf0, sem0, buf1, sem1), (buf1, sem1, buf0, sem0))):
        @pl.when(jnp.bitwise_and(i, 1) == parity)
        def _(cur=cur, csem=csem, nxt=nxt, nsem=nsem):
            @pl.when(i < last)
            def _():
                issue(dn_ref, nxt, nsem)
            finish(cur, csem)


def _combine(d_flat, ys, base, rw, g2, b2):
    T, D = base.shape
    toks = min(COMB_ROWS, T)
    steps = T // toks
    idx_now = pl.BlockSpec((toks * TOP_K,), lambda i: (i,), memory_space=pltpu.SMEM)
    idx_next = pl.BlockSpec((toks * TOP_K,), lambda i: (jnp.minimum(i + 1, steps - 1),),
                            memory_space=pltpu.SMEM)
    return pl.pallas_call(
        _combine_kernel,
        grid_spec=pltpu.PrefetchScalarGridSpec(
            num_scalar_prefetch=0,
            grid=(steps,),
            in_specs=[
                idx_now, idx_next,
                pl.BlockSpec(memory_space=pl.ANY),
                pl.BlockSpec((toks, D), lambda i: (i, 0)),
                pl.BlockSpec((toks, TOP_K), lambda i: (i, 0)),
                pl.BlockSpec(g2.shape, lambda i: (0, 0)),
                pl.BlockSpec(b2.shape, lambda i: (0, 0)),
            ],
            out_specs=pl.BlockSpec((toks, D), lambda i: (i, 0)),
            scratch_shapes=[
                pltpu.VMEM((TOP_K * toks * SUBLANES, LANES), F32),
                pltpu.VMEM((TOP_K * toks * SUBLANES, LANES), F32),
                pltpu.SemaphoreType.DMA(()),
                pltpu.SemaphoreType.DMA(()),
            ],
        ),
        out_shape=jax.ShapeDtypeStruct((T, D), F32),
        compiler_params=_cparams(("arbitrary",)),
        name="moe_combine",
    )(d_flat, d_flat, ys, base, rw, g2, b2)


def _overlap_matrix(ncp):
    n = np.arange(ncp)[:, None]
    j = np.arange(LANES)[None, :]
    start = n * CMP_STRIDE
    end = start + CMP_LEN - 1
    sel_start = j * SEL_BLOCK
    ovl = (start < sel_start + SEL_BLOCK) & (end >= sel_start)
    return jnp.asarray(ovl.astype(np.float32), dtype=BF16)


def _block_onehot(seq):
    pos = np.arange(seq)[:, None]
    j = np.arange(LANES)[None, :]
    return jnp.asarray((pos // SEL_BLOCK == j).astype(np.float32), dtype=BF16)


def _mixer(x2, batch, seq, w_in, conv_w, cmp_k, cmp_v):
    c3 = 3 * CONV_CH
    qd = N_HEADS * HEAD_DIM
    w_conv = w_in[:, :c3].astype(BF16)
    w_q = w_in[:, c3:c3 + qd].astype(BF16)
    kv0 = c3 + qd
    part = lambda j: w_in[:, kv0 + j * KV_DIM:kv0 + (j + 1) * KV_DIM]
    zero = jnp.zeros((w_in.shape[0], HEAD_DIM), w_in.dtype)

    def per_group(w):
        return [c for g in range(N_KV_HEADS) for c in (w[:, g * HEAD_DIM:(g + 1) * HEAD_DIM], zero)]

    w_kv = jnp.concatenate([part(0), part(1)] + per_group(part(2)) + per_group(part(4))
                           + per_group(part(3)) + per_group(part(5)), axis=1).astype(BF16)
    w_g = jnp.pad(w_in[:, kv0 + 6 * KV_DIM:], ((0, 0), (0, LANES - N_HEADS * N_BRANCH))).astype(BF16)
    conv_out, q, kv, gates = _proj_conv(x2, w_conv, w_q, w_kv, w_g, conv_w, batch, seq)
    kc, vc = _compress(kv[:, :KV_DIM], kv[:, KV_DIM:2 * KV_DIM], cmp_k, cmp_v, batch, seq)
    ncp = -(-kc.shape[1] // LANES) * LANES
    if ncp != kc.shape[1]:
        padn = ((0, 0), (0, ncp - kc.shape[1]), (0, 0))
        kc, vc = jnp.pad(kc, padn), jnp.pad(vc, padn)
    kv3 = kv.reshape(batch, seq, kv.shape[1])
    nsa_out = _nsa(q, kc, vc, kv3, gates, _overlap_matrix(ncp), _block_onehot(seq), batch, seq)
    return conv_out, nsa_out


def _moe(x3, base, e_t, r_t, w_t, counts, w_gate, w_up, w_down, g2, b2):
    T = base.shape[0]
    A = T * TOP_K
    n_blocks = -(-(A + N_EXPERTS * (SLOT_BLOCK - 1)) // SLOT_BLOCK)
    cnt = counts[:, 0]
    padded = (cnt + SLOT_BLOCK - 1) // SLOT_BLOCK * SLOT_BLOCK
    pad_end = jnp.cumsum(padded)
    pad_start = (pad_end - padded).astype(I32)
    zero_off = jnp.where(padded > 0, pad_end - SLOT_BLOCK, -1).astype(I32)
    n_used = (pad_end[-1:] // SLOT_BLOCK).astype(I32)
    blk_start = jnp.arange(n_blocks, dtype=I32) * SLOT_BLOCK
    last_e = jnp.max(jnp.where(padded > 0, jnp.arange(N_EXPERTS, dtype=I32), 0))
    blk_e = jnp.minimum(jnp.sum((pad_end[None, :] <= blk_start[:, None]).astype(I32), axis=1),
                        last_e).astype(I32)
    d_flat = _slot_rows(pad_start, e_t, r_t).T.reshape(A)
    xs = _push(zero_off, d_flat, x3, n_blocks * SLOT_BLOCK)
    ys = _experts(blk_e, n_used, xs, w_gate, w_up, w_down)
    return _combine(d_flat, ys, base, w_t.T, g2, b2)


def kernel(x, w_in, conv_w, ck_pos, ck_w1, ck_b1, ck_w2, cv_pos, cv_w1, cv_b1, cv_w2, w_out, ln1_g, ln1_b, router_w, router_bias, w_gate, w_up, w_down, ws_gate, ws_up, ws_down, ln2_g, ln2_b):
    batch, seq, D = x.shape
    depth = w_in.shape[0]
    alpha = (2.0 * depth) ** 0.25
    x2 = x.reshape(batch * seq, D)
    for l in range(depth):
        cmp_k = _compress_weights(ck_pos[l], ck_w1[l], ck_b1[l], ck_w2[l])
        cmp_v = _compress_weights(cv_pos[l], cv_w1[l], cv_b1[l], cv_w2[l])
        conv_out, nsa_out = _mixer(x2, batch, seq, w_in[l], conv_w[l], cmp_k, cmp_v)
        rw_hi, rw_lo = _split_bf16(router_w[l].T)
        rbias = jnp.broadcast_to(router_bias[l][:, None], (N_EXPERTS, LANES))
        x3, base, e_t, r_t, w_t, counts = _post(
            x2, conv_out, nsa_out, w_out[l].astype(BF16), ln1_g[l][None, :], ln1_b[l][None, :],
            rw_hi, rw_lo, rbias,
            ws_gate[l].astype(BF16), ws_up[l].astype(BF16), ws_down[l].astype(BF16), alpha)
        x2 = _moe(x3, base, e_t, r_t, w_t, counts, w_gate[l], w_up[l], w_down[l],
                  ln2_g[l][None, :], ln2_b[l][None, :])
    return x2.reshape(batch, seq, D)
```

```python
import functools
import math

import jax
import jax.numpy as jnp
import numpy as np
from jax import lax
from jax.experimental import pallas as pl
from jax.experimental.pallas import tpu as pltpu

F32 = jnp.float32
BF16 = jnp.bfloat16
I32 = jnp.int32

CONV_CH = 512
CONV_WIDTH = 3
N_HEADS = 8
HEAD_DIM = 64
N_KV_HEADS = 2
Q_PER_KV = N_HEADS // N_KV_HEADS
KV_DIM = N_KV_HEADS * HEAD_DIM
N_BRANCH = 3
CMP_LEN = 32
CMP_STRIDE = 16
SEL_BLOCK = 64
SEL_TOPK = 8
WINDOW = 512
FORCED_SCORE = 1e4
N_EXPERTS = 256
TOP_K = 8
N_GROUPS = 8
TOPK_GROUPS = 4
GROUP_SIZE = N_EXPERTS // N_GROUPS
ROUTED_SCALE = 2.5
LN_EPS = 1e-5
NEG_INF = -1e30
SEL_SHIFT = SEL_BLOCK.bit_length() - 1
GROUP_SHIFT = GROUP_SIZE.bit_length() - 1
TOPK_SHIFT = TOP_K.bit_length() - 1

LANES = 128
SUBLANES = 8
VMEM_LIMIT = 56 * 1024 * 1024
DMA_PRIORITIES = 2

PROJ_ROWS = 1024
NSA_Q = 512
NSA_WIN_Q = 128
NSA_KC = 512
POST_ROWS = 512
SLOT_BLOCK = 512
EXPERT_IN_BUFS = 4
EXPERT_OUT_BUFS = 2
PUSH_ROWS = 1024
COMB_ROWS = 256


def _dot(a, b):
    return jnp.dot(a, b, preferred_element_type=F32)


def _dot_t(a, b):
    return lax.dot_general(a, b, (((1,), (1,)), ((), ())), preferred_element_type=F32)


def _split_bf16(x):
    hi = x.astype(BF16)
    lo = (x - hi.astype(F32)).astype(BF16)
    return hi, lo


def _cparams(sem):
    return pltpu.CompilerParams(dimension_semantics=sem, vmem_limit_bytes=VMEM_LIMIT)


def _proj_conv_kernel(x_ref, wc_ref, wq_ref, wkv_ref, wg_ref, cw_ref,
                      conv_ref, q_ref, kv_ref, gate_ref, carry_ref):
    rows = x_ref.shape[0]

    @pl.when(pl.program_id(1) == 0)
    def _():
        carry_ref[...] = jnp.zeros_like(carry_ref)

    xb = x_ref[...].astype(BF16)
    acc = _dot(xb, wc_ref[...])
    b_g = acc[:, :CONV_CH]
    u = acc[:, CONV_CH:2 * CONV_CH] * acc[:, 2 * CONV_CH:]
    prev2 = carry_ref[SUBLANES - 2:SUBLANES - 1, :]
    prev1 = carry_ref[SUBLANES - 1:SUBLANES, :]
    ri = lax.broadcasted_iota(I32, (rows, 1), 0)
    u1 = jnp.where(ri == 0, prev1, pltpu.roll(u, 1, 0))
    u2 = jnp.where(ri == 0, prev2, jnp.where(ri == 1, prev1, pltpu.roll(u, 2, 0)))
    y = cw_ref[0:1, :] * u2 + cw_ref[1:2, :] * u1 + cw_ref[2:3, :] * u
    conv_ref[...] = (b_g * y).astype(BF16)
    carry_ref[...] = u[rows - SUBLANES:, :]

    q_ref[...] = (_dot(xb, wq_ref[...]) * (HEAD_DIM ** -0.5)).astype(BF16)
    gate_ref[...] = jax.nn.sigmoid(_dot(xb, wg_ref[...]))

    kv = _dot(xb, wkv_ref[...])
    n_plain, n_key = 2, 2 * N_KV_HEADS
    key_w = n_key * LANES
    pos = pl.program_id(1) * rows + lax.broadcasted_iota(I32, (rows, key_w), 0)
    l128 = jnp.bitwise_and(lax.broadcasted_iota(I32, (rows, key_w), 1), LANES - 1)
    feat = jnp.where(l128 == HEAD_DIM, jnp.right_shift(pos, SEL_SHIFT),
                     jnp.where(l128 == HEAD_DIM + 1, jnp.bitwise_and(pos, SEL_BLOCK - 1), 0))
    ones = jnp.where(l128 >= HEAD_DIM, 1.0, 0.0)
    k0, v0 = n_plain * LANES, n_plain * LANES + key_w
    kv_ref[:, :k0] = kv[:, :k0].astype(BF16)
    kv_ref[:, k0:v0] = (kv[:, k0:v0] + feat.astype(F32)).astype(BF16)
    kv_ref[:, v0:] = (kv[:, v0:] + ones).astype(BF16)


def _proj_conv(x2, w_conv, w_q, w_kv, w_g, conv_w, batch, seq):
    T, D = x2.shape
    rows = min(PROJ_ROWS, seq)
    nt = seq // rows
    row_map = lambda b, i: (b * nt + i, 0)
    fixed = lambda b, i: (0, 0)
    return pl.pallas_call(
        _proj_conv_kernel,
        grid=(batch, nt),
        in_specs=[
            pl.BlockSpec((rows, D), row_map),
            pl.BlockSpec(w_conv.shape, fixed),
            pl.BlockSpec(w_q.shape, fixed),
            pl.BlockSpec(w_kv.shape, fixed),
            pl.BlockSpec(w_g.shape, fixed),
            pl.BlockSpec(conv_w.shape, fixed),
        ],
        out_specs=[
            pl.BlockSpec((rows, CONV_CH), row_map),
            pl.BlockSpec((rows, N_HEADS * HEAD_DIM), row_map),
            pl.BlockSpec((rows, w_kv.shape[1]), row_map),
            pl.BlockSpec((rows, LANES), row_map),
        ],
        out_shape=[
            jax.ShapeDtypeStruct((T, CONV_CH), BF16),
            jax.ShapeDtypeStruct((T, N_HEADS * HEAD_DIM), BF16),
            jax.ShapeDtypeStruct((T, w_kv.shape[1]), BF16),
            jax.ShapeDtypeStruct((T, LANES), F32),
        ],
        scratch_shapes=[pltpu.VMEM((SUBLANES, CONV_CH), F32)],
        compiler_params=_cparams(("arbitrary", "arbitrary")),
        name="proj_conv",
    )(x2, w_conv, w_q, w_kv, w_g, conv_w)


def _compress_kernel(ck_ref, cv_ref, wtk_ref, wbk_ref, w2k_ref, ptk_ref, pbk_ref, b1k_ref,
                     wtv_ref, wbv_ref, w2v_ref, ptv_ref, pbv_ref, b1v_ref, kc_ref, vc_ref):
    def one(c_ref, wt_ref, wb_ref, w2_ref, pt_ref, pb_ref, b1_ref, o_ref):
        c = c_ref[0]
        top = _dot(c, wt_ref[...])
        bot = _dot(c, wb_ref[...])
        c0 = _dot(pt_ref[...], wt_ref[...]) + _dot(pb_ref[...], wb_ref[...]) + b1_ref[...]
        n = top.shape[0]
        h = top + pltpu.roll(bot, n - 1, 0) + c0[0:1, :]
        g = jax.nn.gelu(h, approximate=True)
        o_ref[0] = _dot(g.astype(BF16), w2_ref[...]).astype(BF16)

    one(ck_ref, wtk_ref, wbk_ref, w2k_ref, ptk_ref, pbk_ref, b1k_ref, kc_ref)
    one(cv_ref, wtv_ref, wbv_ref, w2v_ref, ptv_ref, pbv_ref, b1v_ref, vc_ref)


def _blockdiag2(w):
    z = jnp.zeros_like(w)
    return jnp.concatenate([jnp.concatenate([w, z], 1), jnp.concatenate([z, w], 1)], 0)


def _compress_weights(pos, w1, b1, w2):
    w1r = w1.reshape(CMP_LEN, HEAD_DIM, HEAD_DIM)
    eye = jnp.eye(N_KV_HEADS, dtype=w1.dtype)
    wfull = (w1r[:, None, :, None, :] * eye[None, :, None, :, None]).reshape(CMP_LEN, KV_DIM, KV_DIM)
    w_top = wfull[:CMP_STRIDE].reshape(CMP_STRIDE * KV_DIM, KV_DIM).astype(BF16)
    w_bot = wfull[CMP_STRIDE:].reshape(CMP_STRIDE * KV_DIM, KV_DIM).astype(BF16)
    posr = jnp.tile(pos, (1, N_KV_HEADS))
    pos_top = jnp.tile(posr[:CMP_STRIDE].reshape(1, -1), (SUBLANES, 1)).astype(BF16)
    pos_bot = jnp.tile(posr[CMP_STRIDE:].reshape(1, -1), (SUBLANES, 1)).astype(BF16)
    b1r = jnp.tile(b1[None, :], (SUBLANES, N_KV_HEADS)).astype(F32)
    return w_top, w_bot, _blockdiag2(w2).astype(BF16), pos_top, pos_bot, b1r


def _compress(kc_raw, vc_raw, wk, wv, batch, seq):
    chunks = seq // CMP_STRIDE
    width = CMP_STRIDE * KV_DIM
    ck = kc_raw.reshape(batch, chunks, width)
    cv = vc_raw.reshape(batch, chunks, width)
    bmap = lambda b: (b, 0, 0)
    fixed = lambda b: (0, 0)
    wspecs = [pl.BlockSpec(w.shape, fixed) for w in wk]
    return pl.pallas_call(
        _compress_kernel,
        grid=(batch,),
        in_specs=[pl.BlockSpec((1, chunks, width), bmap), pl.BlockSpec((1, chunks, width), bmap)]
        + wspecs + wspecs,
        out_specs=[pl.BlockSpec((1, chunks, KV_DIM), bmap)] * 2,
        out_shape=[jax.ShapeDtypeStruct((batch, chunks, KV_DIM), BF16)] * 2,
        compiler_params=_cparams(("arbitrary",)),
        name="compress",
    )(ck, cv, *wk, *wv)


def _softmax_rows(s, valid):
    s = jnp.where(valid, s, NEG_INF)
    m = jnp.max(s, axis=-1, keepdims=True)
    p = jnp.where(valid, jnp.exp(s - m), 0.0)
    l = jnp.sum(p, axis=-1, keepdims=True)
    inv = jnp.where(l > 0.0, 1.0 / l, 0.0)
    return p, inv


def _nsa_kernel(q_ref, kc_ref, vc_ref, ks0_ref, ks1_ref, kw0_ref, kw1_ref, vs0_ref, vs1_ref,
                vw0_ref, vw1_ref, gate_ref, ovl_ref, hot_ref, o_ref, *, seq, n_sel):
    ks_refs, kw_refs = (ks0_ref, ks1_ref), (kw0_ref, kw1_ref)
    vs_refs, vw_refs = (vs0_ref, vs1_ref), (vw0_ref, vw1_ref)
    tq = q_ref.shape[0]
    ncp = kc_ref.shape[1]
    rows = Q_PER_KV * tq
    q0 = pl.program_id(1) * tq
    t_col = q0 + lax.broadcasted_iota(I32, (tq, 1), 0)
    t4 = jnp.concatenate([t_col] * Q_PER_KV, axis=0)
    row_i = lax.broadcasted_iota(I32, (rows, 1), 0)
    lane = lax.broadcasted_iota(I32, (1, LANES), 1)
    feat_lane = lax.broadcasted_iota(I32, (1, HEAD_DIM), 1)
    n_sel_pad = -(-n_sel // SUBLANES) * SUBLANES
    blk_row = lax.broadcasted_iota(I32, (n_sel_pad, tq), 0)
    blk_row_f = blk_row.astype(F32)
    gates = gate_ref[...]
    n_chunks = (q0 + tq + NSA_KC - 1) // NSA_KC
    outs, stage = [], []
    for g in range(N_KV_HEADS):
        lo, hi = g * HEAD_DIM, (g + 1) * HEAD_DIM
        qg = jnp.concatenate(
            [q_ref[:, (g * Q_PER_KV + r) * HEAD_DIM:(g * Q_PER_KV + r + 1) * HEAD_DIM]
             for r in range(Q_PER_KV)], axis=0)
        slope = jnp.zeros((rows, 1), F32)
        for r in range(Q_PER_KV):
            h = g * Q_PER_KV + r
            in_head = (row_i >= r * tq) & (row_i < (r + 1) * tq)
            slope = jnp.where(in_head, 2.0 ** (-8.0 * (h + 1) / N_HEADS), slope)

        cmp_end = lax.broadcasted_iota(I32, (1, ncp), 1) * CMP_STRIDE + (CMP_LEN - 1)
        d_c = t4 - cmp_end
        s_c = _dot_t(qg, kc_ref[0, :, lo:hi]) - slope * d_c.astype(F32)
        p_c, inv_c = _softmax_rows(s_c, d_c >= 0)
        p_c = p_c * inv_c
        o_cmp = _dot(p_c.astype(BF16), vc_ref[0, :, lo:hi])

        ps = p_c[0:tq]
        for r in range(1, Q_PER_KV):
            ps = ps + p_c[r * tq:(r + 1) * tq]
        ps_hi, ps_lo = _split_bf16(ps)
        imp = _dot(ps_hi, ovl_ref[...]) + _dot(ps_lo, ovl_ref[...])
        forced = (lane == 0) | (lane == jnp.right_shift(t_col, SEL_SHIFT))
        causal = lane * SEL_BLOCK <= t_col
        score = jnp.where(forced, FORCED_SCORE, jnp.where(causal, imp, -1.0))
        score_t = score.T[:n_sel_pad, :]
        score_t = jnp.where(blk_row < n_sel, score_t, -jnp.inf)
        sel_t = jnp.zeros((n_sel_pad, tq), F32)
        for _ in range(min(SEL_TOPK, n_sel)):
            mx = jnp.max(score_t, axis=0, keepdims=True)
            first = jnp.min(jnp.where(score_t == mx, blk_row_f, float(LANES)), axis=0, keepdims=True)
            hit = blk_row_f == first
            sel_t = jnp.where(hit, 1.0, sel_t)
            score_t = jnp.where(hit, -jnp.inf, score_t)
        unsel_t = jnp.concatenate(
            [jnp.where(sel_t > 0.5, 0.0, NEG_INF), jnp.full((LANES - n_sel_pad, tq), NEG_INF, F32)], axis=0)
        unsel_b = unsel_t.T.astype(BF16)

        q_feat = jnp.where(feat_lane == 0, slope * float(SEL_BLOCK),
                           jnp.where(feat_lane == 1, slope, 0.0)).astype(BF16)
        q_aug = jnp.concatenate([qg, q_feat], axis=1)
        lhs = jnp.concatenate([q_aug, jnp.concatenate([unsel_b] * Q_PER_KV, axis=0)], axis=1)

        stage.append((q_aug, lhs, o_cmp))

    def sel_chunk(c, carry, diagonal):
        k0 = pl.multiple_of(c * NSA_KC, NSA_KC)
        hot = hot_ref[pl.ds(k0, NSA_KC), :]
        new = []
        for g in range(N_KV_HEADS):
            m, acc = carry[g]
            kch = ks_refs[g][0, pl.ds(k0, NSA_KC), :]
            s = _dot_t(stage[g][1], jnp.concatenate([kch, hot], axis=1))
            if diagonal:
                pos = k0 + lax.broadcasted_iota(I32, (1, NSA_KC), 1)
                s = jnp.where(pos <= t4, s, NEG_INF)
            m_new = jnp.maximum(m, jnp.max(s, axis=-1, keepdims=True))
            a = jnp.exp(m - m_new)
            p = jnp.exp((s - m_new).astype(BF16))
            acc = a * acc + _dot(p, vs_refs[g][0, pl.ds(k0, NSA_KC), :])
            new.append((m_new, acc))
        return tuple(new)

    init = tuple((jnp.full((rows, 1), NEG_INF, F32), jnp.zeros((rows, 2 * HEAD_DIM), F32))
                 for _ in range(N_KV_HEADS))
    carry = lax.fori_loop(0, n_chunks - 1, functools.partial(sel_chunk, diagonal=False), init)
    wq = min(NSA_WIN_Q, tq)

    def head_rows(a, u):
        return jnp.concatenate([a[r * tq + u * wq:r * tq + (u + 1) * wq] for r in range(Q_PER_KV)], axis=0)

    final = sel_chunk(n_chunks - 1, carry, diagonal=True)

    win_len = WINDOW + wq
    win_out = [[[] for _ in range(Q_PER_KV)] for _ in range(N_KV_HEADS)]
    for u in range(tq // wq):
        qs = q0 + u * wq
        w_start = pl.multiple_of(jnp.maximum(qs - WINDOW, 0), wq)
        d_w = (qs + lax.broadcasted_iota(I32, (wq, 1), 0)
               - (w_start + lax.broadcasted_iota(I32, (1, win_len), 1)))
        band = jnp.where((d_w >= 0) & (d_w < WINDOW), 0.0, NEG_INF)
        band4 = jnp.concatenate([band] * Q_PER_KV, axis=0)
        for g in range(N_KV_HEADS):
            s_w = _dot_t(head_rows(stage[g][0], u), kw_refs[g][0, pl.ds(w_start, win_len), :]) + band4
            m_w = jnp.max(s_w, axis=-1, keepdims=True)
            p_w = jnp.exp((s_w - m_w).astype(BF16))
            acc_w = _dot(p_w, vw_refs[g][0, pl.ds(w_start, win_len), :])
            o_sub = acc_w[:, :HEAD_DIM] * (1.0 / acc_w[:, HEAD_DIM:])
            for r in range(Q_PER_KV):
                win_out[g][r].append(o_sub[r * wq:(r + 1) * wq])

    for g in range(N_KV_HEADS):
        o_cmp = stage[g][2]
        acc_s = final[g][1]
        o_slc = acc_s[:, :HEAD_DIM] * (1.0 / acc_s[:, HEAD_DIM:])
        for r in range(Q_PER_KV):
            h = g * Q_PER_KV + r
            sl = slice(r * tq, (r + 1) * tq)
            gc = gates[:, h * N_BRANCH:h * N_BRANCH + 1]
            gs = gates[:, h * N_BRANCH + 1:h * N_BRANCH + 2]
            gw = gates[:, h * N_BRANCH + 2:h * N_BRANCH + 3]
            o_win = jnp.concatenate(win_out[g][r], axis=0)
            outs.append(gc * o_cmp[sl] + gs * o_slc[sl] + gw * o_win)
    o_ref[...] = jnp.concatenate(outs, axis=-1).astype(BF16)


def _nsa(q, kc, vc, kv3, gates, overlap, blk_onehot, batch, seq):
    T = q.shape[0]
    tq = min(NSA_Q, seq)
    assert NSA_KC % tq == 0 and seq % NSA_KC == 0 and tq % min(NSA_WIN_Q, tq) == 0
    assert seq >= WINDOW + min(NSA_WIN_Q, tq)
    nq = seq // tq
    ncp = kc.shape[1]
    n_sel = seq // SEL_BLOCK
    row_map = lambda b, i: (b * nq + i, 0)
    bmap = lambda b, i: (b, 0, 0)
    fixed = lambda b, i: (0, 0)
    kvspec = lambda j: pl.BlockSpec((1, seq, LANES), lambda b, i, j=j: (b, 0, j))
    n_kv = 4 * N_KV_HEADS
    return pl.pallas_call(
        functools.partial(_nsa_kernel, seq=seq, n_sel=n_sel),
        grid=(batch, nq),
        in_specs=[
            pl.BlockSpec((tq, N_HEADS * HEAD_DIM), row_map),
            pl.BlockSpec((1, ncp, KV_DIM), bmap),
            pl.BlockSpec((1, ncp, KV_DIM), bmap),
        ] + [kvspec(2 + j) for j in range(n_kv)] + [
            pl.BlockSpec((tq, LANES), row_map),
            pl.BlockSpec(overlap.shape, fixed),
            pl.BlockSpec(blk_onehot.shape, fixed),
        ],
        out_specs=pl.BlockSpec((tq, N_HEADS * HEAD_DIM), row_map),
        out_shape=jax.ShapeDtypeStruct((T, N_HEADS * HEAD_DIM), BF16),
        compiler_params=_cparams(("arbitrary", "arbitrary")),
        name="nsa",
    )(q, kc, vc, *([kv3] * n_kv), gates, overlap, blk_onehot)


def _layer_norm(y, g, b):
    mu = jnp.mean(y, axis=-1, keepdims=True)
    yc = y - mu
    var = jnp.mean(yc * yc, axis=-1, keepdims=True)
    return yc * lax.rsqrt(var + LN_EPS) * g + b


def _post_kernel(x_ref, conv_ref, nsa_ref, wo_ref, g1_ref, b1_ref, rwh_ref, rwl_ref, rb_ref,
                 wsg_ref, wsu_ref, wsd_ref,
                 x3_ref, base_ref, e_ref, r_ref, w_ref, cnt_ref, carry_ref, *, alpha):
    rows, D = x_ref.shape

    @pl.when(pl.program_id(0) == 0)
    def _():
        carry_ref[...] = jnp.zeros_like(carry_ref)

    half = wo_ref.shape[0] // 2
    mix = _dot(conv_ref[...], wo_ref[:half, :]) + _dot(nsa_ref[...], wo_ref[half:, :])
    x1 = _layer_norm(alpha * x_ref[...] + mix, g1_ref[...], b1_ref[...])
    for s in range(D // LANES):
        x3_ref[pl.ds(s, rows, stride=SUBLANES), :] = x1[:, s * LANES:(s + 1) * LANES]

    xh, xl = _split_bf16(x1)
    hid = jax.nn.silu(_dot(xh, wsg_ref[...])) * _dot(xh, wsu_ref[...])
    base_ref[...] = alpha * x1 + _dot(hid.astype(BF16), wsd_ref[...])

    logits = (_dot_t(rwh_ref[...], xh) + _dot_t(rwh_ref[...], xl) + _dot_t(rwl_ref[...], xh))
    scores = jax.nn.sigmoid(logits)
    reps = rows // LANES
    biased = scores + jnp.concatenate([rb_ref[...]] * reps, axis=1)
    eidx = lax.broadcasted_iota(I32, (N_EXPERTS, rows), 0).astype(F32)
    gidx = lax.broadcasted_iota(I32, (GROUP_SIZE, rows), 0).astype(F32)
    gvals, gscore = [], []
    for gi in range(N_GROUPS):
        v = biased[gi * GROUP_SIZE:(gi + 1) * GROUP_SIZE, :]
        m1 = jnp.max(v, axis=0, keepdims=True)
        i1 = jnp.min(jnp.where(v == m1, gidx, float(GROUP_SIZE)), axis=0, keepdims=True)
        m2 = jnp.max(jnp.where(gidx == i1, -jnp.inf, v), axis=0, keepdims=True)
        gvals.append(v)
        gscore.append(m1 + m2)
    cands = []
    for gi in range(N_GROUPS):
        ahead = jnp.zeros((1, rows), F32)
        for gj in range(N_GROUPS):
            if gj == gi:
                continue
            beats = (gscore[gj] >= gscore[gi]) if gj < gi else (gscore[gj] > gscore[gi])
            ahead = ahead + jnp.where(beats, 1.0, 0.0)
        ahead_full = jnp.broadcast_to(ahead, gvals[gi].shape)
        cands.append(jnp.where(ahead_full < float(TOPK_GROUPS), gvals[gi], NEG_INF))
    cand = jnp.concatenate(cands, axis=0)
    onehot = jnp.zeros((N_EXPERTS, rows), F32)
    idx_rows, w_rows = [], []
    for _ in range(TOP_K):
        mx = jnp.max(cand, axis=0, keepdims=True)
        first = jnp.min(jnp.where(cand == mx, eidx, float(N_EXPERTS)), axis=0, keepdims=True)
        hit = eidx == first
        idx_rows.append(first)
        w_rows.append(jnp.sum(jnp.where(hit, scores, 0.0), axis=0, keepdims=True))
        onehot = jnp.where(hit, 1.0, onehot)
        cand = jnp.where(hit, -jnp.inf, cand)
    wsum = w_rows[0]
    for k in range(1, TOP_K):
        wsum = wsum + w_rows[k]

    ti = lax.broadcasted_iota(I32, (rows, rows), 0)
    tj = lax.broadcasted_iota(I32, (rows, rows), 1)
    earlier = jnp.where(ti < tj, 1.0, 0.0).astype(BF16)
    carry = carry_ref[...]
    before = _dot(onehot.astype(BF16), earlier) + jnp.concatenate([carry] * reps, axis=1)
    krow = lax.broadcasted_iota(I32, (TOP_K, rows), 0)
    e_out = jnp.zeros((TOP_K, rows), F32)
    r_out = jnp.zeros((TOP_K, rows), F32)
    w_out = jnp.zeros((TOP_K, rows), F32)
    for k in range(TOP_K):
        rank = jnp.sum(jnp.where(eidx == idx_rows[k], before, 0.0), axis=0, keepdims=True)
        e_out = jnp.where(krow == k, idx_rows[k], e_out)
        r_out = jnp.where(krow == k, rank, r_out)
        w_out = jnp.where(krow == k, w_rows[k] / wsum * ROUTED_SCALE, w_out)
    e_ref[...] = e_out.astype(I32)
    r_ref[...] = r_out.astype(I32)
    w_ref[...] = w_out
    total = carry + jnp.sum(onehot, axis=1, keepdims=True)
    carry_ref[...] = total
    cnt_ref[...] = total.astype(I32)


def _post(x2, conv_out, nsa_out, w_out, g1, b1, rw_hi, rw_lo, rbias, wsg, wsu, wsd, alpha):
    T, D = x2.shape
    rows = min(POST_ROWS, T)
    row_map = lambda i: (i, 0)
    col_map = lambda i: (0, i)
    fixed = lambda i: (0, 0)
    full = lambda a: pl.BlockSpec(a.shape, fixed)
    return pl.pallas_call(
        functools.partial(_post_kernel, alpha=alpha),
        grid=(T // rows,),
        in_specs=[
            pl.BlockSpec((rows, D), row_map),
            pl.BlockSpec((rows, CONV_CH), row_map),
            pl.BlockSpec((rows, N_HEADS * HEAD_DIM), row_map),
            full(w_out), full(g1), full(b1), full(rw_hi), full(rw_lo), full(rbias),
            full(wsg), full(wsu), full(wsd),
        ],
        out_specs=[
            pl.BlockSpec((rows * SUBLANES, LANES), row_map),
            pl.BlockSpec((rows, D), row_map),
            pl.BlockSpec((TOP_K, rows), col_map),
            pl.BlockSpec((TOP_K, rows), col_map),
            pl.BlockSpec((TOP_K, rows), col_map),
            pl.BlockSpec((N_EXPERTS, LANES), fixed),
        ],
        out_shape=[
            jax.ShapeDtypeStruct((T * SUBLANES, LANES), F32),
            jax.ShapeDtypeStruct((T, D), F32),
            jax.ShapeDtypeStruct((TOP_K, T), I32),
            jax.ShapeDtypeStruct((TOP_K, T), I32),
            jax.ShapeDtypeStruct((TOP_K, T), F32),
            jax.ShapeDtypeStruct((N_EXPERTS, LANES), I32),
        ],
        scratch_shapes=[pltpu.VMEM((N_EXPERTS, LANES), F32)],
        compiler_params=_cparams(("arbitrary",)),
        name="post_attn_router",
    )(x2, conv_out, nsa_out, w_out, g1, b1, rw_hi, rw_lo, rbias, wsg, wsu, wsd)


def _slot_rows_kernel(pstart_ref, e_ref, r_ref, o_ref):
    e = e_ref[...]

    def add_expert(j, acc):
        return acc + jnp.where(e == j, pstart_ref[j], 0)

    o_ref[...] = lax.fori_loop(0, N_EXPERTS, add_expert, r_ref[...]) * SUBLANES


def _slot_rows(pad_start, e_t, r_t):
    T = e_t.shape[1]
    cols = min(2048, T)
    spec = pl.BlockSpec((TOP_K, cols), lambda i, *_: (0, i))
    return pl.pallas_call(
        _slot_rows_kernel,
        grid_spec=pltpu.PrefetchScalarGridSpec(
            num_scalar_prefetch=1, grid=(T // cols,), in_specs=[spec, spec], out_specs=spec),
        out_shape=jax.ShapeDtypeStruct(e_t.shape, I32),
        compiler_params=_cparams(("arbitrary",)),
        name="moe_slot_rows",
    )(pad_start, e_t, r_t)


def _push_kernel(zoff_ref, d_ref, x3_ref, xs_ref, zero_ref, sem, zsem):
    toks = x3_ref.shape[0] // SUBLANES
    zrows = zero_ref.shape[0]

    def zero_copy(e):
        off = pl.multiple_of(zoff_ref[e] * SUBLANES, SUBLANES)
        return pltpu.make_async_copy(zero_ref, xs_ref.at[pl.ds(off, zrows), :], zsem)

    @pl.when(pl.program_id(0) == 0)
    def _():
        zero_ref[...] = jnp.zeros_like(zero_ref)

        def start(e, c):
            @pl.when(zoff_ref[e] >= 0)
            def _():
                zero_copy(e).start()
            return c

        def wait(e, c):
            @pl.when(zoff_ref[e] >= 0)
            def _():
                zero_copy(e).wait()
            return c

        lax.fori_loop(0, N_EXPERTS, start, 0)
        lax.fori_loop(0, N_EXPERTS, wait, 0)

    def push_token(t, c):
        src = x3_ref.at[pl.ds(pl.multiple_of(t * SUBLANES, SUBLANES), SUBLANES), :]
        for k in range(TOP_K):
            row = pl.multiple_of(d_ref[t * TOP_K + k], SUBLANES)
            dst = xs_ref.at[pl.ds(row, SUBLANES), :]
            pltpu.make_async_copy(src, dst, sem).start(priority=k % DMA_PRIORITIES)
        return c

    lax.fori_loop(0, toks, push_token, 0)
    for _ in range(TOP_K):
        pltpu.make_async_copy(x3_ref, xs_ref.at[pl.ds(0, toks * SUBLANES), :], sem).wait()


def _push(zero_off, d_flat, x3, n_slots):
    T = x3.shape[0] // SUBLANES
    toks = min(PUSH_ROWS, T)
    return pl.pallas_call(
        _push_kernel,
        grid_spec=pltpu.PrefetchScalarGridSpec(
            num_scalar_prefetch=1,
            grid=(T // toks,),
            in_specs=[
                pl.BlockSpec((toks * TOP_K,), lambda i, *_: (i,), memory_space=pltpu.SMEM),
                pl.BlockSpec((toks * SUBLANES, LANES), lambda i, *_: (i, 0)),
            ],
            out_specs=pl.BlockSpec(memory_space=pl.ANY),
            scratch_shapes=[
                pltpu.VMEM((SLOT_BLOCK * SUBLANES, LANES), F32),
                pltpu.SemaphoreType.DMA(()),
                pltpu.SemaphoreType.DMA(()),
            ],
        ),
        out_shape=jax.ShapeDtypeStruct((n_slots * SUBLANES, LANES), F32),
        compiler_params=_cparams(("arbitrary",)),
        name="moe_push",
    )(zero_off, d_flat, x3)


def _expert_kernel(blk_e_ref, nused_ref, xs_hbm, wg_ref, wu_ref, wd_ref, ys_hbm,
                   wgu_s, wd_s, xbuf, ybuf, xsem, ysem):
    b = pl.program_id(0)
    n_used = nused_ref[0]
    blk_rows = xbuf.shape[1]
    rows = blk_rows // SUBLANES
    D = wg_ref.shape[1]
    H = wg_ref.shape[2]

    def x_copy(blk, slot):
        src = xs_hbm.at[pl.ds(pl.multiple_of(blk * blk_rows, blk_rows), blk_rows), :]
        return pltpu.make_async_copy(src, xbuf.at[slot], xsem.at[slot])

    def y_copy(blk, slot):
        dst = ys_hbm.at[pl.ds(pl.multiple_of(blk * blk_rows, blk_rows), blk_rows), :]
        return pltpu.make_async_copy(ybuf.at[slot], dst, ysem.at[slot])

    @pl.when(b == 0)
    def _():
        for j in range(EXPERT_IN_BUFS - 1):
            @pl.when(j < n_used)
            def _(j=j):
                x_copy(j, j).start()

    prev = blk_e_ref[jnp.maximum(b - 1, 0)]

    @pl.when((b == 0) | (blk_e_ref[b] != prev))
    def _():
        wgu_s[:, :H] = wg_ref[0].astype(BF16)
        wgu_s[:, H:] = wu_ref[0].astype(BF16)
        wd_s[...] = wd_ref[0].astype(BF16)

    @pl.when(b < n_used)
    def _():
        ahead = b + (EXPERT_IN_BUFS - 1)

        @pl.when(ahead < n_used)
        def _():
            x_copy(ahead, jnp.bitwise_and(ahead, EXPERT_IN_BUFS - 1)).start()

        slot = jnp.bitwise_and(b, EXPERT_IN_BUFS - 1)
        x_copy(b, slot).wait()
        xv = xbuf.at[slot]
        xb = jnp.concatenate(
            [xv[pl.ds(s, rows, stride=SUBLANES), :].astype(BF16) for s in range(D // LANES)],
            axis=-1)
        h = _dot(xb, wgu_s[...])
        act = (jax.nn.silu(h[:, :H]) * h[:, H:]).astype(BF16)
        out = _dot(act, wd_s[...])

        yslot = jnp.bitwise_and(b, EXPERT_OUT_BUFS - 1)

        @pl.when(b >= EXPERT_OUT_BUFS)
        def _():
            y_copy(b - EXPERT_OUT_BUFS, yslot).wait()

        yv = ybuf.at[yslot]
        for s in range(D // LANES):
            yv[pl.ds(s, rows, stride=SUBLANES), :] = out[:, s * LANES:(s + 1) * LANES]
        y_copy(b, yslot).start()

    @pl.when(b == pl.num_programs(0) - 1)
    def _():
        for j in range(EXPERT_OUT_BUFS):
            blk = n_used - 1 - j

            @pl.when(blk >= 0)
            def _(blk=blk):
                y_copy(blk, jnp.bitwise_and(blk, EXPERT_OUT_BUFS - 1)).wait()


def _experts(blk_e, n_used, xs, w_gate, w_up, w_down):
    n_blocks = blk_e.shape[0]
    E, D, H = w_gate.shape
    blk_rows = SLOT_BLOCK * SUBLANES
    return pl.pallas_call(
        _expert_kernel,
        grid_spec=pltpu.PrefetchScalarGridSpec(
            num_scalar_prefetch=2,
            grid=(n_blocks,),
            in_specs=[
                pl.BlockSpec(memory_space=pl.ANY),
                pl.BlockSpec((1, D, H), lambda b, be, nu: (be[b], 0, 0)),
                pl.BlockSpec((1, D, H), lambda b, be, nu: (be[b], 0, 0)),
                pl.BlockSpec((1, H, D), lambda b, be, nu: (be[b], 0, 0)),
            ],
            out_specs=pl.BlockSpec(memory_space=pl.ANY),
            scratch_shapes=[
                pltpu.VMEM((D, 2 * H), BF16), pltpu.VMEM((H, D), BF16),
                pltpu.VMEM((EXPERT_IN_BUFS, blk_rows, LANES), F32),
                pltpu.VMEM((EXPERT_OUT_BUFS, blk_rows, LANES), F32),
                pltpu.SemaphoreType.DMA((EXPERT_IN_BUFS,)),
                pltpu.SemaphoreType.DMA((EXPERT_OUT_BUFS,)),
            ],
        ),
        out_shape=jax.ShapeDtypeStruct(xs.shape, F32),
        compiler_params=_cparams(("arbitrary",)),
        name="moe_experts",
    )(blk_e, n_used, xs, w_gate, w_up, w_down)


def _combine_kernel(d_ref, dn_ref, ys_ref, base_ref, rw_ref, g2_ref,
                    b2_ref, o_ref, buf0, buf1, sem0, sem1):
    toks, D = base_ref.shape
    i = pl.program_id(0)
    last = pl.num_programs(0) - 1

    def issue(dref, buf, sem):
        def gather_token(t, c):
            for k in range(TOP_K):
                row = pl.multiple_of(dref[t * TOP_K + k], SUBLANES)
                src = ys_ref.at[pl.ds(row, SUBLANES), :]
                dst = buf.at[pl.ds(pl.multiple_of((k * toks + t) * SUBLANES, SUBLANES), SUBLANES), :]
                pltpu.make_async_copy(src, dst, sem).start(priority=k % DMA_PRIORITIES)
            return c

        lax.fori_loop(0, toks, gather_token, 0)

    def finish(buf, sem):
        pltpu.make_async_copy(ys_ref.at[pl.ds(0, buf.shape[0]), :], buf, sem).wait()
        w = rw_ref[...]
        pieces = []
        for s in range(D // LANES):
            acc = jnp.zeros((toks, LANES), F32)
            for k in range(TOP_K):
                rows = buf[pl.ds(k * toks * SUBLANES + s, toks, stride=SUBLANES), :]
                acc = acc + w[:, k:k + 1] * rows
            pieces.append(acc)
        y = base_ref[...] + jnp.concatenate(pieces, axis=-1)
        o_ref[...] = _layer_norm(y, g2_ref[...], b2_ref[...])

    @pl.when(i == 0)
    def _():
        issue(d_ref, buf0, sem0)

    for parity, (cur, csem, nxt, nsem) in enumerate(((buf0, sem0, buf1, sem1), (buf1, sem1, buf0, sem0))):
        @pl.when(jnp.bitwise_and(i, 1) == parity)
        def _(cur=cur, csem=csem, nxt=nxt, nsem=nsem):
            @pl.when(i < last)
            def _():
                issue(dn_ref, nxt, nsem)
            finish(cur, csem)


def _combine(d_flat, ys, base, rw, g2, b2):
    T, D = base.shape
    toks = min(COMB_ROWS, T)
    steps = T // toks
    idx_now = pl.BlockSpec((toks * TOP_K,), lambda i: (i,), memory_space=pltpu.SMEM)
    idx_next = pl.BlockSpec((toks * TOP_K,), lambda i: (jnp.minimum(i + 1, steps - 1),),
                            memory_space=pltpu.SMEM)
    return pl.pallas_call(
        _combine_kernel,
        grid_spec=pltpu.PrefetchScalarGridSpec(
            num_scalar_prefetch=0,
            grid=(steps,),
            in_specs=[
                idx_now, idx_next,
                pl.BlockSpec(memory_space=pl.ANY),
                pl.BlockSpec((toks, D), lambda i: (i, 0)),
                pl.BlockSpec((toks, TOP_K), lambda i: (i, 0)),
                pl.BlockSpec(g2.shape, lambda i: (0, 0)),
                pl.BlockSpec(b2.shape, lambda i: (0, 0)),
            ],
            out_specs=pl.BlockSpec((toks, D), lambda i: (i, 0)),
            scratch_shapes=[
                pltpu.VMEM((TOP_K * toks * SUBLANES, LANES), F32),
                pltpu.VMEM((TOP_K * toks * SUBLANES, LANES), F32),
                pltpu.SemaphoreType.DMA(()),
                pltpu.SemaphoreType.DMA(()),
            ],
        ),
        out_shape=jax.ShapeDtypeStruct((T, D), F32),
        compiler_params=_cparams(("arbitrary",)),
        name="moe_combine",
    )(d_flat, d_flat, ys, base, rw, g2, b2)


def _overlap_matrix(ncp):
    n = np.arange(ncp)[:, None]
    j = np.arange(LANES)[None, :]
    start = n * CMP_STRIDE
    end = start + CMP_LEN - 1
    sel_start = j * SEL_BLOCK
    ovl = (start < sel_start + SEL_BLOCK) & (end >= sel_start)
    return jnp.asarray(ovl.astype(np.float32), dtype=BF16)


def _block_onehot(seq):
    pos = np.arange(seq)[:, None]
    j = np.arange(LANES)[None, :]
    return jnp.asarray((pos // SEL_BLOCK == j).astype(np.float32), dtype=BF16)


def _mixer(x2, batch, seq, w_in, conv_w, cmp_k, cmp_v):
    c3 = 3 * CONV_CH
    qd = N_HEADS * HEAD_DIM
    w_conv = w_in[:, :c3].astype(BF16)
    w_q = w_in[:, c3:c3 + qd].astype(BF16)
    kv0 = c3 + qd
    part = lambda j: w_in[:, kv0 + j * KV_DIM:kv0 + (j + 1) * KV_DIM]
    zero = jnp.zeros((w_in.shape[0], HEAD_DIM), w_in.dtype)

    def per_group(w):
        return [c for g in range(N_KV_HEADS) for c in (w[:, g * HEAD_DIM:(g + 1) * HEAD_DIM], zero)]

    w_kv = jnp.concatenate([part(0), part(1)] + per_group(part(2)) + per_group(part(4))
                           + per_group(part(3)) + per_group(part(5)), axis=1).astype(BF16)
    w_g = jnp.pad(w_in[:, kv0 + 6 * KV_DIM:], ((0, 0), (0, LANES - N_HEADS * N_BRANCH))).astype(BF16)
    conv_out, q, kv, gates = _proj_conv(x2, w_conv, w_q, w_kv, w_g, conv_w, batch, seq)
    kc, vc = _compress(kv[:, :KV_DIM], kv[:, KV_DIM:2 * KV_DIM], cmp_k, cmp_v, batch, seq)
    ncp = -(-kc.shape[1] // LANES) * LANES
    if ncp != kc.shape[1]:
        padn = ((0, 0), (0, ncp - kc.shape[1]), (0, 0))
        kc, vc = jnp.pad(kc, padn), jnp.pad(vc, padn)
    kv3 = kv.reshape(batch, seq, kv.shape[1])
    nsa_out = _nsa(q, kc, vc, kv3, gates, _overlap_matrix(ncp), _block_onehot(seq), batch, seq)
    return conv_out, nsa_out


def _moe(x3, base, e_t, r_t, w_t, counts, w_gate, w_up, w_down, g2, b2):
    T = base.shape[0]
    A = T * TOP_K
    n_blocks = -(-(A + N_EXPERTS * (SLOT_BLOCK - 1)) // SLOT_BLOCK)
    cnt = counts[:, 0]
    padded = (cnt + SLOT_BLOCK - 1) // SLOT_BLOCK * SLOT_BLOCK
    pad_end = jnp.cumsum(padded)
    pad_start = (pad_end - padded).astype(I32)
    zero_off = jnp.where(padded > 0, pad_end - SLOT_BLOCK, -1).astype(I32)
    n_used = (pad_end[-1:] // SLOT_BLOCK).astype(I32)
    blk_start = jnp.arange(n_blocks, dtype=I32) * SLOT_BLOCK
    last_e = jnp.max(jnp.where(padded > 0, jnp.arange(N_EXPERTS, dtype=I32), 0))
    blk_e = jnp.minimum(jnp.sum((pad_end[None, :] <= blk_start[:, None]).astype(I32), axis=1),
                        last_e).astype(I32)
    d_flat = _slot_rows(pad_start, e_t, r_t).T.reshape(A)
    xs = _push(zero_off, d_flat, x3, n_blocks * SLOT_BLOCK)
    ys = _experts(blk_e, n_used, xs, w_gate, w_up, w_down)
    return _combine(d_flat, ys, base, w_t.T, g2, b2)


def kernel(x, w_in, conv_w, ck_pos, ck_w1, ck_b1, ck_w2, cv_pos, cv_w1, cv_b1, cv_w2, w_out, ln1_g, ln1_b, router_w, router_bias, w_gate, w_up, w_down, ws_gate, ws_up, ws_down, ln2_g, ln2_b):
    batch, seq, D = x.shape
    depth = w_in.shape[0]
    alpha = (2.0 * depth) ** 0.25
    x2 = x.reshape(batch * seq, D)
    for l in range(depth):
        cmp_k = _compress_weights(ck_pos[l], ck_w1[l], ck_b1[l], ck_w2[l])
        cmp_v = _compress_weights(cv_pos[l], cv_w1[l], cv_b1[l], cv_w2[l])
        conv_out, nsa_out = _mixer(x2, batch, seq, w_in[l], conv_w[l], cmp_k, cmp_v)
        rw_hi, rw_lo = _split_bf16(router_w[l].T)
        rbias = jnp.broadcast_to(router_bias[l][:, None], (N_EXPERTS, LANES))
        x3, base, e_t, r_t, w_t, counts = _post(
            x2, conv_out, nsa_out, w_out[l].astype(BF16), ln1_g[l][None, :], ln1_b[l][None, :],
            rw_hi, rw_lo, rbias,
            ws_gate[l].astype(BF16), ws_up[l].astype(BF16), ws_down[l].astype(BF16), alpha)
        x2 = _moe(x3, base, e_t, r_t, w_t, counts, w_gate[l], w_up[l], w_down[l],
                  ln2_g[l][None, :], ln2_b[l][None, :])
    return x2.reshape(batch, seq, D)
```

```python
import functools
import math

import jax
import jax.numpy as jnp
import numpy as np
from jax import lax
from jax.experimental import pallas as pl
from jax.experimental.pallas import tpu as pltpu

F32 = jnp.float32
BF16 = jnp.bfloat16
I32 = jnp.int32

CONV_CH = 512
CONV_WIDTH = 3
N_HEADS = 8
HEAD_DIM = 64
N_KV_HEADS = 2
Q_PER_KV = N_HEADS // N_KV_HEADS
KV_DIM = N_KV_HEADS * HEAD_DIM
N_BRANCH = 3
CMP_LEN = 32
CMP_STRIDE = 16
SEL_BLOCK = 64
SEL_TOPK = 8
WINDOW = 512
FORCED_SCORE = 1e4
N_EXPERTS = 256
TOP_K = 8
N_GROUPS = 8
TOPK_GROUPS = 4
GROUP_SIZE = N_EXPERTS // N_GROUPS
ROUTED_SCALE = 2.5
LN_EPS = 1e-5
NEG_INF = -1e30
SEL_SHIFT = SEL_BLOCK.bit_length() - 1
GROUP_SHIFT = GROUP_SIZE.bit_length() - 1
TOPK_SHIFT = TOP_K.bit_length() - 1

LANES = 128
SUBLANES = 8
VMEM_LIMIT = 40 * 1024 * 1024
DMA_PRIORITIES = 2

PROJ_ROWS = 1024
NSA_Q = 512
NSA_WIN_Q = 128
NSA_KC = 512
POST_ROWS = 512
SLOT_BLOCK = 512
EXPERT_IN_BUFS = 4
EXPERT_OUT_BUFS = 2
PUSH_ROWS = 1024
COMB_ROWS = 256


def _dot(a, b):
    return jnp.dot(a, b, preferred_element_type=F32)


def _dot_t(a, b):
    return lax.dot_general(a, b, (((1,), (1,)), ((), ())), preferred_element_type=F32)


def _split_bf16(x):
    hi = x.astype(BF16)
    lo = (x - hi.astype(F32)).astype(BF16)
    return hi, lo


def _cparams(sem):
    return pltpu.CompilerParams(dimension_semantics=sem, vmem_limit_bytes=VMEM_LIMIT)


def _proj_conv_kernel(x_ref, wc_ref, wq_ref, wkv_ref, wg_ref, cw_ref,
                      conv_ref, q_ref, kv_ref, gate_ref, carry_ref):
    rows = x_ref.shape[0]

    @pl.when(pl.program_id(1) == 0)
    def _():
        carry_ref[...] = jnp.zeros_like(carry_ref)

    xb = x_ref[...].astype(BF16)
    acc = _dot(xb, wc_ref[...])
    b_g = acc[:, :CONV_CH]
    u = acc[:, CONV_CH:2 * CONV_CH] * acc[:, 2 * CONV_CH:]
    prev2 = carry_ref[SUBLANES - 2:SUBLANES - 1, :]
    prev1 = carry_ref[SUBLANES - 1:SUBLANES, :]
    ri = lax.broadcasted_iota(I32, (rows, 1), 0)
    u1 = jnp.where(ri == 0, prev1, pltpu.roll(u, 1, 0))
    u2 = jnp.where(ri == 0, prev2, jnp.where(ri == 1, prev1, pltpu.roll(u, 2, 0)))
    y = cw_ref[0:1, :] * u2 + cw_ref[1:2, :] * u1 + cw_ref[2:3, :] * u
    conv_ref[...] = (b_g * y).astype(BF16)
    carry_ref[...] = u[rows - SUBLANES:, :]

    q_ref[...] = (_dot(xb, wq_ref[...]) * (HEAD_DIM ** -0.5)).astype(BF16)
    gate_ref[...] = jax.nn.sigmoid(_dot(xb, wg_ref[...]))

    kv = _dot(xb, wkv_ref[...])
    n_plain, n_key = 2, 2 * N_KV_HEADS
    key_w = n_key * LANES
    pos = pl.program_id(1) * rows + lax.broadcasted_iota(I32, (rows, key_w), 0)
    l128 = jnp.bitwise_and(lax.broadcasted_iota(I32, (rows, key_w), 1), LANES - 1)
    feat = jnp.where(l128 == HEAD_DIM, jnp.right_shift(pos, SEL_SHIFT),
                     jnp.where(l128 == HEAD_DIM + 1, jnp.bitwise_and(pos, SEL_BLOCK - 1), 0))
    ones = jnp.where(l128 >= HEAD_DIM, 1.0, 0.0)
    k0, v0 = n_plain * LANES, n_plain * LANES + key_w
    kv_ref[:, :k0] = kv[:, :k0].astype(BF16)
    kv_ref[:, k0:v0] = (kv[:, k0:v0] + feat.astype(F32)).astype(BF16)
    kv_ref[:, v0:] = (kv[:, v0:] + ones).astype(BF16)


def _proj_conv(x2, w_conv, w_q, w_kv, w_g, conv_w, batch, seq):
    T, D = x2.shape
    rows = min(PROJ_ROWS, seq)
    nt = seq // rows
    row_map = lambda b, i: (b * nt + i, 0)
    fixed = lambda b, i: (0, 0)
    return pl.pallas_call(
        _proj_conv_kernel,
        grid=(batch, nt),
        in_specs=[
            pl.BlockSpec((rows, D), row_map),
            pl.BlockSpec(w_conv.shape, fixed),
            pl.BlockSpec(w_q.shape, fixed),
            pl.BlockSpec(w_kv.shape, fixed),
            pl.BlockSpec(w_g.shape, fixed),
            pl.BlockSpec(conv_w.shape, fixed),
        ],
        out_specs=[
            pl.BlockSpec((rows, CONV_CH), row_map),
            pl.BlockSpec((rows, N_HEADS * HEAD_DIM), row_map),
            pl.BlockSpec((rows, w_kv.shape[1]), row_map),
            pl.BlockSpec((rows, LANES), row_map),
        ],
        out_shape=[
            jax.ShapeDtypeStruct((T, CONV_CH), BF16),
            jax.ShapeDtypeStruct((T, N_HEADS * HEAD_DIM), BF16),
            jax.ShapeDtypeStruct((T, w_kv.shape[1]), BF16),
            jax.ShapeDtypeStruct((T, LANES), F32),
        ],
        scratch_shapes=[pltpu.VMEM((SUBLANES, CONV_CH), F32)],
        compiler_params=_cparams(("arbitrary", "arbitrary")),
        name="proj_conv",
    )(x2, w_conv, w_q, w_kv, w_g, conv_w)


def _compress_kernel(ck_ref, cv_ref, wtk_ref, wbk_ref, w2k_ref, ptk_ref, pbk_ref, b1k_ref,
                     wtv_ref, wbv_ref, w2v_ref, ptv_ref, pbv_ref, b1v_ref, kc_ref, vc_ref):
    def one(c_ref, wt_ref, wb_ref, w2_ref, pt_ref, pb_ref, b1_ref, o_ref):
        c = c_ref[0]
        top = _dot(c, wt_ref[...])
        bot = _dot(c, wb_ref[...])
        c0 = _dot(pt_ref[...], wt_ref[...]) + _dot(pb_ref[...], wb_ref[...]) + b1_ref[...]
        n = top.shape[0]
        h = top + pltpu.roll(bot, n - 1, 0) + c0[0:1, :]
        g = jax.nn.gelu(h, approximate=True)
        o_ref[0] = _dot(g.astype(BF16), w2_ref[...]).astype(BF16)

    one(ck_ref, wtk_ref, wbk_ref, w2k_ref, ptk_ref, pbk_ref, b1k_ref, kc_ref)
    one(cv_ref, wtv_ref, wbv_ref, w2v_ref, ptv_ref, pbv_ref, b1v_ref, vc_ref)


def _blockdiag2(w):
    z = jnp.zeros_like(w)
    return jnp.concatenate([jnp.concatenate([w, z], 1), jnp.concatenate([z, w], 1)], 0)


def _compress_weights(pos, w1, b1, w2):
    w1r = w1.reshape(CMP_LEN, HEAD_DIM, HEAD_DIM)
    eye = jnp.eye(N_KV_HEADS, dtype=w1.dtype)
    wfull = (w1r[:, None, :, None, :] * eye[None, :, None, :, None]).reshape(CMP_LEN, KV_DIM, KV_DIM)
    w_top = wfull[:CMP_STRIDE].reshape(CMP_STRIDE * KV_DIM, KV_DIM).astype(BF16)
    w_bot = wfull[CMP_STRIDE:].reshape(CMP_STRIDE * KV_DIM, KV_DIM).astype(BF16)
    posr = jnp.tile(pos, (1, N_KV_HEADS))
    pos_top = jnp.tile(posr[:CMP_STRIDE].reshape(1, -1), (SUBLANES, 1)).astype(BF16)
    pos_bot = jnp.tile(posr[CMP_STRIDE:].reshape(1, -1), (SUBLANES, 1)).astype(BF16)
    b1r = jnp.tile(b1[None, :], (SUBLANES, N_KV_HEADS)).astype(F32)
    return w_top, w_bot, _blockdiag2(w2).astype(BF16), pos_top, pos_bot, b1r


def _compress(kc_raw, vc_raw, wk, wv, batch, seq):
    chunks = seq // CMP_STRIDE
    width = CMP_STRIDE * KV_DIM
    ck = kc_raw.reshape(batch, chunks, width)
    cv = vc_raw.reshape(batch, chunks, width)
    bmap = lambda b: (b, 0, 0)
    fixed = lambda b: (0, 0)
    wspecs = [pl.BlockSpec(w.shape, fixed) for w in wk]
    return pl.pallas_call(
        _compress_kernel,
        grid=(batch,),
        in_specs=[pl.BlockSpec((1, chunks, width), bmap), pl.BlockSpec((1, chunks, width), bmap)]
        + wspecs + wspecs,
        out_specs=[pl.BlockSpec((1, chunks, KV_DIM), bmap)] * 2,
        out_shape=[jax.ShapeDtypeStruct((batch, chunks, KV_DIM), BF16)] * 2,
        compiler_params=_cparams(("arbitrary",)),
        name="compress",
    )(ck, cv, *wk, *wv)


def _softmax_rows(s, valid):
    s = jnp.where(valid, s, NEG_INF)
    m = jnp.max(s, axis=-1, keepdims=True)
    p = jnp.where(valid, jnp.exp(s - m), 0.0)
    l = jnp.sum(p, axis=-1, keepdims=True)
    inv = jnp.where(l > 0.0, 1.0 / l, 0.0)
    return p, inv


def _nsa_kernel(q_ref, kc_ref, vc_ref, ks0_ref, ks1_ref, kw0_ref, kw1_ref, vs0_ref, vs1_ref,
                vw0_ref, vw1_ref, gate_ref, ovl_ref, hot_ref, o_ref, *, seq, n_sel):
    ks_refs, kw_refs = (ks0_ref, ks1_ref), (kw0_ref, kw1_ref)
    vs_refs, vw_refs = (vs0_ref, vs1_ref), (vw0_ref, vw1_ref)
    tq = q_ref.shape[0]
    ncp = kc_ref.shape[1]
    rows = Q_PER_KV * tq
    q0 = pl.program_id(1) * tq
    t_col = q0 + lax.broadcasted_iota(I32, (tq, 1), 0)
    t4 = jnp.concatenate([t_col] * Q_PER_KV, axis=0)
    row_i = lax.broadcasted_iota(I32, (rows, 1), 0)
    lane = lax.broadcasted_iota(I32, (1, LANES), 1)
    feat_lane = lax.broadcasted_iota(I32, (1, HEAD_DIM), 1)
    n_sel_pad = -(-n_sel // SUBLANES) * SUBLANES
    blk_row = lax.broadcasted_iota(I32, (n_sel_pad, tq), 0)
    blk_row_f = blk_row.astype(F32)
    gates = gate_ref[...]
    n_chunks = (q0 + tq + NSA_KC - 1) // NSA_KC
    outs, stage = [], []
    for g in range(N_KV_HEADS):
        lo, hi = g * HEAD_DIM, (g + 1) * HEAD_DIM
        qg = jnp.concatenate(
            [q_ref[:, (g * Q_PER_KV + r) * HEAD_DIM:(g * Q_PER_KV + r + 1) * HEAD_DIM]
             for r in range(Q_PER_KV)], axis=0)
        slope = jnp.zeros((rows, 1), F32)
        for r in range(Q_PER_KV):
            h = g * Q_PER_KV + r
            in_head = (row_i >= r * tq) & (row_i < (r + 1) * tq)
            slope = jnp.where(in_head, 2.0 ** (-8.0 * (h + 1) / N_HEADS), slope)

        cmp_end = lax.broadcasted_iota(I32, (1, ncp), 1) * CMP_STRIDE + (CMP_LEN - 1)
        d_c = t4 - cmp_end
        s_c = _dot_t(qg, kc_ref[0, :, lo:hi]) - slope * d_c.astype(F32)
        p_c, inv_c = _softmax_rows(s_c, d_c >= 0)
        p_c = p_c * inv_c
        o_cmp = _dot(p_c.astype(BF16), vc_ref[0, :, lo:hi])

        ps = p_c[0:tq]
        for r in range(1, Q_PER_KV):
            ps = ps + p_c[r * tq:(r + 1) * tq]
        ps_hi, ps_lo = _split_bf16(ps)
        imp = _dot(ps_hi, ovl_ref[...]) + _dot(ps_lo, ovl_ref[...])
        forced = (lane == 0) | (lane == jnp.right_shift(t_col, SEL_SHIFT))
        causal = lane * SEL_BLOCK <= t_col
        score = jnp.where(forced, FORCED_SCORE, jnp.where(causal, imp, -1.0))
        score_t = score.T[:n_sel_pad, :]
        score_t = jnp.where(blk_row < n_sel, score_t, -jnp.inf)
        sel_t = jnp.zeros((n_sel_pad, tq), F32)
        for _ in range(min(SEL_TOPK, n_sel)):
            mx = jnp.max(score_t, axis=0, keepdims=True)
            first = jnp.min(jnp.where(score_t == mx, blk_row_f, float(LANES)), axis=0, keepdims=True)
            hit = blk_row_f == first
            sel_t = jnp.where(hit, 1.0, sel_t)
            score_t = jnp.where(hit, -jnp.inf, score_t)
        unsel_t = jnp.concatenate(
            [jnp.where(sel_t > 0.5, 0.0, NEG_INF), jnp.full((LANES - n_sel_pad, tq), NEG_INF, F32)], axis=0)
        unsel_b = unsel_t.T.astype(BF16)

        q_feat = jnp.where(feat_lane == 0, slope * float(SEL_BLOCK),
                           jnp.where(feat_lane == 1, slope, 0.0)).astype(BF16)
        q_aug = jnp.concatenate([qg, q_feat], axis=1)
        lhs = jnp.concatenate([q_aug, jnp.concatenate([unsel_b] * Q_PER_KV, axis=0)], axis=1)

        stage.append((q_aug, lhs, o_cmp))

    def sel_chunk(c, carry, diagonal):
        k0 = pl.multiple_of(c * NSA_KC, NSA_KC)
        hot = hot_ref[pl.ds(k0, NSA_KC), :]
        new = []
        for g in range(N_KV_HEADS):
            m, acc = carry[g]
            kch = ks_refs[g][0, pl.ds(k0, NSA_KC), :]
            s = _dot_t(stage[g][1], jnp.concatenate([kch, hot], axis=1))
            if diagonal:
                pos = k0 + lax.broadcasted_iota(I32, (1, NSA_KC), 1)
                s = jnp.where(pos <= t4, s, NEG_INF)
            m_new = jnp.maximum(m, jnp.max(s, axis=-1, keepdims=True))
            a = jnp.exp(m - m_new)
            p = jnp.exp((s - m_new).astype(BF16))
            acc = a * acc + _dot(p, vs_refs[g][0, pl.ds(k0, NSA_KC), :])
            new.append((m_new, acc))
        return tuple(new)

    init = tuple((jnp.full((rows, 1), NEG_INF, F32), jnp.zeros((rows, 2 * HEAD_DIM), F32))
                 for _ in range(N_KV_HEADS))
    carry = lax.fori_loop(0, n_chunks - 1, functools.partial(sel_chunk, diagonal=False), init)
    wq = min(NSA_WIN_Q, tq)

    def head_rows(a, u):
        return jnp.concatenate([a[r * tq + u * wq:r * tq + (u + 1) * wq] for r in range(Q_PER_KV)], axis=0)

    final = sel_chunk(n_chunks - 1, carry, diagonal=True)

    win_len = WINDOW + wq
    win_out = [[[] for _ in range(Q_PER_KV)] for _ in range(N_KV_HEADS)]
    for u in range(tq // wq):
        qs = q0 + u * wq
        w_start = pl.multiple_of(jnp.maximum(qs - WINDOW, 0), wq)
        d_w = (qs + lax.broadcasted_iota(I32, (wq, 1), 0)
               - (w_start + lax.broadcasted_iota(I32, (1, win_len), 1)))
        band = jnp.where((d_w >= 0) & (d_w < WINDOW), 0.0, NEG_INF)
        band4 = jnp.concatenate([band] * Q_PER_KV, axis=0)
        for g in range(N_KV_HEADS):
            s_w = _dot_t(head_rows(stage[g][0], u), kw_refs[g][0, pl.ds(w_start, win_len), :]) + band4
            m_w = jnp.max(s_w, axis=-1, keepdims=True)
            p_w = jnp.exp((s_w - m_w).astype(BF16))
            acc_w = _dot(p_w, vw_refs[g][0, pl.ds(w_start, win_len), :])
            o_sub = acc_w[:, :HEAD_DIM] * (1.0 / acc_w[:, HEAD_DIM:])
            for r in range(Q_PER_KV):
                win_out[g][r].append(o_sub[r * wq:(r + 1) * wq])

    for g in range(N_KV_HEADS):
        o_cmp = stage[g][2]
        acc_s = final[g][1]
        o_slc = acc_s[:, :HEAD_DIM] * (1.0 / acc_s[:, HEAD_DIM:])
        for r in range(Q_PER_KV):
            h = g * Q_PER_KV + r
            sl = slice(r * tq, (r + 1) * tq)
            gc = gates[:, h * N_BRANCH:h * N_BRANCH + 1]
            gs = gates[:, h * N_BRANCH + 1:h * N_BRANCH + 2]
            gw = gates[:, h * N_BRANCH + 2:h * N_BRANCH + 3]
            o_win = jnp.concatenate(win_out[g][r], axis=0)
            outs.append(gc * o_cmp[sl] + gs * o_slc[sl] + gw * o_win)
    o_ref[...] = jnp.concatenate(outs, axis=-1).astype(BF16)


def _nsa(q, kc, vc, kv3, gates, overlap, blk_onehot, batch, seq):
    T = q.shape[0]
    tq = min(NSA_Q, seq)
    assert NSA_KC % tq == 0 and seq % NSA_KC == 0 and tq % min(NSA_WIN_Q, tq) == 0
    assert seq >= WINDOW + min(NSA_WIN_Q, tq)
    nq = seq // tq
    ncp = kc.shape[1]
    n_sel = seq // SEL_BLOCK
    row_map = lambda b, i: (b * nq + i, 0)
    bmap = lambda b, i: (b, 0, 0)
    fixed = lambda b, i: (0, 0)
    kvspec = lambda j: pl.BlockSpec((1, seq, LANES), lambda b, i, j=j: (b, 0, j))
    n_kv = 4 * N_KV_HEADS
    return pl.pallas_call(
        functools.partial(_nsa_kernel, seq=seq, n_sel=n_sel),
        grid=(batch, nq),
        in_specs=[
            pl.BlockSpec((tq, N_HEADS * HEAD_DIM), row_map),
            pl.BlockSpec((1, ncp, KV_DIM), bmap),
            pl.BlockSpec((1, ncp, KV_DIM), bmap),
        ] + [kvspec(2 + j) for j in range(n_kv)] + [
            pl.BlockSpec((tq, LANES), row_map),
            pl.BlockSpec(overlap.shape, fixed),
            pl.BlockSpec(blk_onehot.shape, fixed),
        ],
        out_specs=pl.BlockSpec((tq, N_HEADS * HEAD_DIM), row_map),
        out_shape=jax.ShapeDtypeStruct((T, N_HEADS * HEAD_DIM), BF16),
        compiler_params=_cparams(("arbitrary", "arbitrary")),
        name="nsa",
    )(q, kc, vc, *([kv3] * n_kv), gates, overlap, blk_onehot)


def _layer_norm(y, g, b):
    mu = jnp.mean(y, axis=-1, keepdims=True)
    yc = y - mu
    var = jnp.mean(yc * yc, axis=-1, keepdims=True)
    return yc * lax.rsqrt(var + LN_EPS) * g + b


def _post_kernel(x_ref, conv_ref, nsa_ref, wo_ref, g1_ref, b1_ref, rwh_ref, rwl_ref, rb_ref,
                 wsg_ref, wsu_ref, wsd_ref,
                 x3_ref, base_ref, e_ref, r_ref, w_ref, cnt_ref, carry_ref, *, alpha):
    rows, D = x_ref.shape

    @pl.when(pl.program_id(0) == 0)
    def _():
        carry_ref[...] = jnp.zeros_like(carry_ref)

    half = wo_ref.shape[0] // 2
    mix = _dot(conv_ref[...], wo_ref[:half, :]) + _dot(nsa_ref[...], wo_ref[half:, :])
    x1 = _layer_norm(alpha * x_ref[...] + mix, g1_ref[...], b1_ref[...])
    for s in range(D // LANES):
        x3_ref[pl.ds(s, rows, stride=SUBLANES), :] = x1[:, s * LANES:(s + 1) * LANES]

    xh, xl = _split_bf16(x1)
    hid = jax.nn.silu(_dot(xh, wsg_ref[...])) * _dot(xh, wsu_ref[...])
    base_ref[...] = alpha * x1 + _dot(hid.astype(BF16), wsd_ref[...])

    logits = (_dot_t(rwh_ref[...], xh) + _dot_t(rwh_ref[...], xl) + _dot_t(rwl_ref[...], xh))
    scores = jax.nn.sigmoid(logits)
    reps = rows // LANES
    biased = scores + jnp.concatenate([rb_ref[...]] * reps, axis=1)
    eidx = lax.broadcasted_iota(I32, (N_EXPERTS, rows), 0).astype(F32)
    gidx = lax.broadcasted_iota(I32, (GROUP_SIZE, rows), 0).astype(F32)
    gvals, gscore = [], []
    for gi in range(N_GROUPS):
        v = biased[gi * GROUP_SIZE:(gi + 1) * GROUP_SIZE, :]
        m1 = jnp.max(v, axis=0, keepdims=True)
        i1 = jnp.min(jnp.where(v == m1, gidx, float(GROUP_SIZE)), axis=0, keepdims=True)
        m2 = jnp.max(jnp.where(gidx == i1, -jnp.inf, v), axis=0, keepdims=True)
        gvals.append(v)
        gscore.append(m1 + m2)
    cands = []
    for gi in range(N_GROUPS):
        ahead = jnp.zeros((1, rows), F32)
        for gj in range(N_GROUPS):
            if gj == gi:
                continue
            beats = (gscore[gj] >= gscore[gi]) if gj < gi else (gscore[gj] > gscore[gi])
            ahead = ahead + jnp.where(beats, 1.0, 0.0)
        ahead_full = jnp.broadcast_to(ahead, gvals[gi].shape)
        cands.append(jnp.where(ahead_full < float(TOPK_GROUPS), gvals[gi], NEG_INF))
    cand = jnp.concatenate(cands, axis=0)
    onehot = jnp.zeros((N_EXPERTS, rows), F32)
    idx_rows, w_rows = [], []
    for _ in range(TOP_K):
        mx = jnp.max(cand, axis=0, keepdims=True)
        first = jnp.min(jnp.where(cand == mx, eidx, float(N_EXPERTS)), axis=0, keepdims=True)
        hit = eidx == first
        idx_rows.append(first)
        w_rows.append(jnp.sum(jnp.where(hit, scores, 0.0), axis=0, keepdims=True))
        onehot = jnp.where(hit, 1.0, onehot)
        cand = jnp.where(hit, -jnp.inf, cand)
    wsum = w_rows[0]
    for k in range(1, TOP_K):
        wsum = wsum + w_rows[k]

    ti = lax.broadcasted_iota(I32, (rows, rows), 0)
    tj = lax.broadcasted_iota(I32, (rows, rows), 1)
    earlier = jnp.where(ti < tj, 1.0, 0.0).astype(BF16)
    carry = carry_ref[...]
    before = _dot(onehot.astype(BF16), earlier) + jnp.concatenate([carry] * reps, axis=1)
    krow = lax.broadcasted_iota(I32, (TOP_K, rows), 0)
    e_out = jnp.zeros((TOP_K, rows), F32)
    r_out = jnp.zeros((TOP_K, rows), F32)
    w_out = jnp.zeros((TOP_K, rows), F32)
    for k in range(TOP_K):
        rank = jnp.sum(jnp.where(eidx == idx_rows[k], before, 0.0), axis=0, keepdims=True)
        e_out = jnp.where(krow == k, idx_rows[k], e_out)
        r_out = jnp.where(krow == k, rank, r_out)
        w_out = jnp.where(krow == k, w_rows[k] / wsum * ROUTED_SCALE, w_out)
    e_ref[...] = e_out.astype(I32)
    r_ref[...] = r_out.astype(I32)
    w_ref[...] = w_out
    total = carry + jnp.sum(onehot, axis=1, keepdims=True)
    carry_ref[...] = total
    cnt_ref[...] = total.astype(I32)


def _post(x2, conv_out, nsa_out, w_out, g1, b1, rw_hi, rw_lo, rbias, wsg, wsu, wsd, alpha):
    T, D = x2.shape
    rows = min(POST_ROWS, T)
    row_map = lambda i: (i, 0)
    col_map = lambda i: (0, i)
    fixed = lambda i: (0, 0)
    full = lambda a: pl.BlockSpec(a.shape, fixed)
    return pl.pallas_call(
        functools.partial(_post_kernel, alpha=alpha),
        grid=(T // rows,),
        in_specs=[
            pl.BlockSpec((rows, D), row_map),
            pl.BlockSpec((rows, CONV_CH), row_map),
            pl.BlockSpec((rows, N_HEADS * HEAD_DIM), row_map),
            full(w_out), full(g1), full(b1), full(rw_hi), full(rw_lo), full(rbias),
            full(wsg), full(wsu), full(wsd),
        ],
        out_specs=[
            pl.BlockSpec((rows * SUBLANES, LANES), row_map),
            pl.BlockSpec((rows, D), row_map),
            pl.BlockSpec((TOP_K, rows), col_map),
            pl.BlockSpec((TOP_K, rows), col_map),
            pl.BlockSpec((TOP_K, rows), col_map),
            pl.BlockSpec((N_EXPERTS, LANES), fixed),
        ],
        out_shape=[
            jax.ShapeDtypeStruct((T * SUBLANES, LANES), F32),
            jax.ShapeDtypeStruct((T, D), F32),
            jax.ShapeDtypeStruct((TOP_K, T), I32),
            jax.ShapeDtypeStruct((TOP_K, T), I32),
            jax.ShapeDtypeStruct((TOP_K, T), F32),
            jax.ShapeDtypeStruct((N_EXPERTS, LANES), I32),
        ],
        scratch_shapes=[pltpu.VMEM((N_EXPERTS, LANES), F32)],
        compiler_params=_cparams(("arbitrary",)),
        name="post_attn_router",
    )(x2, conv_out, nsa_out, w_out, g1, b1, rw_hi, rw_lo, rbias, wsg, wsu, wsd)


def _slot_rows_kernel(pstart_ref, e_ref, r_ref, o_ref):
    e = e_ref[...]

    def add_expert(j, acc):
        return acc + jnp.where(e == j, pstart_ref[j], 0)

    o_ref[...] = lax.fori_loop(0, N_EXPERTS, add_expert, r_ref[...]) * SUBLANES


def _slot_rows(pad_start, e_t, r_t):
    T = e_t.shape[1]
    cols = min(2048, T)
    spec = pl.BlockSpec((TOP_K, cols), lambda i, *_: (0, i))
    return pl.pallas_call(
        _slot_rows_kernel,
        grid_spec=pltpu.PrefetchScalarGridSpec(
            num_scalar_prefetch=1, grid=(T // cols,), in_specs=[spec, spec], out_specs=spec),
        out_shape=jax.ShapeDtypeStruct(e_t.shape, I32),
        compiler_params=_cparams(("arbitrary",)),
        name="moe_slot_rows",
    )(pad_start, e_t, r_t)


def _push_kernel(zoff_ref, d_ref, x3_ref, xs_ref, zero_ref, sem, zsem):
    toks = x3_ref.shape[0] // SUBLANES
    zrows = zero_ref.shape[0]

    def zero_copy(e):
        off = pl.multiple_of(zoff_ref[e] * SUBLANES, SUBLANES)
        return pltpu.make_async_copy(zero_ref, xs_ref.at[pl.ds(off, zrows), :], zsem)

    @pl.when(pl.program_id(0) == 0)
    def _():
        zero_ref[...] = jnp.zeros_like(zero_ref)

        def start(e, c):
            @pl.when(zoff_ref[e] >= 0)
            def _():
                zero_copy(e).start()
            return c

        def wait(e, c):
            @pl.when(zoff_ref[e] >= 0)
            def _():
                zero_copy(e).wait()
            return c

        lax.fori_loop(0, N_EXPERTS, start, 0)
        lax.fori_loop(0, N_EXPERTS, wait, 0)

    def push_token(t, c):
        src = x3_ref.at[pl.ds(pl.multiple_of(t * SUBLANES, SUBLANES), SUBLANES), :]
        for k in range(TOP_K):
            row = pl.multiple_of(d_ref[t * TOP_K + k], SUBLANES)
            dst = xs_ref.at[pl.ds(row, SUBLANES), :]
            pltpu.make_async_copy(src, dst, sem).start(priority=k % DMA_PRIORITIES)
        return c

    lax.fori_loop(0, toks, push_token, 0)
    for _ in range(TOP_K):
        pltpu.make_async_copy(x3_ref, xs_ref.at[pl.ds(0, toks * SUBLANES), :], sem).wait()


def _push(zero_off, d_flat, x3, n_slots):
    T = x3.shape[0] // SUBLANES
    toks = min(PUSH_ROWS, T)
    return pl.pallas_call(
        _push_kernel,
        grid_spec=pltpu.PrefetchScalarGridSpec(
            num_scalar_prefetch=1,
            grid=(T // toks,),
            in_specs=[
                pl.BlockSpec((toks * TOP_K,), lambda i, *_: (i,), memory_space=pltpu.SMEM),
                pl.BlockSpec((toks * SUBLANES, LANES), lambda i, *_: (i, 0)),
            ],
            out_specs=pl.BlockSpec(memory_space=pl.ANY),
            scratch_shapes=[
                pltpu.VMEM((SLOT_BLOCK * SUBLANES, LANES), F32),
                pltpu.SemaphoreType.DMA(()),
                pltpu.SemaphoreType.DMA(()),
            ],
        ),
        out_shape=jax.ShapeDtypeStruct((n_slots * SUBLANES, LANES), F32),
        compiler_params=_cparams(("arbitrary",)),
        name="moe_push",
    )(zero_off, d_flat, x3)


def _expert_kernel(blk_e_ref, nused_ref, xs_hbm, wg_ref, wu_ref, wd_ref, ys_hbm,
                   wgu_s, wd_s, xbuf, ybuf, xsem, ysem):
    b = pl.program_id(0)
    n_used = nused_ref[0]
    blk_rows = xbuf.shape[1]
    rows = blk_rows // SUBLANES
    D = wg_ref.shape[1]
    H = wg_ref.shape[2]

    def x_copy(blk, slot):
        src = xs_hbm.at[pl.ds(pl.multiple_of(blk * blk_rows, blk_rows), blk_rows), :]
        return pltpu.make_async_copy(src, xbuf.at[slot], xsem.at[slot])

    def y_copy(blk, slot):
        dst = ys_hbm.at[pl.ds(pl.multiple_of(blk * blk_rows, blk_rows), blk_rows), :]
        return pltpu.make_async_copy(ybuf.at[slot], dst, ysem.at[slot])

    @pl.when(b == 0)
    def _():
        for j in range(EXPERT_IN_BUFS - 1):
            @pl.when(j < n_used)
            def _(j=j):
                x_copy(j, j).start()

    prev = blk_e_ref[jnp.maximum(b - 1, 0)]

    @pl.when((b == 0) | (blk_e_ref[b] != prev))
    def _():
        wgu_s[:, :H] = wg_ref[0].astype(BF16)
        wgu_s[:, H:] = wu_ref[0].astype(BF16)
        wd_s[...] = wd_ref[0].astype(BF16)

    @pl.when(b < n_used)
    def _():
        ahead = b + (EXPERT_IN_BUFS - 1)

        @pl.when(ahead < n_used)
        def _():
            x_copy(ahead, jnp.bitwise_and(ahead, EXPERT_IN_BUFS - 1)).start()

        slot = jnp.bitwise_and(b, EXPERT_IN_BUFS - 1)
        x_copy(b, slot).wait()
        xv = xbuf.at[slot]
        xb = jnp.concatenate(
            [xv[pl.ds(s, rows, stride=SUBLANES), :].astype(BF16) for s in range(D // LANES)],
            axis=-1)
        h = _dot(xb, wgu_s[...])
        act = (jax.nn.silu(h[:, :H]) * h[:, H:]).astype(BF16)
        out = _dot(act, wd_s[...])

        yslot = jnp.bitwise_and(b, EXPERT_OUT_BUFS - 1)

        @pl.when(b >= EXPERT_OUT_BUFS)
        def _():
            y_copy(b - EXPERT_OUT_BUFS, yslot).wait()

        yv = ybuf.at[yslot]
        for s in range(D // LANES):
            yv[pl.ds(s, rows, stride=SUBLANES), :] = out[:, s * LANES:(s + 1) * LANES]
        y_copy(b, yslot).start()

    @pl.when(b == pl.num_programs(0) - 1)
    def _():
        for j in range(EXPERT_OUT_BUFS):
            blk = n_used - 1 - j

            @pl.when(blk >= 0)
            def _(blk=blk):
                y_copy(blk, jnp.bitwise_and(blk, EXPERT_OUT_BUFS - 1)).wait()


def _experts(blk_e, n_used, xs, w_gate, w_up, w_down):
    n_blocks = blk_e.shape[0]
    E, D, H = w_gate.shape
    blk_rows = SLOT_BLOCK * SUBLANES
    return pl.pallas_call(
        _expert_kernel,
        grid_spec=pltpu.PrefetchScalarGridSpec(
            num_scalar_prefetch=2,
            grid=(n_blocks,),
            in_specs=[
                pl.BlockSpec(memory_space=pl.ANY),
                pl.BlockSpec((1, D, H), lambda b, be, nu: (be[b], 0, 0)),
                pl.BlockSpec((1, D, H), lambda b, be, nu: (be[b], 0, 0)),
                pl.BlockSpec((1, H, D), lambda b, be, nu: (be[b], 0, 0)),
            ],
            out_specs=pl.BlockSpec(memory_space=pl.ANY),
            scratch_shapes=[
                pltpu.VMEM((D, 2 * H), BF16), pltpu.VMEM((H, D), BF16),
                pltpu.VMEM((EXPERT_IN_BUFS, blk_rows, LANES), F32),
                pltpu.VMEM((EXPERT_OUT_BUFS, blk_rows, LANES), F32),
                pltpu.SemaphoreType.DMA((EXPERT_IN_BUFS,)),
                pltpu.SemaphoreType.DMA((EXPERT_OUT_BUFS,)),
            ],
        ),
        out_shape=jax.ShapeDtypeStruct(xs.shape, F32),
        compiler_params=_cparams(("arbitrary",)),
        name="moe_experts",
    )(blk_e, n_used, xs, w_gate, w_up, w_down)


def _combine_kernel(d_ref, dn_ref, ys_ref, base_ref, rw_ref, g2_ref,
                    b2_ref, o_ref, buf0, buf1, sem0, sem1):
    toks, D = base_ref.shape
    i = pl.program_id(0)
    last = pl.num_programs(0) - 1

    def issue(dref, buf, sem):
        def gather_token(t, c):
            for k in range(TOP_K):
                row = pl.multiple_of(dref[t * TOP_K + k], SUBLANES)
                src = ys_ref.at[pl.ds(row, SUBLANES), :]
                dst = buf.at[pl.ds(pl.multiple_of((k * toks + t) * SUBLANES, SUBLANES), SUBLANES), :]
                pltpu.make_async_copy(src, dst, sem).start(priority=k % DMA_PRIORITIES)
            return c

        lax.fori_loop(0, toks, gather_token, 0)

    def finish(buf, sem):
        pltpu.make_async_copy(ys_ref.at[pl.ds(0, buf.shape[0]), :], buf, sem).wait()
        w = rw_ref[...]
        pieces = []
        for s in range(D // LANES):
            acc = jnp.zeros((toks, LANES), F32)
            for k in range(TOP_K):
                rows = buf[pl.ds(k * toks * SUBLANES + s, toks, stride=SUBLANES), :]
                acc = acc + w[:, k:k + 1] * rows
            pieces.append(acc)
        y = base_ref[...] + jnp.concatenate(pieces, axis=-1)
        o_ref[...] = _layer_norm(y, g2_ref[...], b2_ref[...])

    @pl.when(i == 0)
    def _():
        issue(d_ref, buf0, sem0)

    for parity, (cur, csem, nxt, nsem) in enumerate(((buf0, sem0, buf1, sem1), (buf1, sem1, buf0, sem0))):
        @pl.when(jnp.bitwise_and(i, 1) == parity)
        def _(cur=cur, csem=csem, nxt=nxt, nsem=nsem):
            @pl.when(i < last)
            def _():
                issue(dn_ref, nxt, nsem)
            finish(cur, csem)


def _combine(d_flat, ys, base, rw, g2, b2):
    T, D = base.shape
    toks = min(COMB_ROWS, T)
    steps = T // toks
    idx_now = pl.BlockSpec((toks * TOP_K,), lambda i: (i,), memory_space=pltpu.SMEM)
    idx_next = pl.BlockSpec((toks * TOP_K,), lambda i: (jnp.minimum(i + 1, steps - 1),),
                            memory_space=pltpu.SMEM)
    return pl.pallas_call(
        _combine_kernel,
        grid_spec=pltpu.PrefetchScalarGridSpec(
            num_scalar_prefetch=0,
            grid=(steps,),
            in_specs=[
                idx_now, idx_next,
                pl.BlockSpec(memory_space=pl.ANY),
                pl.BlockSpec((toks, D), lambda i: (i, 0)),
                pl.BlockSpec((toks, TOP_K), lambda i: (i, 0)),
                pl.BlockSpec(g2.shape, lambda i: (0, 0)),
                pl.BlockSpec(b2.shape, lambda i: (0, 0)),
            ],
            out_specs=pl.BlockSpec((toks, D), lambda i: (i, 0)),
            scratch_shapes=[
                pltpu.VMEM((TOP_K * toks * SUBLANES, LANES), F32),
                pltpu.VMEM((TOP_K * toks * SUBLANES, LANES), F32),
                pltpu.SemaphoreType.DMA(()),
                pltpu.SemaphoreType.DMA(()),
            ],
        ),
        out_shape=jax.ShapeDtypeStruct((T, D), F32),
        compiler_params=_cparams(("arbitrary",)),
        name="moe_combine",
    )(d_flat, d_flat, ys, base, rw, g2, b2)


def _overlap_matrix(ncp):
    n = np.arange(ncp)[:, None]
    j = np.arange(LANES)[None, :]
    start = n * CMP_STRIDE
    end = start + CMP_LEN - 1
    sel_start = j * SEL_BLOCK
    ovl = (start < sel_start + SEL_BLOCK) & (end >= sel_start)
    return jnp.asarray(ovl.astype(np.float32), dtype=BF16)


def _block_onehot(seq):
    pos = np.arange(seq)[:, None]
    j = np.arange(LANES)[None, :]
    return jnp.asarray((pos // SEL_BLOCK == j).astype(np.float32), dtype=BF16)


def _mixer(x2, batch, seq, w_in, conv_w, cmp_k, cmp_v):
    c3 = 3 * CONV_CH
    qd = N_HEADS * HEAD_DIM
    w_conv = w_in[:, :c3].astype(BF16)
    w_q = w_in[:, c3:c3 + qd].astype(BF16)
    kv0 = c3 + qd
    part = lambda j: w_in[:, kv0 + j * KV_DIM:kv0 + (j + 1) * KV_DIM]
    zero = jnp.zeros((w_in.shape[0], HEAD_DIM), w_in.dtype)

    def per_group(w):
        return [c for g in range(N_KV_HEADS) for c in (w[:, g * HEAD_DIM:(g + 1) * HEAD_DIM], zero)]

    w_kv = jnp.concatenate([part(0), part(1)] + per_group(part(2)) + per_group(part(4))
                           + per_group(part(3)) + per_group(part(5)), axis=1).astype(BF16)
    w_g = jnp.pad(w_in[:, kv0 + 6 * KV_DIM:], ((0, 0), (0, LANES - N_HEADS * N_BRANCH))).astype(BF16)
    conv_out, q, kv, gates = _proj_conv(x2, w_conv, w_q, w_kv, w_g, conv_w, batch, seq)
    kc, vc = _compress(kv[:, :KV_DIM], kv[:, KV_DIM:2 * KV_DIM], cmp_k, cmp_v, batch, seq)
    ncp = -(-kc.shape[1] // LANES) * LANES
    if ncp != kc.shape[1]:
        padn = ((0, 0), (0, ncp - kc.shape[1]), (0, 0))
        kc, vc = jnp.pad(kc, padn), jnp.pad(vc, padn)
    kv3 = kv.reshape(batch, seq, kv.shape[1])
    nsa_out = _nsa(q, kc, vc, kv3, gates, _overlap_matrix(ncp), _block_onehot(seq), batch, seq)
    return conv_out, nsa_out


def _moe(x3, base, e_t, r_t, w_t, counts, w_gate, w_up, w_down, g2, b2):
    T = base.shape[0]
    A = T * TOP_K
    n_blocks = -(-(A + N_EXPERTS * (SLOT_BLOCK - 1)) // SLOT_BLOCK)
    cnt = counts[:, 0]
    padded = (cnt + SLOT_BLOCK - 1) // SLOT_BLOCK * SLOT_BLOCK
    pad_end = jnp.cumsum(padded)
    pad_start = (pad_end - padded).astype(I32)
    zero_off = jnp.where(padded > 0, pad_end - SLOT_BLOCK, -1).astype(I32)
    n_used = (pad_end[-1:] // SLOT_BLOCK).astype(I32)
    blk_start = jnp.arange(n_blocks, dtype=I32) * SLOT_BLOCK
    last_e = jnp.max(jnp.where(padded > 0, jnp.arange(N_EXPERTS, dtype=I32), 0))
    blk_e = jnp.minimum(jnp.sum((pad_end[None, :] <= blk_start[:, None]).astype(I32), axis=1),
                        last_e).astype(I32)
    d_flat = _slot_rows(pad_start, e_t, r_t).T.reshape(A)
    xs = _push(zero_off, d_flat, x3, n_blocks * SLOT_BLOCK)
    ys = _experts(blk_e, n_used, xs, w_gate, w_up, w_down)
    return _combine(d_flat, ys, base, w_t.T, g2, b2)


def kernel(x, w_in, conv_w, ck_pos, ck_w1, ck_b1, ck_w2, cv_pos, cv_w1, cv_b1, cv_w2, w_out, ln1_g, ln1_b, router_w, router_bias, w_gate, w_up, w_down, ws_gate, ws_up, ws_down, ln2_g, ln2_b):
    batch, seq, D = x.shape
    depth = w_in.shape[0]
    alpha = (2.0 * depth) ** 0.25
    x2 = x.reshape(batch * seq, D)
    for l in range(depth):
        cmp_k = _compress_weights(ck_pos[l], ck_w1[l], ck_b1[l], ck_w2[l])
        cmp_v = _compress_weights(cv_pos[l], cv_w1[l], cv_b1[l], cv_w2[l])
        conv_out, nsa_out = _mixer(x2, batch, seq, w_in[l], conv_w[l], cmp_k, cmp_v)
        rw_hi, rw_lo = _split_bf16(router_w[l].T)
        rbias = jnp.broadcast_to(router_bias[l][:, None], (N_EXPERTS, LANES))
        x3, base, e_t, r_t, w_t, counts = _post(
            x2, conv_out, nsa_out, w_out[l].astype(BF16), ln1_g[l][None, :], ln1_b[l][None, :],
            rw_hi, rw_lo, rbias,
            ws_gate[l].astype(BF16), ws_up[l].astype(BF16), ws_down[l].astype(BF16), alpha)
        x2 = _moe(x3, base, e_t, r_t, w_t, counts, w_gate[l], w_up[l], w_down[l],
                  ln2_g[l][None, :], ln2_b[l][None, :])
    return x2.reshape(batch, seq, D)
```

```python
import functools
import math

import jax
import jax.numpy as jnp
import numpy as np
from jax import lax
from jax.experimental import pallas as pl
from jax.experimental.pallas import tpu as pltpu

F32 = jnp.float32
BF16 = jnp.bfloat16
I32 = jnp.int32

CONV_CH = 512
CONV_WIDTH = 3
N_HEADS = 8
HEAD_DIM = 64
N_KV_HEADS = 2
Q_PER_KV = N_HEADS // N_KV_HEADS
KV_DIM = N_KV_HEADS * HEAD_DIM
N_BRANCH = 3
CMP_LEN = 32
CMP_STRIDE = 16
SEL_BLOCK = 64
SEL_TOPK = 8
WINDOW = 512
FORCED_SCORE = 1e4
N_EXPERTS = 256
TOP_K = 8
N_GROUPS = 8
TOPK_GROUPS = 4
GROUP_SIZE = N_EXPERTS // N_GROUPS
ROUTED_SCALE = 2.5
LN_EPS = 1e-5
NEG_INF = -1e30
SEL_SHIFT = SEL_BLOCK.bit_length() - 1
GROUP_SHIFT = GROUP_SIZE.bit_length() - 1
TOPK_SHIFT = TOP_K.bit_length() - 1

LANES = 128
SUBLANES = 8
VMEM_LIMIT = 40 * 1024 * 1024
DMA_PRIORITIES = 2

PROJ_ROWS = 1024
NSA_Q = 512
NSA_WIN_Q = 128
NSA_KC = 512
POST_ROWS = 512
SLOT_BLOCK = 512
SLOT_PART = 128
EXPERT_IN_BUFS = 4
EXPERT_OUT_BUFS = 2
PUSH_ROWS = 1024
COMB_ROWS = 256


def _dot(a, b):
    return jnp.dot(a, b, preferred_element_type=F32)


def _dot_t(a, b):
    return lax.dot_general(a, b, (((1,), (1,)), ((), ())), preferred_element_type=F32)


def _split_bf16(x):
    hi = x.astype(BF16)
    lo = (x - hi.astype(F32)).astype(BF16)
    return hi, lo


def _cparams(sem):
    return pltpu.CompilerParams(dimension_semantics=sem, vmem_limit_bytes=VMEM_LIMIT)


def _proj_conv_kernel(x_ref, wc_ref, wq_ref, wkv_ref, wg_ref, cw_ref,
                      conv_ref, q_ref, kv_ref, gate_ref, carry_ref):
    rows = x_ref.shape[0]

    @pl.when(pl.program_id(1) == 0)
    def _():
        carry_ref[...] = jnp.zeros_like(carry_ref)

    xb = x_ref[...].astype(BF16)
    acc = _dot(xb, wc_ref[...])
    b_g = acc[:, :CONV_CH]
    u = acc[:, CONV_CH:2 * CONV_CH] * acc[:, 2 * CONV_CH:]
    prev2 = carry_ref[SUBLANES - 2:SUBLANES - 1, :]
    prev1 = carry_ref[SUBLANES - 1:SUBLANES, :]
    ri = lax.broadcasted_iota(I32, (rows, 1), 0)
    u1 = jnp.where(ri == 0, prev1, pltpu.roll(u, 1, 0))
    u2 = jnp.where(ri == 0, prev2, jnp.where(ri == 1, prev1, pltpu.roll(u, 2, 0)))
    y = cw_ref[0:1, :] * u2 + cw_ref[1:2, :] * u1 + cw_ref[2:3, :] * u
    conv_ref[...] = (b_g * y).astype(BF16)
    carry_ref[...] = u[rows - SUBLANES:, :]

    q_ref[...] = (_dot(xb, wq_ref[...]) * (HEAD_DIM ** -0.5)).astype(BF16)
    gate_ref[...] = jax.nn.sigmoid(_dot(xb, wg_ref[...]))

    kv = _dot(xb, wkv_ref[...])
    n_plain, n_key = 2, 2 * N_KV_HEADS
    key_w = n_key * LANES
    pos = pl.program_id(1) * rows + lax.broadcasted_iota(I32, (rows, key_w), 0)
    l128 = jnp.bitwise_and(lax.broadcasted_iota(I32, (rows, key_w), 1), LANES - 1)
    feat = jnp.where(l128 == HEAD_DIM, jnp.right_shift(pos, SEL_SHIFT),
                     jnp.where(l128 == HEAD_DIM + 1, jnp.bitwise_and(pos, SEL_BLOCK - 1), 0))
    ones = jnp.where(l128 >= HEAD_DIM, 1.0, 0.0)
    k0, v0 = n_plain * LANES, n_plain * LANES + key_w
    kv_ref[:, :k0] = kv[:, :k0].astype(BF16)
    kv_ref[:, k0:v0] = (kv[:, k0:v0] + feat.astype(F32)).astype(BF16)
    kv_ref[:, v0:] = (kv[:, v0:] + ones).astype(BF16)


def _proj_conv(x2, w_conv, w_q, w_kv, w_g, conv_w, batch, seq):
    T, D = x2.shape
    rows = min(PROJ_ROWS, seq)
    nt = seq // rows
    row_map = lambda b, i: (b * nt + i, 0)
    fixed = lambda b, i: (0, 0)
    return pl.pallas_call(
        _proj_conv_kernel,
        grid=(batch, nt),
        in_specs=[
            pl.BlockSpec((rows, D), row_map),
            pl.BlockSpec(w_conv.shape, fixed),
            pl.BlockSpec(w_q.shape, fixed),
            pl.BlockSpec(w_kv.shape, fixed),
            pl.BlockSpec(w_g.shape, fixed),
            pl.BlockSpec(conv_w.shape, fixed),
        ],
        out_specs=[
            pl.BlockSpec((rows, CONV_CH), row_map),
            pl.BlockSpec((rows, N_HEADS * HEAD_DIM), row_map),
            pl.BlockSpec((rows, w_kv.shape[1]), row_map),
            pl.BlockSpec((rows, LANES), row_map),
        ],
        out_shape=[
            jax.ShapeDtypeStruct((T, CONV_CH), BF16),
            jax.ShapeDtypeStruct((T, N_HEADS * HEAD_DIM), BF16),
            jax.ShapeDtypeStruct((T, w_kv.shape[1]), BF16),
            jax.ShapeDtypeStruct((T, LANES), F32),
        ],
        scratch_shapes=[pltpu.VMEM((SUBLANES, CONV_CH), F32)],
        compiler_params=_cparams(("arbitrary", "arbitrary")),
        name="proj_conv",
    )(x2, w_conv, w_q, w_kv, w_g, conv_w)


def _compress_kernel(ck_ref, cv_ref, wtk_ref, wbk_ref, w2k_ref, ptk_ref, pbk_ref, b1k_ref,
                     wtv_ref, wbv_ref, w2v_ref, ptv_ref, pbv_ref, b1v_ref, kc_ref, vc_ref):
    def one(c_ref, wt_ref, wb_ref, w2_ref, pt_ref, pb_ref, b1_ref, o_ref):
        c = c_ref[0]
        top = _dot(c, wt_ref[...])
        bot = _dot(c, wb_ref[...])
        c0 = _dot(pt_ref[...], wt_ref[...]) + _dot(pb_ref[...], wb_ref[...]) + b1_ref[...]
        n = top.shape[0]
        h = top + pltpu.roll(bot, n - 1, 0) + c0[0:1, :]
        g = jax.nn.gelu(h, approximate=True)
        o_ref[0] = _dot(g.astype(BF16), w2_ref[...]).astype(BF16)

    one(ck_ref, wtk_ref, wbk_ref, w2k_ref, ptk_ref, pbk_ref, b1k_ref, kc_ref)
    one(cv_ref, wtv_ref, wbv_ref, w2v_ref, ptv_ref, pbv_ref, b1v_ref, vc_ref)


def _blockdiag2(w):
    z = jnp.zeros_like(w)
    return jnp.concatenate([jnp.concatenate([w, z], 1), jnp.concatenate([z, w], 1)], 0)


def _compress_weights(pos, w1, b1, w2):
    w1r = w1.reshape(CMP_LEN, HEAD_DIM, HEAD_DIM)
    eye = jnp.eye(N_KV_HEADS, dtype=w1.dtype)
    wfull = (w1r[:, None, :, None, :] * eye[None, :, None, :, None]).reshape(CMP_LEN, KV_DIM, KV_DIM)
    w_top = wfull[:CMP_STRIDE].reshape(CMP_STRIDE * KV_DIM, KV_DIM).astype(BF16)
    w_bot = wfull[CMP_STRIDE:].reshape(CMP_STRIDE * KV_DIM, KV_DIM).astype(BF16)
    posr = jnp.tile(pos, (1, N_KV_HEADS))
    pos_top = jnp.tile(posr[:CMP_STRIDE].reshape(1, -1), (SUBLANES, 1)).astype(BF16)
    pos_bot = jnp.tile(posr[CMP_STRIDE:].reshape(1, -1), (SUBLANES, 1)).astype(BF16)
    b1r = jnp.tile(b1[None, :], (SUBLANES, N_KV_HEADS)).astype(F32)
    return w_top, w_bot, _blockdiag2(w2).astype(BF16), pos_top, pos_bot, b1r


def _compress(kc_raw, vc_raw, wk, wv, batch, seq):
    chunks = seq // CMP_STRIDE
    width = CMP_STRIDE * KV_DIM
    ck = kc_raw.reshape(batch, chunks, width)
    cv = vc_raw.reshape(batch, chunks, width)
    bmap = lambda b: (b, 0, 0)
    fixed = lambda b: (0, 0)
    wspecs = [pl.BlockSpec(w.shape, fixed) for w in wk]
    return pl.pallas_call(
        _compress_kernel,
        grid=(batch,),
        in_specs=[pl.BlockSpec((1, chunks, width), bmap), pl.BlockSpec((1, chunks, width), bmap)]
        + wspecs + wspecs,
        out_specs=[pl.BlockSpec((1, chunks, KV_DIM), bmap)] * 2,
        out_shape=[jax.ShapeDtypeStruct((batch, chunks, KV_DIM), BF16)] * 2,
        compiler_params=_cparams(("arbitrary",)),
        name="compress",
    )(ck, cv, *wk, *wv)


def _softmax_rows(s, valid):
    s = jnp.where(valid, s, NEG_INF)
    m = jnp.max(s, axis=-1, keepdims=True)
    p = jnp.where(valid, jnp.exp(s - m), 0.0)
    l = jnp.sum(p, axis=-1, keepdims=True)
    inv = jnp.where(l > 0.0, 1.0 / l, 0.0)
    return p, inv


def _nsa_kernel(q_ref, kc_ref, vc_ref, ks0_ref, ks1_ref, kw0_ref, kw1_ref, vs0_ref, vs1_ref,
                vw0_ref, vw1_ref, gate_ref, ovl_ref, hot_ref, o_ref, *, seq, n_sel):
    ks_refs, kw_refs = (ks0_ref, ks1_ref), (kw0_ref, kw1_ref)
    vs_refs, vw_refs = (vs0_ref, vs1_ref), (vw0_ref, vw1_ref)
    tq = q_ref.shape[0]
    ncp = kc_ref.shape[1]
    rows = Q_PER_KV * tq
    q0 = pl.program_id(1) * tq
    t_col = q0 + lax.broadcasted_iota(I32, (tq, 1), 0)
    t4 = jnp.concatenate([t_col] * Q_PER_KV, axis=0)
    row_i = lax.broadcasted_iota(I32, (rows, 1), 0)
    lane = lax.broadcasted_iota(I32, (1, LANES), 1)
    feat_lane = lax.broadcasted_iota(I32, (1, HEAD_DIM), 1)
    n_sel_pad = -(-n_sel // SUBLANES) * SUBLANES
    blk_row = lax.broadcasted_iota(I32, (n_sel_pad, tq), 0)
    blk_row_f = blk_row.astype(F32)
    gates = gate_ref[...]
    n_chunks = (q0 + tq + NSA_KC - 1) // NSA_KC
    outs, stage = [], []
    for g in range(N_KV_HEADS):
        lo, hi = g * HEAD_DIM, (g + 1) * HEAD_DIM
        qg = jnp.concatenate(
            [q_ref[:, (g * Q_PER_KV + r) * HEAD_DIM:(g * Q_PER_KV + r + 1) * HEAD_DIM]
             for r in range(Q_PER_KV)], axis=0)
        slope = jnp.zeros((rows, 1), F32)
        for r in range(Q_PER_KV):
            h = g * Q_PER_KV + r
            in_head = (row_i >= r * tq) & (row_i < (r + 1) * tq)
            slope = jnp.where(in_head, 2.0 ** (-8.0 * (h + 1) / N_HEADS), slope)

        cmp_end = lax.broadcasted_iota(I32, (1, ncp), 1) * CMP_STRIDE + (CMP_LEN - 1)
        d_c = t4 - cmp_end
        s_c = _dot_t(qg, kc_ref[0, :, lo:hi]) - slope * d_c.astype(F32)
        p_c, inv_c = _softmax_rows(s_c, d_c >= 0)
        p_c = p_c * inv_c
        o_cmp = _dot(p_c.astype(BF16), vc_ref[0, :, lo:hi])

        ps = p_c[0:tq]
        for r in range(1, Q_PER_KV):
            ps = ps + p_c[r * tq:(r + 1) * tq]
        ps_hi, ps_lo = _split_bf16(ps)
        imp = _dot(ps_hi, ovl_ref[...]) + _dot(ps_lo, ovl_ref[...])
        forced = (lane == 0) | (lane == jnp.right_shift(t_col, SEL_SHIFT))
        causal = lane * SEL_BLOCK <= t_col
        score = jnp.where(forced, FORCED_SCORE, jnp.where(causal, imp, -1.0))
        score_t = score.T[:n_sel_pad, :]
        score_t = jnp.where(blk_row < n_sel, score_t, -jnp.inf)
        sel_t = jnp.zeros((n_sel_pad, tq), F32)
        for _ in range(min(SEL_TOPK, n_sel)):
            mx = jnp.max(score_t, axis=0, keepdims=True)
            first = jnp.min(jnp.where(score_t == mx, blk_row_f, float(LANES)), axis=0, keepdims=True)
            hit = blk_row_f == first
            sel_t = jnp.where(hit, 1.0, sel_t)
            score_t = jnp.where(hit, -jnp.inf, score_t)
        unsel_t = jnp.concatenate(
            [jnp.where(sel_t > 0.5, 0.0, NEG_INF), jnp.full((LANES - n_sel_pad, tq), NEG_INF, F32)], axis=0)
        unsel_b = unsel_t.T.astype(BF16)

        q_feat = jnp.where(feat_lane == 0, slope * float(SEL_BLOCK),
                           jnp.where(feat_lane == 1, slope, 0.0)).astype(BF16)
        q_aug = jnp.concatenate([qg, q_feat], axis=1)
        lhs = jnp.concatenate([q_aug, jnp.concatenate([unsel_b] * Q_PER_KV, axis=0)], axis=1)

        stage.append((q_aug, lhs, o_cmp))

    def sel_chunk(c, carry, diagonal):
        k0 = pl.multiple_of(c * NSA_KC, NSA_KC)
        hot = hot_ref[pl.ds(k0, NSA_KC), :]
        new = []
        for g in range(N_KV_HEADS):
            m, acc = carry[g]
            kch = ks_refs[g][0, pl.ds(k0, NSA_KC), :]
            s = _dot_t(stage[g][1], jnp.concatenate([kch, hot], axis=1))
            if diagonal:
                pos = k0 + lax.broadcasted_iota(I32, (1, NSA_KC), 1)
                s = jnp.where(pos <= t4, s, NEG_INF)
            m_new = jnp.maximum(m, jnp.max(s, axis=-1, keepdims=True))
            a = jnp.exp(m - m_new)
            p = jnp.exp((s - m_new).astype(BF16))
            acc = a * acc + _dot(p, vs_refs[g][0, pl.ds(k0, NSA_KC), :])
            new.append((m_new, acc))
        return tuple(new)

    init = tuple((jnp.full((rows, 1), NEG_INF, F32), jnp.zeros((rows, 2 * HEAD_DIM), F32))
                 for _ in range(N_KV_HEADS))
    carry = lax.fori_loop(0, n_chunks - 1, functools.partial(sel_chunk, diagonal=False), init)
    wq = min(NSA_WIN_Q, tq)

    def head_rows(a, u):
        return jnp.concatenate([a[r * tq + u * wq:r * tq + (u + 1) * wq] for r in range(Q_PER_KV)], axis=0)

    final = sel_chunk(n_chunks - 1, carry, diagonal=True)

    win_len = WINDOW + wq
    win_out = [[[] for _ in range(Q_PER_KV)] for _ in range(N_KV_HEADS)]
    for u in range(tq // wq):
        qs = q0 + u * wq
        w_start = pl.multiple_of(jnp.maximum(qs - WINDOW, 0), wq)
        d_w = (qs + lax.broadcasted_iota(I32, (wq, 1), 0)
               - (w_start + lax.broadcasted_iota(I32, (1, win_len), 1)))
        band = jnp.where((d_w >= 0) & (d_w < WINDOW), 0.0, NEG_INF)
        band4 = jnp.concatenate([band] * Q_PER_KV, axis=0)
        for g in range(N_KV_HEADS):
            s_w = _dot_t(head_rows(stage[g][0], u), kw_refs[g][0, pl.ds(w_start, win_len), :]) + band4
            m_w = jnp.max(s_w, axis=-1, keepdims=True)
            p_w = jnp.exp((s_w - m_w).astype(BF16))
            acc_w = _dot(p_w, vw_refs[g][0, pl.ds(w_start, win_len), :])
            o_sub = acc_w[:, :HEAD_DIM] * (1.0 / acc_w[:, HEAD_DIM:])
            for r in range(Q_PER_KV):
                win_out[g][r].append(o_sub[r * wq:(r + 1) * wq])

    for g in range(N_KV_HEADS):
        o_cmp = stage[g][2]
        acc_s = final[g][1]
        o_slc = acc_s[:, :HEAD_DIM] * (1.0 / acc_s[:, HEAD_DIM:])
        for r in range(Q_PER_KV):
            h = g * Q_PER_KV + r
            sl = slice(r * tq, (r + 1) * tq)
            gc = gates[:, h * N_BRANCH:h * N_BRANCH + 1]
            gs = gates[:, h * N_BRANCH + 1:h * N_BRANCH + 2]
            gw = gates[:, h * N_BRANCH + 2:h * N_BRANCH + 3]
            o_win = jnp.concatenate(win_out[g][r], axis=0)
            outs.append(gc * o_cmp[sl] + gs * o_slc[sl] + gw * o_win)
    o_ref[...] = jnp.concatenate(outs, axis=-1).astype(BF16)


def _nsa(q, kc, vc, kv3, gates, overlap, blk_onehot, batch, seq):
    T = q.shape[0]
    tq = min(NSA_Q, seq)
    assert NSA_KC % tq == 0 and seq % NSA_KC == 0 and tq % min(NSA_WIN_Q, tq) == 0
    assert seq >= WINDOW + min(NSA_WIN_Q, tq)
    nq = seq // tq
    ncp = kc.shape[1]
    n_sel = seq // SEL_BLOCK
    row_map = lambda b, i: (b * nq + i, 0)
    bmap = lambda b, i: (b, 0, 0)
    fixed = lambda b, i: (0, 0)
    kvspec = lambda j: pl.BlockSpec((1, seq, LANES), lambda b, i, j=j: (b, 0, j))
    n_kv = 4 * N_KV_HEADS
    return pl.pallas_call(
        functools.partial(_nsa_kernel, seq=seq, n_sel=n_sel),
        grid=(batch, nq),
        in_specs=[
            pl.BlockSpec((tq, N_HEADS * HEAD_DIM), row_map),
            pl.BlockSpec((1, ncp, KV_DIM), bmap),
            pl.BlockSpec((1, ncp, KV_DIM), bmap),
        ] + [kvspec(2 + j) for j in range(n_kv)] + [
            pl.BlockSpec((tq, LANES), row_map),
            pl.BlockSpec(overlap.shape, fixed),
            pl.BlockSpec(blk_onehot.shape, fixed),
        ],
        out_specs=pl.BlockSpec((tq, N_HEADS * HEAD_DIM), row_map),
        out_shape=jax.ShapeDtypeStruct((T, N_HEADS * HEAD_DIM), BF16),
        compiler_params=_cparams(("arbitrary", "arbitrary")),
        name="nsa",
    )(q, kc, vc, *([kv3] * n_kv), gates, overlap, blk_onehot)


def _layer_norm(y, g, b):
    mu = jnp.mean(y, axis=-1, keepdims=True)
    yc = y - mu
    var = jnp.mean(yc * yc, axis=-1, keepdims=True)
    return yc * lax.rsqrt(var + LN_EPS) * g + b


def _post_kernel(x_ref, conv_ref, nsa_ref, wo_ref, g1_ref, b1_ref, rwh_ref, rwl_ref, rb_ref,
                 wsg_ref, wsu_ref, wsd_ref,
                 x3_ref, base_ref, e_ref, r_ref, w_ref, cnt_ref, carry_ref, *, alpha):
    rows, D = x_ref.shape

    @pl.when(pl.program_id(0) == 0)
    def _():
        carry_ref[...] = jnp.zeros_like(carry_ref)

    half = wo_ref.shape[0] // 2
    mix = _dot(conv_ref[...], wo_ref[:half, :]) + _dot(nsa_ref[...], wo_ref[half:, :])
    x1 = _layer_norm(alpha * x_ref[...] + mix, g1_ref[...], b1_ref[...])
    for s in range(D // LANES):
        x3_ref[pl.ds(s, rows, stride=SUBLANES), :] = x1[:, s * LANES:(s + 1) * LANES]

    xh, xl = _split_bf16(x1)
    hid = jax.nn.silu(_dot(xh, wsg_ref[...])) * _dot(xh, wsu_ref[...])
    base_ref[...] = alpha * x1 + _dot(hid.astype(BF16), wsd_ref[...])

    logits = (_dot_t(rwh_ref[...], xh) + _dot_t(rwh_ref[...], xl) + _dot_t(rwl_ref[...], xh))
    scores = jax.nn.sigmoid(logits)
    reps = rows // LANES
    biased = scores + jnp.concatenate([rb_ref[...]] * reps, axis=1)
    eidx = lax.broadcasted_iota(I32, (N_EXPERTS, rows), 0).astype(F32)
    gidx = lax.broadcasted_iota(I32, (GROUP_SIZE, rows), 0).astype(F32)
    gvals, gscore = [], []
    for gi in range(N_GROUPS):
        v = biased[gi * GROUP_SIZE:(gi + 1) * GROUP_SIZE, :]
        m1 = jnp.max(v, axis=0, keepdims=True)
        i1 = jnp.min(jnp.where(v == m1, gidx, float(GROUP_SIZE)), axis=0, keepdims=True)
        m2 = jnp.max(jnp.where(gidx == i1, -jnp.inf, v), axis=0, keepdims=True)
        gvals.append(v)
        gscore.append(m1 + m2)
    cands = []
    for gi in range(N_GROUPS):
        ahead = jnp.zeros((1, rows), F32)
        for gj in range(N_GROUPS):
            if gj == gi:
                continue
            beats = (gscore[gj] >= gscore[gi]) if gj < gi else (gscore[gj] > gscore[gi])
            ahead = ahead + jnp.where(beats, 1.0, 0.0)
        ahead_full = jnp.broadcast_to(ahead, gvals[gi].shape)
        cands.append(jnp.where(ahead_full < float(TOPK_GROUPS), gvals[gi], NEG_INF))
    cand = jnp.concatenate(cands, axis=0)
    onehot = jnp.zeros((N_EXPERTS, rows), F32)
    idx_rows, w_rows = [], []
    for _ in range(TOP_K):
        mx = jnp.max(cand, axis=0, keepdims=True)
        first = jnp.min(jnp.where(cand == mx, eidx, float(N_EXPERTS)), axis=0, keepdims=True)
        hit = eidx == first
        idx_rows.append(first)
        w_rows.append(jnp.sum(jnp.where(hit, scores, 0.0), axis=0, keepdims=True))
        onehot = jnp.where(hit, 1.0, onehot)
        cand = jnp.where(hit, -jnp.inf, cand)
    wsum = w_rows[0]
    for k in range(1, TOP_K):
        wsum = wsum + w_rows[k]

    ti = lax.broadcasted_iota(I32, (rows, rows), 0)
    tj = lax.broadcasted_iota(I32, (rows, rows), 1)
    earlier = jnp.where(ti < tj, 1.0, 0.0).astype(BF16)
    carry = carry_ref[...]
    before = _dot(onehot.astype(BF16), earlier) + jnp.concatenate([carry] * reps, axis=1)
    krow = lax.broadcasted_iota(I32, (TOP_K, rows), 0)
    e_out = jnp.zeros((TOP_K, rows), F32)
    r_out = jnp.zeros((TOP_K, rows), F32)
    w_out = jnp.zeros((TOP_K, rows), F32)
    for k in range(TOP_K):
        rank = jnp.sum(jnp.where(eidx == idx_rows[k], before, 0.0), axis=0, keepdims=True)
        e_out = jnp.where(krow == k, idx_rows[k], e_out)
        r_out = jnp.where(krow == k, rank, r_out)
        w_out = jnp.where(krow == k, w_rows[k] / wsum * ROUTED_SCALE, w_out)
    e_ref[...] = e_out.astype(I32)
    r_ref[...] = r_out.astype(I32)
    w_ref[...] = w_out
    total = carry + jnp.sum(onehot, axis=1, keepdims=True)
    carry_ref[...] = total
    cnt_ref[...] = total.astype(I32)


def _post(x2, conv_out, nsa_out, w_out, g1, b1, rw_hi, rw_lo, rbias, wsg, wsu, wsd, alpha):
    T, D = x2.shape
    rows = min(POST_ROWS, T)
    row_map = lambda i: (i, 0)
    col_map = lambda i: (0, i)
    fixed = lambda i: (0, 0)
    full = lambda a: pl.BlockSpec(a.shape, fixed)
    return pl.pallas_call(
        functools.partial(_post_kernel, alpha=alpha),
        grid=(T // rows,),
        in_specs=[
            pl.BlockSpec((rows, D), row_map),
            pl.BlockSpec((rows, CONV_CH), row_map),
            pl.BlockSpec((rows, N_HEADS * HEAD_DIM), row_map),
            full(w_out), full(g1), full(b1), full(rw_hi), full(rw_lo), full(rbias),
            full(wsg), full(wsu), full(wsd),
        ],
        out_specs=[
            pl.BlockSpec((rows * SUBLANES, LANES), row_map),
            pl.BlockSpec((rows, D), row_map),
            pl.BlockSpec((TOP_K, rows), col_map),
            pl.BlockSpec((TOP_K, rows), col_map),
            pl.BlockSpec((TOP_K, rows), col_map),
            pl.BlockSpec((N_EXPERTS, LANES), fixed),
        ],
        out_shape=[
            jax.ShapeDtypeStruct((T * SUBLANES, LANES), F32),
            jax.ShapeDtypeStruct((T, D), F32),
            jax.ShapeDtypeStruct((TOP_K, T), I32),
            jax.ShapeDtypeStruct((TOP_K, T), I32),
            jax.ShapeDtypeStruct((TOP_K, T), F32),
            jax.ShapeDtypeStruct((N_EXPERTS, LANES), I32),
        ],
        scratch_shapes=[pltpu.VMEM((N_EXPERTS, LANES), F32)],
        compiler_params=_cparams(("arbitrary",)),
        name="post_attn_router",
    )(x2, conv_out, nsa_out, w_out, g1, b1, rw_hi, rw_lo, rbias, wsg, wsu, wsd)


def _slot_rows_kernel(pstart_ref, e_ref, r_ref, o_ref):
    e = e_ref[...]

    def add_expert(j, acc):
        return acc + jnp.where(e == j, pstart_ref[j], 0)

    o_ref[...] = lax.fori_loop(0, N_EXPERTS, add_expert, r_ref[...]) * SUBLANES


def _slot_rows(pad_start, e_t, r_t):
    T = e_t.shape[1]
    cols = min(2048, T)
    spec = pl.BlockSpec((TOP_K, cols), lambda i, *_: (0, i))
    return pl.pallas_call(
        _slot_rows_kernel,
        grid_spec=pltpu.PrefetchScalarGridSpec(
            num_scalar_prefetch=1, grid=(T // cols,), in_specs=[spec, spec], out_specs=spec),
        out_shape=jax.ShapeDtypeStruct(e_t.shape, I32),
        compiler_params=_cparams(("arbitrary",)),
        name="moe_slot_rows",
    )(pad_start, e_t, r_t)


def _push_kernel(zoff_ref, d_ref, x3_ref, xs_ref, zero_ref, sem, zsem):
    toks = x3_ref.shape[0] // SUBLANES
    zrows = zero_ref.shape[0]

    def zero_copy(e):
        off = pl.multiple_of(zoff_ref[e] * SUBLANES, SUBLANES)
        return pltpu.make_async_copy(zero_ref, xs_ref.at[pl.ds(off, zrows), :], zsem)

    @pl.when(pl.program_id(0) == 0)
    def _():
        zero_ref[...] = jnp.zeros_like(zero_ref)

        def start(e, c):
            @pl.when(zoff_ref[e] >= 0)
            def _():
                zero_copy(e).start()
            return c

        def wait(e, c):
            @pl.when(zoff_ref[e] >= 0)
            def _():
                zero_copy(e).wait()
            return c

        lax.fori_loop(0, N_EXPERTS, start, 0)
        lax.fori_loop(0, N_EXPERTS, wait, 0)

    def push_token(t, c):
        src = x3_ref.at[pl.ds(pl.multiple_of(t * SUBLANES, SUBLANES), SUBLANES), :]
        for k in range(TOP_K):
            row = pl.multiple_of(d_ref[t * TOP_K + k], SUBLANES)
            dst = xs_ref.at[pl.ds(row, SUBLANES), :]
            pltpu.make_async_copy(src, dst, sem).start(priority=k % DMA_PRIORITIES)
        return c

    lax.fori_loop(0, toks, push_token, 0)
    for _ in range(TOP_K):
        pltpu.make_async_copy(x3_ref, xs_ref.at[pl.ds(0, toks * SUBLANES), :], sem).wait()


def _push(zero_off, d_flat, x3, n_slots):
    T = x3.shape[0] // SUBLANES
    toks = min(PUSH_ROWS, T)
    return pl.pallas_call(
        _push_kernel,
        grid_spec=pltpu.PrefetchScalarGridSpec(
            num_scalar_prefetch=1,
            grid=(T // toks,),
            in_specs=[
                pl.BlockSpec((toks * TOP_K,), lambda i, *_: (i,), memory_space=pltpu.SMEM),
                pl.BlockSpec((toks * SUBLANES, LANES), lambda i, *_: (i, 0)),
            ],
            out_specs=pl.BlockSpec(memory_space=pl.ANY),
            scratch_shapes=[
                pltpu.VMEM((SLOT_PART * SUBLANES, LANES), F32),
                pltpu.SemaphoreType.DMA(()),
                pltpu.SemaphoreType.DMA(()),
            ],
        ),
        out_shape=jax.ShapeDtypeStruct((n_slots * SUBLANES, LANES), F32),
        compiler_params=_cparams(("arbitrary",)),
        name="moe_push",
    )(zero_off, d_flat, x3)


class _PartCopies:
    def __init__(self, copies, n_parts):
        self.copies, self.n_parts = copies, n_parts

    def start(self):
        for q, c in enumerate(self.copies):
            pl.when(q < self.n_parts)(c.start)

    def wait(self):
        for q, c in enumerate(self.copies):
            pl.when(q < self.n_parts)(c.wait)


def _expert_kernel(blk_e_ref, nused_ref, nparts_ref, xs_hbm, wg_ref, wu_ref, wd_ref, ys_hbm,
                   wgu_s, wd_s, xbuf, ybuf, xsem, ysem):
    b = pl.program_id(0)
    n_used = nused_ref[0]
    blk_rows = xbuf.shape[1]
    part_rows = SLOT_PART * SUBLANES
    rows = blk_rows // SUBLANES
    D = wg_ref.shape[1]
    H = wg_ref.shape[2]

    def part_windows(blk):
        return [pl.ds(pl.multiple_of(blk * blk_rows + q * part_rows, part_rows), part_rows)
                for q in range(blk_rows // part_rows)]

    def x_copy(blk, slot):
        return _PartCopies(
            [pltpu.make_async_copy(xs_hbm.at[w, :], xbuf.at[slot].at[pl.ds(q * part_rows, part_rows), :],
                                   xsem.at[slot]) for q, w in enumerate(part_windows(blk))],
            nparts_ref[blk])

    def y_copy(blk, slot):
        return _PartCopies(
            [pltpu.make_async_copy(ybuf.at[slot].at[pl.ds(q * part_rows, part_rows), :], ys_hbm.at[w, :],
                                   ysem.at[slot]) for q, w in enumerate(part_windows(blk))],
            nparts_ref[blk])

    @pl.when(b == 0)
    def _():
        xbuf[...] = jnp.zeros_like(xbuf)
        for j in range(EXPERT_IN_BUFS - 1):
            @pl.when(j < n_used)
            def _(j=j):
                x_copy(jnp.int32(j), j).start()

    prev = blk_e_ref[jnp.maximum(b - 1, 0)]

    @pl.when((b == 0) | (blk_e_ref[b] != prev))
    def _():
        wgu_s[:, :H] = wg_ref[0].astype(BF16)
        wgu_s[:, H:] = wu_ref[0].astype(BF16)
        wd_s[...] = wd_ref[0].astype(BF16)

    @pl.when(b < n_used)
    def _():
        ahead = b + (EXPERT_IN_BUFS - 1)

        @pl.when(ahead < n_used)
        def _():
            x_copy(ahead, jnp.bitwise_and(ahead, EXPERT_IN_BUFS - 1)).start()

        slot = jnp.bitwise_and(b, EXPERT_IN_BUFS - 1)
        x_copy(b, slot).wait()
        xv = xbuf.at[slot]
        xb = jnp.concatenate(
            [xv[pl.ds(s, rows, stride=SUBLANES), :].astype(BF16) for s in range(D // LANES)],
            axis=-1)
        h = _dot(xb, wgu_s[...])
        act = (jax.nn.silu(h[:, :H]) * h[:, H:]).astype(BF16)
        out = _dot(act, wd_s[...])

        yslot = jnp.bitwise_and(b, EXPERT_OUT_BUFS - 1)

        @pl.when(b >= EXPERT_OUT_BUFS)
        def _():
            y_copy(b - EXPERT_OUT_BUFS, yslot).wait()

        yv = ybuf.at[yslot]
        for s in range(D // LANES):
            yv[pl.ds(s, rows, stride=SUBLANES), :] = out[:, s * LANES:(s + 1) * LANES]
        y_copy(b, yslot).start()

    @pl.when(b == pl.num_programs(0) - 1)
    def _():
        for j in range(EXPERT_OUT_BUFS):
            blk = n_used - 1 - j

            @pl.when(blk >= 0)
            def _(blk=blk):
                y_copy(blk, jnp.bitwise_and(blk, EXPERT_OUT_BUFS - 1)).wait()


def _experts(blk_e, n_used, blk_parts, xs, w_gate, w_up, w_down):
    n_blocks = blk_e.shape[0]
    E, D, H = w_gate.shape
    blk_rows = SLOT_BLOCK * SUBLANES
    return pl.pallas_call(
        _expert_kernel,
        grid_spec=pltpu.PrefetchScalarGridSpec(
            num_scalar_prefetch=3,
            grid=(n_blocks,),
            in_specs=[
                pl.BlockSpec(memory_space=pl.ANY),
                pl.BlockSpec((1, D, H), lambda b, be, nu, bp: (be[b], 0, 0)),
                pl.BlockSpec((1, D, H), lambda b, be, nu, bp: (be[b], 0, 0)),
                pl.BlockSpec((1, H, D), lambda b, be, nu, bp: (be[b], 0, 0)),
            ],
            out_specs=pl.BlockSpec(memory_space=pl.ANY),
            scratch_shapes=[
                pltpu.VMEM((D, 2 * H), BF16), pltpu.VMEM((H, D), BF16),
                pltpu.VMEM((EXPERT_IN_BUFS, blk_rows, LANES), F32),
                pltpu.VMEM((EXPERT_OUT_BUFS, blk_rows, LANES), F32),
                pltpu.SemaphoreType.DMA((EXPERT_IN_BUFS,)),
                pltpu.SemaphoreType.DMA((EXPERT_OUT_BUFS,)),
            ],
        ),
        out_shape=jax.ShapeDtypeStruct(xs.shape, F32),
        compiler_params=_cparams(("arbitrary",)),
        name="moe_experts",
    )(blk_e, n_used, blk_parts, xs, w_gate, w_up, w_down)


def _combine_kernel(d_ref, dn_ref, ys_ref, base_ref, rw_ref, g2_ref,
                    b2_ref, o_ref, buf0, buf1, sem0, sem1):
    toks, D = base_ref.shape
    i = pl.program_id(0)
    last = pl.num_programs(0) - 1

    def issue(dref, buf, sem):
        def gather_token(t, c):
            for k in range(TOP_K):
                row = pl.multiple_of(dref[t * TOP_K + k], SUBLANES)
                src = ys_ref.at[pl.ds(row, SUBLANES), :]
                dst = buf.at[pl.ds(pl.multiple_of((k * toks + t) * SUBLANES, SUBLANES), SUBLANES), :]
                pltpu.make_async_copy(src, dst, sem).start(priority=k % DMA_PRIORITIES)
            return c

        lax.fori_loop(0, toks, gather_token, 0)

    def finish(buf, sem):
        pltpu.make_async_copy(ys_ref.at[pl.ds(0, buf.shape[0]), :], buf, sem).wait()
        w = rw_ref[...]
        pieces = []
        for s in range(D // LANES):
            acc = jnp.zeros((toks, LANES), F32)
            for k in range(TOP_K):
                rows = buf[pl.ds(k * toks * SUBLANES + s, toks, stride=SUBLANES), :]
                acc = acc + w[:, k:k + 1] * rows
            pieces.append(acc)
        y = base_ref[...] + jnp.concatenate(pieces, axis=-1)
        o_ref[...] = _layer_norm(y, g2_ref[...], b2_ref[...])

    @pl.when(i == 0)
    def _():
        issue(d_ref, buf0, sem0)

    for parity, (cur, csem, nxt, nsem) in enumerate(((buf0, sem0, buf1, sem1), (buf1, sem1, buf0, sem0))):
        @pl.when(jnp.bitwise_and(i, 1) == parity)
        def _(cur=cur, csem=csem, nxt=nxt, nsem=nsem):
            @pl.when(i < last)
            def _():
                issue(dn_ref, nxt, nsem)
            finish(cur, csem)


def _combine(d_flat, ys, base, rw, g2, b2):
    T, D = base.shape
    toks = min(COMB_ROWS, T)
    steps = T // toks
    idx_now = pl.BlockSpec((toks * TOP_K,), lambda i: (i,), memory_space=pltpu.SMEM)
    idx_next = pl.BlockSpec((toks * TOP_K,), lambda i: (jnp.minimum(i + 1, steps - 1),),
                            memory_space=pltpu.SMEM)
    return pl.pallas_call(
        _combine_kernel,
        grid_spec=pltpu.PrefetchScalarGridSpec(
            num_scalar_prefetch=0,
            grid=(steps,),
            in_specs=[
                idx_now, idx_next,
                pl.BlockSpec(memory_space=pl.ANY),
                pl.BlockSpec((toks, D), lambda i: (i, 0)),
                pl.BlockSpec((toks, TOP_K), lambda i: (i, 0)),
                pl.BlockSpec(g2.shape, lambda i: (0, 0)),
                pl.BlockSpec(b2.shape, lambda i: (0, 0)),
            ],
            out_specs=pl.BlockSpec((toks, D), lambda i: (i, 0)),
            scratch_shapes=[
                pltpu.VMEM((TOP_K * toks * SUBLANES, LANES), F32),
                pltpu.VMEM((TOP_K * toks * SUBLANES, LANES), F32),
                pltpu.SemaphoreType.DMA(()),
                pltpu.SemaphoreType.DMA(()),
            ],
        ),
        out_shape=jax.ShapeDtypeStruct((T, D), F32),
        compiler_params=_cparams(("arbitrary",)),
        name="moe_combine",
    )(d_flat, d_flat, ys, base, rw, g2, b2)


def _overlap_matrix(ncp):
    n = np.arange(ncp)[:, None]
    j = np.arange(LANES)[None, :]
    start = n * CMP_STRIDE
    end = start + CMP_LEN - 1
    sel_start = j * SEL_BLOCK
    ovl = (start < sel_start + SEL_BLOCK) & (end >= sel_start)
    return jnp.asarray(ovl.astype(np.float32), dtype=BF16)


def _block_onehot(seq):
    pos = np.arange(seq)[:, None]
    j = np.arange(LANES)[None, :]
    return jnp.asarray((pos // SEL_BLOCK == j).astype(np.float32), dtype=BF16)


def _mixer(x2, batch, seq, w_in, conv_w, cmp_k, cmp_v):
    c3 = 3 * CONV_CH
    qd = N_HEADS * HEAD_DIM
    w_conv = w_in[:, :c3].astype(BF16)
    w_q = w_in[:, c3:c3 + qd].astype(BF16)
    kv0 = c3 + qd
    part = lambda j: w_in[:, kv0 + j * KV_DIM:kv0 + (j + 1) * KV_DIM]
    zero = jnp.zeros((w_in.shape[0], HEAD_DIM), w_in.dtype)

    def per_group(w):
        return [c for g in range(N_KV_HEADS) for c in (w[:, g * HEAD_DIM:(g + 1) * HEAD_DIM], zero)]

    w_kv = jnp.concatenate([part(0), part(1)] + per_group(part(2)) + per_group(part(4))
                           + per_group(part(3)) + per_group(part(5)), axis=1).astype(BF16)
    w_g = jnp.pad(w_in[:, kv0 + 6 * KV_DIM:], ((0, 0), (0, LANES - N_HEADS * N_BRANCH))).astype(BF16)
    conv_out, q, kv, gates = _proj_conv(x2, w_conv, w_q, w_kv, w_g, conv_w, batch, seq)
    kc, vc = _compress(kv[:, :KV_DIM], kv[:, KV_DIM:2 * KV_DIM], cmp_k, cmp_v, batch, seq)
    ncp = -(-kc.shape[1] // LANES) * LANES
    if ncp != kc.shape[1]:
        padn = ((0, 0), (0, ncp - kc.shape[1]), (0, 0))
        kc, vc = jnp.pad(kc, padn), jnp.pad(vc, padn)
    kv3 = kv.reshape(batch, seq, kv.shape[1])
    nsa_out = _nsa(q, kc, vc, kv3, gates, _overlap_matrix(ncp), _block_onehot(seq), batch, seq)
    return conv_out, nsa_out


def _moe(x3, base, e_t, r_t, w_t, counts, w_gate, w_up, w_down, g2, b2):
    T = base.shape[0]
    A = T * TOP_K
    n_blocks = -(-(A + N_EXPERTS * (SLOT_BLOCK - 1)) // SLOT_BLOCK)
    cnt = counts[:, 0]
    padded = (cnt + SLOT_BLOCK - 1) // SLOT_BLOCK * SLOT_BLOCK
    pad_end = jnp.cumsum(padded)
    pad_start = (pad_end - padded).astype(I32)
    zero_off = jnp.where(cnt > 0, pad_start + (cnt - 1) // SLOT_PART * SLOT_PART, -1).astype(I32)
    n_used = (pad_end[-1:] // SLOT_BLOCK).astype(I32)
    blk_start = jnp.arange(n_blocks, dtype=I32) * SLOT_BLOCK
    last_e = jnp.max(jnp.where(padded > 0, jnp.arange(N_EXPERTS, dtype=I32), 0))
    blk_e = jnp.minimum(jnp.sum((pad_end[None, :] <= blk_start[:, None]).astype(I32), axis=1),
                        last_e).astype(I32)
    d_flat = _slot_rows(pad_start, e_t, r_t).T.reshape(A)
    xs = _push(zero_off, d_flat, x3, n_blocks * SLOT_BLOCK)
    of_blk = blk_e[:, None] == jnp.arange(N_EXPERTS, dtype=I32)[None, :]
    blk_cnt = jnp.sum(jnp.where(of_blk, cnt[None, :], 0), axis=1)
    blk_first = jnp.sum(jnp.where(of_blk, pad_start[None, :], 0), axis=1)
    blk_valid = jnp.clip(blk_cnt - (blk_start - blk_first), 0, SLOT_BLOCK)
    blk_parts = ((blk_valid + SLOT_PART - 1) // SLOT_PART).astype(I32)
    ys = _experts(blk_e, n_used, blk_parts, xs, w_gate, w_up, w_down)
    return _combine(d_flat, ys, base, w_t.T, g2, b2)


def kernel(x, w_in, conv_w, ck_pos, ck_w1, ck_b1, ck_w2, cv_pos, cv_w1, cv_b1, cv_w2, w_out, ln1_g, ln1_b, router_w, router_bias, w_gate, w_up, w_down, ws_gate, ws_up, ws_down, ln2_g, ln2_b):
    batch, seq, D = x.shape
    depth = w_in.shape[0]
    alpha = (2.0 * depth) ** 0.25
    x2 = x.reshape(batch * seq, D)
    for l in range(depth):
        cmp_k = _compress_weights(ck_pos[l], ck_w1[l], ck_b1[l], ck_w2[l])
        cmp_v = _compress_weights(cv_pos[l], cv_w1[l], cv_b1[l], cv_w2[l])
        conv_out, nsa_out = _mixer(x2, batch, seq, w_in[l], conv_w[l], cmp_k, cmp_v)
        rw_hi, rw_lo = _split_bf16(router_w[l].T)
        rbias = jnp.broadcast_to(router_bias[l][:, None], (N_EXPERTS, LANES))
        x3, base, e_t, r_t, w_t, counts = _post(
            x2, conv_out, nsa_out, w_out[l].astype(BF16), ln1_g[l][None, :], ln1_b[l][None, :],
            rw_hi, rw_lo, rbias,
            ws_gate[l].astype(BF16), ws_up[l].astype(BF16), ws_down[l].astype(BF16), alpha)
        x2 = _moe(x3, base, e_t, r_t, w_t, counts, w_gate[l], w_up[l], w_down[l],
                  ln2_g[l][None, :], ln2_b[l][None, :])
    return x2.reshape(batch, seq, D)
```
